```python
import jax, jax.numpy as jnp
from jax import lax
import numpy as np

D_MODEL = 4096
BATCH = 2
SEQ = 4096
DEPTH = 2

CTX_LEN = 256
GRID_W = 64
HEAD_DIM = 128
ATTN_HEADS = D_MODEL // (2 * HEAD_DIM)
ATTN_KV_HEADS = 4
ATTN_GROUP = ATTN_HEADS // ATTN_KV_HEADS
ATTN_Q = ATTN_HEADS * HEAD_DIM
ATTN_KV = ATTN_KV_HEADS * HEAD_DIM
WINDOW = 128
ATTN_BLOCK = 128
ROPE_BASE = 10000.0
CMLP_GROUPS = D_MODEL // (4 * HEAD_DIM)
CMLP_CH = HEAD_DIM
CMLP_WIDTH = CMLP_GROUPS * CMLP_CH
CMLP_CHUNK = 128
RWKV_HEAD = 64
RWKV_WIDTH = D_MODEL // 4
RWKV_HEADS = RWKV_WIDTH // RWKV_HEAD
RWKV_LORA = 128
RWKV_GN_EPS = 64e-5
CONV_W = 3
N_DIR = 2
D_MIX = ATTN_Q + CMLP_WIDTH + RWKV_WIDTH
D_PROJ = ATTN_Q + 2 * ATTN_KV + 2 * CMLP_WIDTH + 4 * RWKV_WIDTH
N_EXPERTS = 16
N_EXPERT_GROUPS = 4
EXPERTS_PER_GROUP = N_EXPERTS // N_EXPERT_GROUPS
TOP_K = 2
EXPERT_FF = 1024
N_MOD = 6
EPS = 1e-6

kernel_name = 'hybrid_dit_hymba_rwkv7_moe'


def rmsnorm(x, g):
    xf = x.astype(jnp.float32)
    y = xf * lax.rsqrt(jnp.mean(xf * xf, axis=-1, keepdims=True) + EPS)
    return y.astype(x.dtype) * g


def modulate(z, g, shift, scale):
    return rmsnorm(z, g) * (1 + scale) + shift


def rope_1d(x, pos):
    half = x.shape[-1] // 2
    freqs = ROPE_BASE ** (-jnp.arange(half, dtype=jnp.float32) / half)
    ang = pos.astype(jnp.float32)[:, None] * freqs[None, :]
    cos = jnp.cos(ang)[None, :, None, :].astype(x.dtype)
    sin = jnp.sin(ang)[None, :, None, :].astype(x.dtype)
    x1, x2 = jnp.split(x, 2, axis=-1)
    return jnp.concatenate([x1 * cos - x2 * sin, x2 * cos + x1 * sin], axis=-1)


def axial_rope(x, row, col):
    xr, xc = jnp.split(x, 2, axis=-1)
    return jnp.concatenate([rope_1d(xr, row), rope_1d(xc, col)], axis=-1)


def centred_conv(x, w):
    return lax.conv_general_dilated(x, w[:, None, :], window_strides=(1,),
                                    padding=[(CONV_W // 2, CONV_W // 2)],
                                    dimension_numbers=('NWC', 'WIO', 'NWC'),
                                    feature_group_count=x.shape[-1])


def split_proj(zn, w_in_l, conv_l):
    proj = zn @ w_in_l
    offs = [ATTN_Q, ATTN_Q + ATTN_KV, ATTN_Q + 2 * ATTN_KV, ATTN_Q + 2 * ATTN_KV + CMLP_WIDTH,
            ATTN_Q + 2 * ATTN_KV + 2 * CMLP_WIDTH, ATTN_Q + 2 * ATTN_KV + 2 * CMLP_WIDTH + 3 * RWKV_WIDTH]
    q, k, v, u, gv, rkv, g = jnp.split(proj, offs, axis=-1)
    u, gv = jax.nn.gelu(u), jax.nn.gelu(gv)
    r, kr, vr = jnp.split(centred_conv(rkv, conv_l), 3, axis=-1)
    return q, k, v, u, gv, r, kr, vr, g


def latent_attention(q, k, v, kc, vc, sink):
    B, T = q.shape[:2]
    L = kc.shape[1]
    nb = T // ATTN_BLOCK
    nk = 3 * ATTN_BLOCK
    scale = HEAD_DIM ** -0.5
    qb = q.reshape(B, nb, ATTN_BLOCK, ATTN_KV_HEADS, ATTN_GROUP, HEAD_DIM)

    def band(z):
        zp = jnp.pad(z, ((0, 0), (ATTN_BLOCK, ATTN_BLOCK), (0, 0), (0, 0)))
        zp = zp.reshape(B, nb + 2, ATTN_BLOCK, ATTN_KV_HEADS, HEAD_DIM)
        return jnp.concatenate([zp[:, :-2], zp[:, 1:-1], zp[:, 2:]], axis=2)

    kb, vb = band(k), band(v)
    s_loc = jnp.einsum('bnqhgd,bnkhd->bnhgqk', qb, kb).astype(jnp.float32) * scale
    rel = jnp.arange(nk)[None, :] - ATTN_BLOCK - jnp.arange(ATTN_BLOCK)[:, None]
    key_pos = jnp.arange(nb)[:, None] * ATTN_BLOCK - ATTN_BLOCK + jnp.arange(nk)[None, :]
    mask = (jnp.abs(rel) <= WINDOW)[None] & ((key_pos >= 0) & (key_pos < T))[:, None, :]
    s_loc = jnp.where(mask[None, :, None, None], s_loc, -jnp.inf)
    s_ctx = jnp.einsum('bnqhgd,bkhd->bnhgqk', qb, kc).astype(jnp.float32) * scale
    s_sink = jnp.broadcast_to(sink.reshape(1, 1, ATTN_KV_HEADS, ATTN_GROUP, 1, 1).astype(jnp.float32),
                              s_loc.shape[:-1] + (1,))
    p = jax.nn.softmax(jnp.concatenate([s_loc, s_ctx, s_sink], axis=-1), axis=-1).astype(v.dtype)
    o = (jnp.einsum('bnhgqk,bnkhd->bnqhgd', p[..., :nk], vb)
         + jnp.einsum('bnhgqk,bkhd->bnqhgd', p[..., nk:nk + L], vc))
    return o.reshape(B, T, ATTN_Q)


def context_attention(qc, kc, vc, sink):
    B, L = qc.shape[:2]
    q = qc.reshape(B, L, ATTN_KV_HEADS, ATTN_GROUP, HEAD_DIM)
    s = jnp.einsum('bqhgd,bkhd->bhgqk', q, kc).astype(jnp.float32) * (HEAD_DIM ** -0.5)
    s_sink = jnp.broadcast_to(sink.reshape(1, ATTN_KV_HEADS, ATTN_GROUP, 1, 1).astype(jnp.float32),
                              s.shape[:-1] + (1,))
    p = jax.nn.softmax(jnp.concatenate([s, s_sink], axis=-1), axis=-1)[..., :L].astype(vc.dtype)
    return jnp.einsum('bhgqk,bkhd->bqhgd', p, vc).reshape(B, L, ATTN_Q)


def chunk_mlp(u, gv, norm_g, ws, bs):
    B, T = u.shape[:2]
    nc = T // CMLP_CHUNK
    gv = rmsnorm(gv, norm_g).reshape(B, nc, CMLP_CHUNK, CMLP_GROUPS, CMLP_CH)
    mixed = jnp.einsum('gpq,bnqgc->bnpgc', ws, gv) + bs.T[None, None, :, :, None]
    return (u.reshape(B, nc, CMLP_CHUNK, CMLP_GROUPS, CMLP_CH) * mixed).reshape(B, T, CMLP_WIDTH)


def heads(z):
    return z.reshape(z.shape[:-1] + (RWKV_HEADS, RWKV_HEAD))


def wkv_step(S, inp):
    r_t, w_t, kk_t, b_t, k_t, v_t = inp
    sa = jnp.einsum('zbhvk,zbhk->zbhv', S, -kk_t)
    S = S * w_t[..., None, :] + sa[..., :, None] * b_t[..., None, :] + v_t[..., :, None] * k_t[..., None, :]
    return S, jnp.einsum('zbhvk,zbhk->zbhv', S, r_t)


def time_major(z):
    return jnp.moveaxis(jnp.stack([z[0], jnp.flip(z[1], axis=1)]), 2, 0)


def rwkv_scan(zn, r, k, v, w0, w1, w2, a0, a1, a2, kk_p, ka_p, S0):
    f32 = jnp.float32
    w_raw = w0[:, None, None, :] + jnp.einsum('zbtr,zrc->zbtc', jnp.tanh(jnp.einsum('btd,zdr->zbtr', zn, w1)), w2)
    decay = jnp.exp(-jnp.exp(-jax.nn.softplus(-w_raw.astype(f32)) - 0.5))
    a = jax.nn.sigmoid((a0[:, None, None, :] + jnp.einsum('zbtr,zrc->zbtc', jnp.einsum('btd,zdr->zbtr', zn, a1), a2)).astype(f32))
    kf = k.astype(f32)
    kk = heads(kf * kk_p)
    kk = kk * lax.rsqrt(jnp.sum(kk * kk, axis=-1, keepdims=True) + 1e-12)
    a_h = heads(a)
    k_rep = heads(kf)[None] * (1 + (a_h - 1) * heads(ka_p))
    b = kk[None] * a_h
    shp = a_h.shape
    xs = (time_major(jnp.broadcast_to(heads(r.astype(f32))[None], shp)), time_major(heads(decay)),
          time_major(jnp.broadcast_to(kk[None], shp)), time_major(b), time_major(k_rep),
          time_major(jnp.broadcast_to(heads(v.astype(f32))[None], shp)))
    S, ys = lax.scan(wkv_step, S0, xs)
    ys = jnp.moveaxis(ys, 0, 2)
    return ys[0] + jnp.flip(ys[1], axis=1), S


def rwkv_output(y, r, k, v, g, rk, ln_w, ln_b):
    B, T = r.shape[:2]
    mu = jnp.mean(y, axis=-1, keepdims=True)
    var = jnp.mean(jnp.square(y - mu), axis=-1, keepdims=True)
    yn = ((y - mu) * lax.rsqrt(var + RWKV_GN_EPS)).reshape(B, T, RWKV_WIDTH) * ln_w + ln_b
    bonus = (jnp.sum(heads(r) * heads(k) * heads(rk), axis=-1, keepdims=True) * heads(v)).reshape(B, T, RWKV_WIDTH)
    return ((yn + bonus) * jax.nn.sigmoid(g)).astype(r.dtype)


def moe(zn, router_w, router_b, w1, w3, w2):
    shp = zn.shape
    z = zn.reshape(-1, D_MODEL)
    scores = jax.nn.sigmoid((z @ router_w).astype(jnp.float32))
    sel = scores + router_b.astype(jnp.float32)
    grp_score = jnp.sum(lax.top_k(sel.reshape(-1, N_EXPERT_GROUPS, EXPERTS_PER_GROUP), TOP_K)[0], axis=-1)
    best = jnp.argmax(grp_score, axis=-1)
    in_grp = (jnp.arange(N_EXPERTS) // EXPERTS_PER_GROUP)[None, :] == best[:, None]
    _, top_idx = lax.top_k(jnp.where(in_grp, sel, -jnp.inf), TOP_K)
    wts = jnp.take_along_axis(scores, top_idx, axis=-1)
    wts = wts / jnp.sum(wts, axis=-1, keepdims=True)
    gates = jnp.sum(jax.nn.one_hot(top_idx, N_EXPERTS, dtype=jnp.float32) * wts[..., None], axis=1).astype(z.dtype)
    hid = jax.nn.silu(jnp.einsum('nd,edf->nef', z, w1)) * jnp.einsum('nd,edf->nef', z, w3)
    hid = (hid * gates[:, :, None]).reshape(-1, N_EXPERTS * EXPERT_FF)
    return (hid @ w2.reshape(N_EXPERTS * EXPERT_FF, D_MODEL)).reshape(shp)


def setup_inputs(seed: int = 0) -> dict:
    key = jax.random.key(seed)
    ks = jax.random.split(key, 32)

    def nrm(i, shape, s):
        return jax.random.normal(ks[i], shape, jnp.float32) * s

    conv_base = jnp.zeros((CONV_W, 1), jnp.float32).at[CONV_W // 2].set(1.0)
    return {
        'x': nrm(0, (BATCH, SEQ, D_MODEL), 1.0),
        'c': nrm(1, (BATCH, D_MODEL), 1.0),
        'ctx': nrm(2, (BATCH, CTX_LEN, D_MODEL), 1.0),
        'c_ctx': nrm(3, (D_MODEL,), 1.0),
        'ada_w': nrm(4, (DEPTH, D_MODEL, N_MOD * D_MODEL), 0.5 * D_MODEL ** -0.5),
        'ada_b': nrm(5, (DEPTH, N_MOD * D_MODEL), 0.01),
        'norm1_g': 1.0 + nrm(6, (DEPTH, D_MODEL), 0.05),
        'w_in': nrm(7, (DEPTH, D_MODEL, D_PROJ), D_MODEL ** -0.5),
        'rwkv_conv': conv_base + nrm(8, (DEPTH, CONV_W, 3 * RWKV_WIDTH), 0.3),
        'attn_sink': nrm(9, (DEPTH, ATTN_HEADS), 0.5),
        'cmlp_norm_g': 1.0 + nrm(10, (DEPTH, CMLP_WIDTH), 0.05),
        'cmlp_ws': nrm(11, (DEPTH, CMLP_GROUPS, CMLP_CHUNK, CMLP_CHUNK), CMLP_CHUNK ** -0.5),
        'cmlp_b': 1.0 + nrm(12, (DEPTH, CMLP_GROUPS, CMLP_CHUNK), 0.1),
        'rwkv_w0': nrm(13, (DEPTH, N_DIR, RWKV_WIDTH), 1.0),
        'rwkv_w1': nrm(14, (DEPTH, N_DIR, D_MODEL, RWKV_LORA), D_MODEL ** -0.5),
        'rwkv_w2': nrm(15, (DEPTH, N_DIR, RWKV_LORA, RWKV_WIDTH), 0.5 * RWKV_LORA ** -0.5),
        'rwkv_a0': nrm(16, (DEPTH, N_DIR, RWKV_WIDTH), 0.5),
        'rwkv_a1': nrm(17, (DEPTH, N_DIR, D_MODEL, RWKV_LORA), D_MODEL ** -0.5),
        'rwkv_a2': nrm(18, (DEPTH, N_DIR, RWKV_LORA, RWKV_WIDTH), 0.5 * RWKV_LORA ** -0.5),
        'rwkv_kk': 1.0 + nrm(19, (DEPTH, RWKV_WIDTH), 0.1),
        'rwkv_ka': 1.0 + nrm(20, (DEPTH, RWKV_WIDTH), 0.1),
        'rwkv_rk': nrm(21, (DEPTH, RWKV_WIDTH), 0.3),
        'rwkv_ln_w': 1.0 + nrm(22, (DEPTH, RWKV_WIDTH), 0.05),
        'rwkv_ln_b': nrm(23, (DEPTH, RWKV_WIDTH), 0.01),
        'w_out': nrm(24, (DEPTH, D_MIX, D_MODEL), D_MIX ** -0.5),
        'norm2_g': 1.0 + nrm(25, (DEPTH, D_MODEL), 0.05),
        'router_w': nrm(26, (D_MODEL, N_EXPERTS), D_MODEL ** -0.5),
        'router_b': nrm(27, (N_EXPERTS,), 0.01),
        'moe_w1': nrm(28, (DEPTH, N_EXPERTS, D_MODEL, EXPERT_FF), D_MODEL ** -0.5),
        'moe_w3': nrm(29, (DEPTH, N_EXPERTS, D_MODEL, EXPERT_FF), D_MODEL ** -0.5),
        'moe_w2': nrm(30, (DEPTH, N_EXPERTS, EXPERT_FF, D_MODEL), EXPERT_FF ** -0.5),
        'final_g': 1.0 + nrm(31, (D_MODEL,), 0.05),
    }


def reference(x, c, ctx, c_ctx, ada_w, ada_b, norm1_g, w_in, rwkv_conv, attn_sink, cmlp_norm_g, cmlp_ws, cmlp_b,
              rwkv_w0, rwkv_w1, rwkv_w2, rwkv_a0, rwkv_a1, rwkv_a2, rwkv_kk, rwkv_ka, rwkv_rk, rwkv_ln_w, rwkv_ln_b,
              w_out, norm2_g, router_w, router_b, moe_w1, moe_w3, moe_w2, final_g):
    B, T, _ = x.shape
    L = ctx.shape[1]
    rows = T // GRID_W
    row = jnp.repeat(jnp.arange(rows), GRID_W)
    col = jnp.tile(jnp.arange(GRID_W), rows)
    s_zero = jnp.zeros((N_DIR, B, RWKV_HEADS, RWKV_HEAD, RWKV_HEAD), jnp.float32)
    h = ctx
    for l in range(DEPTH):
        mod_x = jax.nn.silu(c) @ ada_w[l] + ada_b[l]
        mod_c = jax.nn.silu(c_ctx) @ ada_w[l] + ada_b[l]
        sh1x, sc1x, g1x, sh2x, sc2x, g2x = [m[:, None, :] for m in jnp.split(mod_x, N_MOD, axis=-1)]
        sh1c, sc1c, g1c, sh2c, sc2c, g2c = jnp.split(mod_c, N_MOD, axis=-1)
        xn = modulate(x, norm1_g[l], sh1x, sc1x)
        hn = modulate(h, norm1_g[l], sh1c, sc1c)
        qx, kx, vx, ux, gvx, rx, krx, vrx, gx = split_proj(xn, w_in[l], rwkv_conv[l])
        qc, kc, vc, uc, gvc, rc, krc, vrc, gc = split_proj(hn, w_in[l], rwkv_conv[l])
        kc = kc.reshape(B, L, ATTN_KV_HEADS, HEAD_DIM)
        vc = vc.reshape(B, L, ATTN_KV_HEADS, HEAD_DIM)
        qx = axial_rope(qx.reshape(B, T, ATTN_HEADS, HEAD_DIM), row, col)
        kx = axial_rope(kx.reshape(B, T, ATTN_KV_HEADS, HEAD_DIM), row, col)
        vx = vx.reshape(B, T, ATTN_KV_HEADS, HEAD_DIM)
        attn_x = latent_attention(qx, kx, vx, kc, vc, attn_sink[l])
        cmlp_x = chunk_mlp(ux, gvx, cmlp_norm_g[l], cmlp_ws[l], cmlp_b[l])
        rw = (rwkv_w0[l], rwkv_w1[l], rwkv_w2[l], rwkv_a0[l], rwkv_a1[l], rwkv_a2[l], rwkv_kk[l], rwkv_ka[l])
        y_c, s_ctx = rwkv_scan(hn, rc, krc, vrc, *rw, s_zero)
        y_x, _ = rwkv_scan(xn, rx, krx, vrx, *rw, s_ctx)
        rwkv_x = rwkv_output(y_x, rx, krx, vrx, gx, rwkv_rk[l], rwkv_ln_w[l], rwkv_ln_b[l])
        x = x + g1x * (jnp.concatenate([attn_x, cmlp_x, rwkv_x], axis=-1) @ w_out[l])
        x = x + g2x * moe(modulate(x, norm2_g[l], sh2x, sc2x), router_w, router_b, moe_w1[l], moe_w3[l], moe_w2[l])
        if l < DEPTH - 1:
            attn_c = context_attention(qc, kc, vc, attn_sink[l])
            cmlp_c = chunk_mlp(uc, gvc, cmlp_norm_g[l], cmlp_ws[l], cmlp_b[l])
            rwkv_c = rwkv_output(y_c, rc, krc, vrc, gc, rwkv_rk[l], rwkv_ln_w[l], rwkv_ln_b[l])
            h = h + g1c * (jnp.concatenate([attn_c, cmlp_c, rwkv_c], axis=-1) @ w_out[l])
            h = h + g2c * moe(modulate(h, norm2_g[l], sh2c, sc2c), router_w, router_b, moe_w1[l], moe_w3[l], moe_w2[l])
    return rmsnorm(x, final_g)
```

```python
import functools

import jax
import jax.numpy as jnp
from jax import lax
from jax.experimental import pallas as pl
from jax.experimental.pallas import tpu as pltpu

F32, BF16 = jnp.float32, jnp.bfloat16
HIGHEST = lax.Precision.HIGHEST

HEAD_DIM = 128
KV_HEADS = 4
WINDOW = 128
ATTN_BLOCK = 128
GRID_W = 64
ROPE_BASE = 10000.0
CMLP_CH = 128
CMLP_CHUNK = 128
RWKV_HEAD = 64
RWKV_GN_EPS = 64e-5
N_EXPERT_GROUPS = 4
TOP_K = 2
N_MOD = 6
EPS = 1e-6
MASKED = -1e30

MOD_ROWS = 8
SCAN_CHUNK = 64
SCAN_HEADS = 4
VMEM_LIMIT_BYTES = 56 * 1024 * 1024


def _params(*sem):
    return pltpu.CompilerParams(dimension_semantics=sem, vmem_limit_bytes=VMEM_LIMIT_BYTES)


def _dot(a, b, **kw):
    return jnp.dot(a, b, preferred_element_type=F32, **kw)


def _dot_nt(a, b, **kw):
    return lax.dot_general(a, b, (((1,), (1,)), ((), ())), preferred_element_type=F32, **kw)


def _dot_tn(a, b, **kw):
    return lax.dot_general(a, b, (((0,), (0,)), ((), ())), preferred_element_type=F32, **kw)


def _iota(shape, dim):
    return lax.broadcasted_iota(jnp.int32, shape, dim)


def _ada_body(c_ref, w_ref, b_ref, o_ref):
    c = c_ref[...]
    a = (c * jax.nn.sigmoid(c)).astype(BF16)
    o_ref[...] = _dot(a, w_ref[...].astype(BF16)) + b_ref[...]


def _ada(cpad, ada_w, ada_b):
    depth, d, n = ada_w.shape
    tn = 512
    return pl.pallas_call(
        _ada_body,
        grid=(depth, n // tn),
        in_specs=[pl.BlockSpec((MOD_ROWS, d), lambda l, j: (0, 0)),
                  pl.BlockSpec((None, d, tn), lambda l, j: (l, 0, j)),
                  pl.BlockSpec((None, 1, tn), lambda l, j: (l, 0, j))],
        out_specs=pl.BlockSpec((None, MOD_ROWS, tn), lambda l, j: (l, 0, j)),
        out_shape=jax.ShapeDtypeStruct((depth, MOD_ROWS, n), F32),
        compiler_params=_params("parallel", "parallel"),
    )(cpad, ada_w, ada_b.reshape(depth, 1, n))


def _mod_row(i, tm, rows_per_mod, fixed_row):
    return fixed_row if rows_per_mod is None else (i * tm) // rows_per_mod


def _modulated_norm(x, g, shift, scale):
    y = x * lax.rsqrt(jnp.mean(x * x, axis=-1, keepdims=True) + EPS) * g
    return y * (1.0 + scale) + shift


def _proj_body(x_ref, g_ref, sh_ref, sc_ref, w_ref, o_ref, xn_ref, *, tm, rows_per_mod, fixed_row):
    @pl.when(pl.program_id(1) == 0)
    def _():
        r = _mod_row(pl.program_id(0), tm, rows_per_mod, fixed_row)
        xn = _modulated_norm(x_ref[...], g_ref[...], sh_ref[pl.ds(r, 1), :], sc_ref[pl.ds(r, 1), :])
        xn_ref[...] = xn.astype(BF16)

    o_ref[...] = _dot(xn_ref[...], w_ref[...])


def _proj(x2d, g, mod, w, layer, rows_per_mod, fixed_row):
    n, d = x2d.shape
    dp = w.shape[1]
    tm = min(512, n)
    tn = 512
    body = functools.partial(_proj_body, tm=tm, rows_per_mod=rows_per_mod, fixed_row=fixed_row)
    return pl.pallas_call(
        body,
        grid=(n // tm, dp // tn),
        in_specs=[pl.BlockSpec((tm, d), lambda i, j: (i, 0)),
                  pl.BlockSpec((None, 1, d), lambda i, j: (layer, 0, 0)),
                  pl.BlockSpec((None, MOD_ROWS, d), lambda i, j: (layer, 0, 0)),
                  pl.BlockSpec((None, MOD_ROWS, d), lambda i, j: (layer, 0, 1)),
                  pl.BlockSpec((d, tn), lambda i, j: (0, j))],
        out_specs=pl.BlockSpec((tm, tn), lambda i, j: (i, j)),
        out_shape=jax.ShapeDtypeStruct((n, dp), F32),
        scratch_shapes=[pltpu.VMEM((tm, d), BF16)],
        compiler_params=_params("parallel", "arbitrary"),
    )(x2d, g, mod, mod, w)


def _rope(x, cos, sin_signed):
    lane = _iota(x.shape, 1)
    swapped = jnp.where((lane % 64) < 32, pltpu.roll(x, 96, axis=1), pltpu.roll(x, 32, axis=1))
    return x * cos + swapped * sin_signed


def _softmax_pv(parts, sink_col, vall):
    m = sink_col
    for s in parts:
        m = jnp.maximum(m, jnp.max(s, axis=-1, keepdims=True))
    ps = [jnp.exp(s - m) for s in parts]
    denom = jnp.exp(sink_col - m)
    for p in ps:
        denom = denom + jnp.sum(p, axis=-1, keepdims=True)
    p = ps[0] if len(ps) == 1 else jnp.concatenate(ps, axis=1)
    return _dot(p.astype(BF16), vall) / denom


def _attn_body(q_ref, kp_ref, kc_ref, kn_ref, vp_ref, vc_ref, vn_ref, kx_ref, vx_ref,
               cp_ref, cc_ref, cn_ref, sp_ref, sc_ref, sn_ref, sink_ref, o_ref, *, nb, group):
    n = pl.program_id(1)
    blk = ATTN_BLOCK
    cos = (cp_ref[...], cc_ref[...], cn_ref[...])
    sin = (sp_ref[...], sc_ref[...], sn_ref[...])
    qi = _iota((group * blk, 3 * blk), 0) % blk
    kj = _iota((group * blk, 3 * blk), 1)
    in_seq = ((kj >= blk) | (n > 0)) & ((kj < 2 * blk) | (n < nb - 1))
    band_ok = (jnp.abs(kj - blk - qi) <= WINDOW) & in_seq
    scale = HEAD_DIM ** -0.5
    for h in range(KV_HEADS):
        hs = slice(h * HEAD_DIM, (h + 1) * HEAD_DIM)
        kb = [_rope(r[:, hs], c, s) for r, c, s in zip((kp_ref, kc_ref, kn_ref), cos, sin)]
        kall = jnp.concatenate(kb + [kx_ref[:, hs]], axis=0).astype(BF16)
        vall = jnp.concatenate([vp_ref[:, hs], vc_ref[:, hs], vn_ref[:, hs], vx_ref[:, hs]], axis=0).astype(BF16)
        qs = []
        for g in range(group):
            c0 = (h * group + g) * HEAD_DIM
            qs.append(_rope(q_ref[:, c0:c0 + HEAD_DIM], cos[1], sin[1]))
        qh = jnp.concatenate(qs, axis=0).astype(BF16)
        s = _dot_nt(qh, kall) * scale
        s_loc = jnp.where(band_ok, s[:, :3 * blk], MASKED)
        o = _softmax_pv([s_loc, s[:, 3 * blk:]], sink_ref[h], vall)
        for g in range(group):
            c0 = (h * group + g) * HEAD_DIM
            o_ref[:, c0:c0 + HEAD_DIM] = o[g * blk:(g + 1) * blk].astype(o_ref.dtype)


def _latent_attention(px, pc, cos_t, sin_t, sink_col, b, t, l, aq, akv):
    blk = ATTN_BLOCK
    nb = t // blk
    group = aq // akv
    kcol, vcol = aq // akv, aq // akv + 1
    prev = lambda n: jnp.maximum(n - 1, 0)
    nxt = lambda n: jnp.minimum(n + 1, nb - 1)
    kv_spec = lambda col, f: pl.BlockSpec((blk, akv), lambda bi, n: (bi * nb + f(n), col))
    tab_spec = lambda f: pl.BlockSpec((blk, HEAD_DIM), lambda bi, n: (f(n), 0))
    ident = lambda n: n
    body = functools.partial(_attn_body, nb=nb, group=group)
    return pl.pallas_call(
        body,
        grid=(b, nb),
        in_specs=[pl.BlockSpec((blk, aq), lambda bi, n: (bi * nb + n, 0)),
                  kv_spec(kcol, prev), kv_spec(kcol, ident), kv_spec(kcol, nxt),
                  kv_spec(vcol, prev), kv_spec(vcol, ident), kv_spec(vcol, nxt),
                  pl.BlockSpec((l, akv), lambda bi, n: (bi, kcol)),
                  pl.BlockSpec((l, akv), lambda bi, n: (bi, vcol)),
                  tab_spec(prev), tab_spec(ident), tab_spec(nxt),
                  tab_spec(prev), tab_spec(ident), tab_spec(nxt),
                  pl.BlockSpec((KV_HEADS, group * blk, 1), lambda bi, n: (0, 0, 0))],
        out_specs=pl.BlockSpec((blk, aq), lambda bi, n: (bi * nb + n, 0)),
        out_shape=jax.ShapeDtypeStruct((b * t, aq), BF16),
        compiler_params=_params("parallel", "parallel"),
    )(px, px, px, px, px, px, px, pc, pc, cos_t, cos_t, cos_t, sin_t, sin_t, sin_t, sink_col)


def _ctx_attn_body(q_ref, k_ref, v_ref, sink_ref, o_ref, *, group):
    scale = HEAD_DIM ** -0.5
    for h in range(KV_HEADS):
        hs = slice(h * HEAD_DIM, (h + 1) * HEAD_DIM)
        kall = k_ref[:, hs].astype(BF16)
        vall = v_ref[:, hs].astype(BF16)
        for g in range(group):
            c0 = (h * group + g) * HEAD_DIM
            s = _dot_nt(q_ref[:, c0:c0 + HEAD_DIM].astype(BF16), kall) * scale
            o = _softmax_pv([s], sink_ref[h * group + g], vall)
            o_ref[:, c0:c0 + HEAD_DIM] = o.astype(o_ref.dtype)


def _context_attention(pc, sink_rows, b, l, aq, akv):
    group = aq // akv
    kcol, vcol = aq // akv, aq // akv + 1
    return pl.pallas_call(
        functools.partial(_ctx_attn_body, group=group),
        grid=(b,),
        in_specs=[pl.BlockSpec((l, aq), lambda bi: (bi, 0)),
                  pl.BlockSpec((l, akv), lambda bi: (bi, kcol)),
                  pl.BlockSpec((l, akv), lambda bi: (bi, vcol)),
                  pl.BlockSpec((KV_HEADS * group, l, 1), lambda bi: (0, 0, 0))],
        out_specs=pl.BlockSpec((l, aq), lambda bi: (bi, 0)),
        out_shape=jax.ShapeDtypeStruct((b * l, aq), BF16),
        compiler_params=_params("parallel"),
    )(pc, pc, pc, sink_rows)


def _cmlp_body(u_ref, gv_ref, g_ref, ws_ref, bs_ref, o_ref, *, groups):
    u = jax.nn.gelu(u_ref[...])
    gv = jax.nn.gelu(gv_ref[...])
    gvn = gv * lax.rsqrt(jnp.mean(gv * gv, axis=-1, keepdims=True) + EPS) * g_ref[...]
    for gi in range(groups):
        cs = slice(gi * CMLP_CH, (gi + 1) * CMLP_CH)
        mixed = _dot(ws_ref[gi].astype(BF16), gvn[:, cs].astype(BF16)) + bs_ref[gi]
        o_ref[:, cs] = (u[:, cs] * mixed).astype(o_ref.dtype)


def _chunk_mlp(p, norm_g, ws, bs_b, ucol, cw):
    n = p.shape[0]
    groups = cw // CMLP_CH
    ch = CMLP_CHUNK
    return pl.pallas_call(
        functools.partial(_cmlp_body, groups=groups),
        grid=(n // ch,),
        in_specs=[pl.BlockSpec((ch, cw), lambda i: (i, ucol)),
                  pl.BlockSpec((ch, cw), lambda i: (i, ucol + 1)),
                  pl.BlockSpec((1, cw), lambda i: (0, 0)),
                  pl.BlockSpec((groups, ch, ch), lambda i: (0, 0, 0)),
                  pl.BlockSpec((groups, ch, CMLP_CH), lambda i: (0, 0, 0))],
        out_specs=pl.BlockSpec((ch, cw), lambda i: (i, 0)),
        out_shape=jax.ShapeDtypeStruct((n, cw), BF16),
        compiler_params=_params("parallel"),
    )(p, p, norm_g, ws, bs_b)


def _head_sum(x):
    ones = (_iota((128, 128), 0) // RWKV_HEAD == _iota((128, 128), 1) // RWKV_HEAD).astype(F32)
    cols = [_dot(x[:, s * 128:(s + 1) * 128], ones, precision=HIGHEST) for s in range(x.shape[1] // 128)]
    return cols[0] if len(cols) == 1 else jnp.concatenate(cols, axis=1)


def _prep_body(r_ref, k_ref, v_ref, rp_ref, kp_ref, vp_ref, rn_ref, kn_ref, vn_ref,
               cr_ref, ck_ref, cv_ref, hw_ref, ha_ref, w2_ref, a2_ref, w0_ref, a0_ref, kkp_ref, kap_ref,
               ro_ref, ko_ref, vo_ref, kko_ref, lw_ref, bo_ref, kr_ref, *, tr, seq, lora):
    i = pl.program_id(0)
    first = (i * tr) % seq == 0
    last = ((i + 1) * tr) % seq == 0
    row = _iota(r_ref.shape, 0)

    def conv(x_ref, xp_ref, xn_ref, w_ref):
        x = x_ref[...]
        before = jnp.where(first, 0.0, xp_ref[7:8, :])
        after = jnp.where(last, 0.0, xn_ref[0:1, :])
        xm = jnp.where(row == 0, before, pltpu.roll(x, 1, axis=0))
        xp = jnp.where(row == tr - 1, after, pltpu.roll(x, tr - 1, axis=0))
        return xm * w_ref[0:1, :] + x * w_ref[1:2, :] + xp * w_ref[2:3, :]

    r = conv(r_ref, rp_ref, rn_ref, cr_ref)
    k = conv(k_ref, kp_ref, kn_ref, ck_ref)
    v = conv(v_ref, vp_ref, vn_ref, cv_ref)
    kk = k * kkp_ref[...]
    kk = kk * lax.rsqrt(_head_sum(kk * kk) + 1e-12)
    ro_ref[...] = r
    ko_ref[...] = k
    vo_ref[...] = v
    kko_ref[...] = kk
    for z in range(2):
        zs = slice(z * lora, (z + 1) * lora)
        w_raw = w0_ref[z:z + 1, :] + _dot(jnp.tanh(hw_ref[:, zs]), w2_ref[z], precision=HIGHEST)
        softplus_neg = jnp.maximum(-w_raw, 0.0) + jnp.log1p(jnp.exp(-jnp.abs(w_raw)))
        lw_ref[z] = -jnp.exp(-softplus_neg - 0.5)
        a = jax.nn.sigmoid(a0_ref[z:z + 1, :] + _dot(ha_ref[:, zs], a2_ref[z], precision=HIGHEST))
        kr_ref[z] = k * (1.0 + (a - 1.0) * kap_ref[...])
        bo_ref[z] = kk * a


def _rwkv_prep(p, conv_w, w2, a2, w0, a0, kk_p, ka_p, seq, rcol, hcol, rw, lora):
    n = p.shape[0]
    tr = min(256, seq)
    nh = n // 8
    body = functools.partial(_prep_body, tr=tr, seq=seq, lora=lora)
    cur = lambda c: pl.BlockSpec((tr, rw), lambda i: (i, rcol + c))
    prv = lambda c: pl.BlockSpec((8, rw), lambda i: (jnp.maximum(i * (tr // 8) - 1, 0), rcol + c))
    nxt = lambda c: pl.BlockSpec((8, rw), lambda i: (jnp.minimum((i + 1) * (tr // 8), nh - 1), rcol + c))
    cw = lambda c: pl.BlockSpec((3, rw), lambda i: (0, c))
    full2 = lambda shape: pl.BlockSpec(shape, lambda i: (0,) * len(shape))
    shared = pl.BlockSpec((tr, rw), lambda i: (i, 0))
    directed = pl.BlockSpec((2, tr, rw), lambda i: (0, i, 0))
    return pl.pallas_call(
        body,
        grid=(n // tr,),
        in_specs=[cur(0), cur(1), cur(2), prv(0), prv(1), prv(2), nxt(0), nxt(1), nxt(2),
                  cw(0), cw(1), cw(2),
                  pl.BlockSpec((tr, 2 * lora), lambda i: (i, hcol)),
                  pl.BlockSpec((tr, 2 * lora), lambda i: (i, hcol + 1)),
                  full2((2, lora, rw)), full2((2, lora, rw)), full2((2, rw)), full2((2, rw)),
                  full2((1, rw)), full2((1, rw))],
        out_specs=[shared, shared, shared, shared, directed, directed, directed],
        out_shape=[jax.ShapeDtypeStruct((n, rw), F32)] * 4 + [jax.ShapeDtypeStruct((2, n, rw), F32)] * 3,
        compiler_params=_params("parallel"),
    )(p, p, p, p, p, p, p, p, p, conv_w, conv_w, conv_w, p, p, w2, a2, w0, a0, kk_p, ka_p)


def _unit_tri_inverse(nmat, eye, same16, same32, hp):
    n16 = jnp.where(same16, nmat, 0.0)
    x = eye - n16
    pw = n16
    for _ in range(3):
        pw = _dot(pw, pw, precision=hp)
        x = x + _dot(x, pw, precision=hp)
    for off in (jnp.where(same32 & ~same16, nmat, 0.0), jnp.where(~same32, nmat, 0.0)):
        x = x - _dot(_dot(x, off, precision=hp), x, precision=hp)
    return x


def _scan_body(r_ref, kk_ref, v_ref, lw_ref, b_ref, k_ref, s0_ref, y_ref, sf_ref, st_ref, *, heads, nchunks):
    z = pl.program_id(0)
    c = pl.program_id(3)
    C, K = SCAN_CHUNK, RWKV_HEAD
    hp = HIGHEST

    @pl.when(c == 0)
    def _():
        st_ref[...] = s0_ref[...]

    ti, si = _iota((C, C), 0), _iota((C, C), 1)
    before = (si - ti) * (1 - 2 * z) < 0
    upto = before | (si == ti)
    eye = (si == ti).astype(F32)
    same16 = (ti // 16) == (si // 16)
    same32 = (ti // 32) == (si // 32)

    lw = lw_ref[...]
    lc = _dot(upto.astype(F32), lw, precision=hp)
    ltot = jnp.sum(lw, axis=0, keepdims=True)
    e_neg = jnp.exp(-lc)
    e_out = jnp.exp(ltot - lc)
    kkt = kk_ref[...] * jnp.exp(lc - lw)
    rt = r_ref[...] * jnp.exp(lc)
    bt = b_ref[...] * e_neg
    kt = k_ref[...] * e_neg
    bh = b_ref[...] * e_out
    kh = k_ref[...] * e_out
    etot = jnp.exp(ltot)
    v = v_ref[...]

    ys = []
    for h in range(heads):
        ls = slice(h * K, (h + 1) * K)
        p = _dot_nt(jnp.concatenate([kkt[:, ls], rt[:, ls]], axis=0),
                    jnp.concatenate([bt[:, ls], kt[:, ls]], axis=0), precision=hp)
        nmat = jnp.where(before, p[:C, :C], 0.0)
        pkk = jnp.where(before, p[:C, C:], 0.0)
        prb = jnp.where(upto, p[C:, :C], 0.0)
        prk = jnp.where(upto, p[C:, C:], 0.0)
        tinv = _unit_tri_inverse(nmat, eye, same16, same32, hp)
        tg = _dot(tinv, jnp.concatenate([kkt[:, ls], pkk], axis=1), precision=hp)
        qa = jnp.concatenate([rt[:, ls], prk], axis=1) - _dot(prb, tg, precision=hp)
        m3 = _dot_tn(bh[:, ls], tg, precision=hp)
        sv = jnp.concatenate([st_ref[h], v[:, ls]], axis=0)
        ys.append(_dot(qa, sv, precision=hp))
        decay_diag = eye * jnp.broadcast_to(etot[:, ls], (K, K))
        trans = jnp.concatenate([decay_diag, jnp.zeros((K, C), F32)], axis=1) - m3
        st_ref[h] = _dot(trans, sv, precision=hp) + _dot_tn(kh[:, ls], v[:, ls], precision=hp)
    y_ref[...] = ys[0] if heads == 1 else jnp.concatenate(ys, axis=1)

    @pl.when(c == nchunks - 1)
    def _():
        sf_ref[...] = st_ref[...]


def _rwkv_scan(r, kk, v, lw, bb, kr, s0, b, seq, rw):
    C, K = SCAN_CHUNK, RWKV_HEAD
    nchunks = seq // C
    heads = min(SCAN_HEADS, rw // K)
    ngroups = rw // (heads * K)
    n = b * seq
    row = lambda z, bi, hg, c: bi * nchunks + c + z * (nchunks - 1 - 2 * c)
    shared = pl.BlockSpec((C, heads * K), lambda z, bi, hg, c: (row(z, bi, hg, c), hg))
    directed = pl.BlockSpec((None, C, heads * K), lambda z, bi, hg, c: (z, row(z, bi, hg, c), hg))
    state = pl.BlockSpec((None, None, heads, K, K), lambda z, bi, hg, c: (z, bi, hg, 0, 0))
    body = functools.partial(_scan_body, heads=heads, nchunks=nchunks)
    return pl.pallas_call(
        body,
        grid=(2, b, ngroups, nchunks),
        in_specs=[shared, shared, shared, directed, directed, directed, state],
        out_specs=[directed, state],
        out_shape=[jax.ShapeDtypeStruct((2, n, rw), F32),
                   jax.ShapeDtypeStruct((2, b, rw // K, K, K), F32)],
        scratch_shapes=[pltpu.VMEM((heads, K, K), F32)],
        compiler_params=_params("parallel", "parallel", "parallel", "arbitrary"),
    )(r, kk, v, lw, bb, kr, s0)


def _rwkv_out_body(yf_ref, yb_ref, r_ref, k_ref, v_ref, g_ref, rk_ref, lnw_ref, lnb_ref, o_ref):
    y = yf_ref[...] + yb_ref[...]
    inv = 1.0 / RWKV_HEAD
    mu = _head_sum(y) * inv
    d = y - mu
    var = _head_sum(d * d) * inv
    yn = d * lax.rsqrt(var + RWKV_GN_EPS) * lnw_ref[...] + lnb_ref[...]
    bonus = _head_sum(r_ref[...] * k_ref[...] * rk_ref[...]) * v_ref[...]
    o_ref[...] = ((yn + bonus) * jax.nn.sigmoid(g_ref[...])).astype(o_ref.dtype)


def _rwkv_output(y, r, k, v, p, gcol, rk, ln_w, ln_b):
    n, rw = r.shape
    tr = min(256, n)
    shared = pl.BlockSpec((tr, rw), lambda i: (i, 0))
    vec = pl.BlockSpec((1, rw), lambda i: (0, 0))
    return pl.pallas_call(
        _rwkv_out_body,
        grid=(n // tr,),
        in_specs=[pl.BlockSpec((None, tr, rw), lambda i: (0, i, 0)),
                  pl.BlockSpec((None, tr, rw), lambda i: (1, i, 0)),
                  shared, shared, shared,
                  pl.BlockSpec((tr, rw), lambda i: (i, gcol)),
                  vec, vec, vec],
        out_specs=shared,
        out_shape=jax.ShapeDtypeStruct((n, rw), BF16),
        compiler_params=_params("parallel"),
    )(y, y, r, k, v, p, rk, ln_w, ln_b)


def _wout_body(a_ref, c_ref, r_ref, wa_ref, wc_ref, wr_ref, x_ref, g_ref, o_ref, *, tm, rows_per_mod, fixed_row):
    row = _mod_row(pl.program_id(0), tm, rows_per_mod, fixed_row)
    acc = _dot(a_ref[...], wa_ref[...]) + _dot(c_ref[...], wc_ref[...]) + _dot(r_ref[...], wr_ref[...])
    o_ref[...] = x_ref[...] + g_ref[pl.ds(row, 1), :] * acc


def _out_proj(attn, cmlp, rwkv, w_out, x2d, mod, layer, rows_per_mod, fixed_row):
    n, d = x2d.shape
    aq, cw, rw = attn.shape[1], cmlp.shape[1], rwkv.shape[1]
    tm = min(512, n)
    tn = min(1024, d)
    body = functools.partial(_wout_body, tm=tm, rows_per_mod=rows_per_mod, fixed_row=fixed_row)
    return pl.pallas_call(
        body,
        grid=(n // tm, d // tn),
        in_specs=[pl.BlockSpec((tm, aq), lambda i, j: (i, 0)),
                  pl.BlockSpec((tm, cw), lambda i, j: (i, 0)),
                  pl.BlockSpec((tm, rw), lambda i, j: (i, 0)),
                  pl.BlockSpec((None, aq, tn), lambda i, j: (layer, 0, j)),
                  pl.BlockSpec((None, cw, tn), lambda i, j: (layer, aq // cw, j)),
                  pl.BlockSpec((None, rw, tn), lambda i, j: (layer, (aq + cw) // rw, j)),
                  pl.BlockSpec((tm, tn), lambda i, j: (i, j)),
                  pl.BlockSpec((None, MOD_ROWS, tn), lambda i, j: (layer, 0, 2 * (d // tn) + j))],
        out_specs=pl.BlockSpec((tm, tn), lambda i, j: (i, j)),
        out_shape=jax.ShapeDtypeStruct((n, d), F32),
        compiler_params=_params("parallel", "parallel"),
    )(attn, cmlp, rwkv, w_out, w_out, w_out, x2d, mod)


def _router_body(x_ref, g_ref, sh_ref, sc_ref, rw_ref, rb_ref, zn_ref, gate_ref, *,
                 tm, rows_per_mod, fixed_row, experts):
    row = _mod_row(pl.program_id(0), tm, rows_per_mod, fixed_row)
    zn = _modulated_norm(x_ref[...], g_ref[...], sh_ref[pl.ds(row, 1), :], sc_ref[pl.ds(row, 1), :])
    zn_ref[...] = zn.astype(BF16)
    logits = _dot(zn, rw_ref[...], precision=HIGHEST).T
    per_group = experts // N_EXPERT_GROUPS
    scores = [jax.nn.sigmoid(logits[e:e + 1, :]) for e in range(experts)]
    sel = [scores[e] + rb_ref[e:e + 1, :] for e in range(experts)]
    best_val, best_grp = None, None
    for gi in range(N_EXPERT_GROUPS):
        mem = sel[gi * per_group:(gi + 1) * per_group]
        top2 = None
        for a in range(per_group):
            for b2 in range(a + 1, per_group):
                pair = mem[a] + mem[b2]
                top2 = pair if top2 is None else jnp.maximum(top2, pair)
        if gi == 0:
            best_val, best_grp = top2, jnp.zeros(top2.shape, jnp.int32)
        else:
            better = top2 > best_val
            best_grp = jnp.where(better, gi, best_grp)
            best_val = jnp.where(better, top2, best_val)
    picked = []
    for e in range(experts):
        gi = e // per_group
        rank = jnp.zeros(best_grp.shape, jnp.int32)
        for j in range(gi * per_group, (gi + 1) * per_group):
            if j != e:
                ahead = (sel[j] > sel[e]) | ((sel[j] == sel[e]) & (j < e))
                rank = rank + ahead.astype(jnp.int32)
        picked.append(jnp.where((best_grp == gi) & (rank < TOP_K), scores[e], 0.0))
    total = picked[0]
    for e in range(1, experts):
        total = total + picked[e]
    gates = jnp.concatenate([pk / total for pk in picked] + [jnp.zeros((128 - experts, tm), F32)], axis=0)
    gate_ref[...] = gates.T


def _router(x2d, g, mod, router_w_pad, router_b_col, layer, rows_per_mod, fixed_row, experts):
    n, d = x2d.shape
    tm = min(256, n)
    body = functools.partial(_router_body, tm=tm, rows_per_mod=rows_per_mod, fixed_row=fixed_row, experts=experts)
    return pl.pallas_call(
        body,
        grid=(n // tm,),
        in_specs=[pl.BlockSpec((tm, d), lambda i: (i, 0)),
                  pl.BlockSpec((None, 1, d), lambda i: (layer, 0, 0)),
                  pl.BlockSpec((None, MOD_ROWS, d), lambda i: (layer, 0, 3)),
                  pl.BlockSpec((None, MOD_ROWS, d), lambda i: (layer, 0, 4)),
                  pl.BlockSpec((d, 128), lambda i: (0, 0)),
                  pl.BlockSpec((128, 1), lambda i: (0, 0))],
        out_specs=[pl.BlockSpec((tm, d), lambda i: (i, 0)),
                   pl.BlockSpec((tm, 128), lambda i: (i, 0))],
        out_shape=[jax.ShapeDtypeStruct((n, d), BF16), jax.ShapeDtypeStruct((n, 128), F32)],
        compiler_params=_params("parallel"),
    )(x2d, g, mod, mod, router_w_pad, router_b_col)


def _moe_body(zn_ref, gate_ref, w1_ref, w3_ref, w2_ref, g2_ref, o_ref, *,
              tm, tf, rows_per_mod, fixed_row, n_e, n_f):
    e, f = pl.program_id(1), pl.program_id(2)

    @pl.when((e == 0) & (f == 0))
    def _():
        o_ref[...] = jnp.zeros_like(o_ref)

    zn = zn_ref[...]
    h1 = _dot(zn, w1_ref[...])
    h3 = _dot(zn, w3_ref[...])
    onehot = (_iota((128, tf), 0) == e).astype(BF16)
    gates = gate_ref[...]
    g_hi = gates.astype(BF16)
    g_lo = (gates - g_hi.astype(F32)).astype(BF16)
    gcol = _dot(g_hi, onehot) + _dot(g_lo, onehot)
    hid = (h1 * jax.nn.sigmoid(h1)) * h3 * gcol
    o_ref[...] += _dot(hid.astype(BF16), w2_ref[...])

    @pl.when((e == n_e - 1) & (f == n_f - 1))
    def _():
        row = _mod_row(pl.program_id(0), tm, rows_per_mod, fixed_row)
        o_ref[...] = g2_ref[pl.ds(row, 1), :] * o_ref[...]


def _moe(zn, gates, w1, w3, w2, mod, layer, rows_per_mod, fixed_row):
    n, d = zn.shape
    n_e, ff = w1.shape[0], w1.shape[2]
    tm = min(512, n)
    tf = min(256, ff)
    n_f = ff // tf
    body = functools.partial(_moe_body, tm=tm, tf=tf, rows_per_mod=rows_per_mod, fixed_row=fixed_row,
                             n_e=n_e, n_f=n_f)
    return pl.pallas_call(
        body,
        grid=(n // tm, n_e, n_f),
        in_specs=[pl.BlockSpec((tm, d), lambda i, e, f: (i, 0)),
                  pl.BlockSpec((tm, 128), lambda i, e, f: (i, 0)),
                  pl.BlockSpec((None, d, tf), lambda i, e, f: (e, 0, f)),
                  pl.BlockSpec((None, d, tf), lambda i, e, f: (e, 0, f)),
                  pl.BlockSpec((None, tf, d), lambda i, e, f: (e, f, 0)),
                  pl.BlockSpec((None, MOD_ROWS, d), lambda i, e, f: (layer, 0, 5))],
        out_specs=pl.BlockSpec((tm, d), lambda i, e, f: (i, 0)),
        out_shape=jax.ShapeDtypeStruct((n, d), F32),
        compiler_params=_params("parallel", "arbitrary", "arbitrary"),
    )(zn, gates, w1, w3, w2, mod)


def _final_body(x_ref, g_ref, o_ref):
    x = x_ref[...]
    o_ref[...] = x * lax.rsqrt(jnp.mean(x * x, axis=-1, keepdims=True) + EPS) * g_ref[...]


def _final_norm(x2d, g):
    n, d = x2d.shape
    tm = min(256, n)
    return pl.pallas_call(
        _final_body,
        grid=(n // tm,),
        in_specs=[pl.BlockSpec((tm, d), lambda i: (i, 0)), pl.BlockSpec((1, d), lambda i: (0, 0))],
        out_specs=pl.BlockSpec((tm, d), lambda i: (i, 0)),
        out_shape=jax.ShapeDtypeStruct((n, d), F32),
        compiler_params=_params("parallel"),
    )(x2d, g)


def _rope_tables(t):
    pos = jnp.arange(t)
    half = HEAD_DIM // 4
    freqs = ROPE_BASE ** (-jnp.arange(half, dtype=F32) / half)
    ang_r = (pos // GRID_W).astype(F32)[:, None] * freqs[None, :]
    ang_c = (pos % GRID_W).astype(F32)[:, None] * freqs[None, :]
    cos_t = jnp.concatenate([jnp.cos(ang_r)] * 2 + [jnp.cos(ang_c)] * 2, axis=1)
    sin_t = jnp.concatenate([-jnp.sin(ang_r), jnp.sin(ang_r), -jnp.sin(ang_c), jnp.sin(ang_c)], axis=1)
    return cos_t, sin_t


def kernel(x, c, ctx, c_ctx, ada_w, ada_b, norm1_g, w_in, rwkv_conv, attn_sink, cmlp_norm_g, cmlp_ws, cmlp_b,
           rwkv_w0, rwkv_w1, rwkv_w2, rwkv_a0, rwkv_a1, rwkv_a2, rwkv_kk, rwkv_ka, rwkv_rk, rwkv_ln_w, rwkv_ln_b,
           w_out, norm2_g, router_w, router_b, moe_w1, moe_w3, moe_w2, final_g):
    b, t, d = x.shape
    l = ctx.shape[1]
    depth = ada_w.shape[0]
    cw = cmlp_norm_g.shape[1]
    rw = rwkv_w0.shape[2]
    lora = rwkv_w1.shape[3]
    experts = router_w.shape[1]
    dp = w_in.shape[2]
    akv = KV_HEADS * HEAD_DIM
    aq = dp - 2 * akv - 2 * cw - 4 * rw
    group = aq // akv
    heads = rw // RWKV_HEAD
    ucol = (aq + 2 * akv) // cw
    rcol = (aq + 2 * akv + 2 * cw) // rw
    gcol = rcol + 3
    hcol = dp // (2 * lora)
    assert (aq + 2 * akv) % cw == 0 and (aq + 2 * akv + 2 * cw) % rw == 0 and dp % (2 * lora) == 0
    assert b + 1 <= MOD_ROWS and (b * t) % l == 0 and t % 256 == 0 and l % 128 == 0

    cpad = jnp.zeros((MOD_ROWS, d), F32).at[:b].set(c).at[b].set(c_ctx)
    mod = _ada(cpad, ada_w, ada_b)
    cos_t, sin_t = _rope_tables(t)
    router_w_pad = jnp.zeros((d, 128), F32).at[:, :experts].set(router_w)
    router_b_col = jnp.zeros((128, 1), F32).at[:experts, 0].set(router_b)
    w_out_bf = w_out.astype(BF16)
    s_zero = jnp.zeros((2, b, heads, RWKV_HEAD, RWKV_HEAD), F32)

    xs = x.reshape(b * t, d)
    hs = ctx.reshape(b * l, d)
    for layer in range(depth):
        lat = dict(rows_per_mod=t, fixed_row=None)
        con = dict(rows_per_mod=None, fixed_row=b)
        last = layer == depth - 1
        w_cat = jnp.concatenate(
            [w_in[layer]] + [rwkv_w1[layer, z] for z in range(2)] + [rwkv_a1[layer, z] for z in range(2)],
            axis=1).astype(BF16)
        w1_bf, w3_bf, w2_bf = (w[layer].astype(BF16) for w in (moe_w1, moe_w3, moe_w2))
        px = _proj(xs, norm1_g.reshape(depth, 1, d), mod, w_cat, layer, **lat)
        pc = _proj(hs, norm1_g.reshape(depth, 1, d), mod, w_cat, layer, **con)

        sink = attn_sink[layer]
        sink_col = jnp.repeat(sink.reshape(KV_HEADS, group), ATTN_BLOCK, axis=1)[..., None]
        attn_x = _latent_attention(px, pc, cos_t, sin_t, sink_col, b, t, l, aq, akv)

        bs_b = jnp.broadcast_to(cmlp_b[layer][:, :, None], cmlp_b.shape[1:] + (CMLP_CH,))
        cmlp_x = _chunk_mlp(px, cmlp_norm_g[layer][None], cmlp_ws[layer], bs_b, ucol, cw)

        prep_args = (rwkv_conv[layer], rwkv_w2[layer], rwkv_a2[layer], rwkv_w0[layer], rwkv_a0[layer],
                     rwkv_kk[layer][None], rwkv_ka[layer][None])
        rc, kc, vc, kkc, lwc, bbc, krc = _rwkv_prep(pc, *prep_args, l, rcol, hcol, rw, lora)
        rx, kx, vx, kkx, lwx, bbx, krx = _rwkv_prep(px, *prep_args, t, rcol, hcol, rw, lora)
        y_c, s_ctx = _rwkv_scan(rc, kkc, vc, lwc, bbc, krc, s_zero, b, l, rw)
        y_x, _ = _rwkv_scan(rx, kkx, vx, lwx, bbx, krx, s_ctx, b, t, rw)
        out_args = (rwkv_rk[layer][None], rwkv_ln_w[layer][None], rwkv_ln_b[layer][None])
        rwkv_x = _rwkv_output(y_x, rx, kx, vx, px, gcol, *out_args)

        xs = _out_proj(attn_x, cmlp_x, rwkv_x, w_out_bf, xs, mod, layer, **lat)
        zn, gates = _router(xs, norm2_g.reshape(depth, 1, d), mod, router_w_pad, router_b_col, layer,
                            experts=experts, **lat)
        xs = xs + _moe(zn, gates, w1_bf, w3_bf, w2_bf, mod, layer, **lat)

        if not last:
            sink_rows = jnp.broadcast_to(sink[:, None, None], (KV_HEADS * group, l, 1))
            attn_c = _context_attention(pc, sink_rows, b, l, aq, akv)
            cmlp_c = _chunk_mlp(pc, cmlp_norm_g[layer][None], cmlp_ws[layer], bs_b, ucol, cw)
            rwkv_c = _rwkv_output(y_c, rc, kc, vc, pc, gcol, *out_args)
            hs = _out_proj(attn_c, cmlp_c, rwkv_c, w_out_bf, hs, mod, layer, **con)
            zn_c, gates_c = _router(hs, norm2_g.reshape(depth, 1, d), mod, router_w_pad, router_b_col, layer,
                                    experts=experts, **con)
            hs = hs + _moe(zn_c, gates_c, w1_bf, w3_bf, w2_bf, mod, layer, **con)
    return _final_norm(xs, final_g[None]).reshape(b, t, d)
```

```python
import functools

import jax
import jax.numpy as jnp
from jax import lax
from jax.experimental import pallas as pl
from jax.experimental.pallas import tpu as pltpu

F32, BF16 = jnp.float32, jnp.bfloat16
HIGHEST = lax.Precision.HIGHEST

HEAD_DIM = 128
KV_HEADS = 4
WINDOW = 128
ATTN_BLOCK = 128
GRID_W = 64
ROPE_BASE = 10000.0
CMLP_CH = 128
CMLP_CHUNK = 128
RWKV_HEAD = 64
RWKV_GN_EPS = 64e-5
N_EXPERT_GROUPS = 4
TOP_K = 2
N_MOD = 6
EPS = 1e-6
MASKED = -1e30

MOD_ROWS = 8
SCAN_CHUNK = 64
SCAN_HEADS = 16
SCAN_PASSES = {"pair": 1, "inv": 1, "solve": 1, "state": 1}
VMEM_LIMIT_BYTES = 56 * 1024 * 1024


def _params(*sem):
    return pltpu.CompilerParams(dimension_semantics=sem, vmem_limit_bytes=VMEM_LIMIT_BYTES)


def _dot(a, b, **kw):
    return jnp.dot(a, b, preferred_element_type=F32, **kw)


def _dot_nt(a, b, **kw):
    return lax.dot_general(a, b, (((1,), (1,)), ((), ())), preferred_element_type=F32, **kw)


def _dot_tn(a, b, **kw):
    return lax.dot_general(a, b, (((0,), (0,)), ((), ())), preferred_element_type=F32, **kw)


def _iota(shape, dim):
    return lax.broadcasted_iota(jnp.int32, shape, dim)


def _ada_body(c_ref, w_ref, b_ref, o_ref):
    c = c_ref[...]
    a = (c * jax.nn.sigmoid(c)).astype(BF16)
    o_ref[...] = _dot(a, w_ref[...].astype(BF16)) + b_ref[...]


def _ada(cpad, ada_w, ada_b):
    depth, d, n = ada_w.shape
    tn = 512
    return pl.pallas_call(
        _ada_body,
        grid=(depth, n // tn),
        in_specs=[pl.BlockSpec((MOD_ROWS, d), lambda l, j: (0, 0)),
                  pl.BlockSpec((None, d, tn), lambda l, j: (l, 0, j)),
                  pl.BlockSpec((None, 1, tn), lambda l, j: (l, 0, j))],
        out_specs=pl.BlockSpec((None, MOD_ROWS, tn), lambda l, j: (l, 0, j)),
        out_shape=jax.ShapeDtypeStruct((depth, MOD_ROWS, n), F32),
        compiler_params=_params("parallel", "parallel"),
    )(cpad, ada_w, ada_b.reshape(depth, 1, n))


def _mod_row(i, tm, rows_per_mod, fixed_row):
    return fixed_row if rows_per_mod is None else (i * tm) // rows_per_mod


def _modulated_norm(x, g, shift, scale):
    y = x * lax.rsqrt(jnp.mean(x * x, axis=-1, keepdims=True) + EPS) * g
    return y * (1.0 + scale) + shift


def _proj_body(x_ref, g_ref, sh_ref, sc_ref, w_ref, wl_ref, o_ref, xn_ref, *, tm, n_main, rows_per_mod, fixed_row):
    j = pl.program_id(1)

    @pl.when(j == 0)
    def _():
        r = _mod_row(pl.program_id(0), tm, rows_per_mod, fixed_row)
        xn = _modulated_norm(x_ref[...], g_ref[...], sh_ref[pl.ds(r, 1), :], sc_ref[pl.ds(r, 1), :])
        xn_ref[...] = xn.astype(BF16)

    @pl.when(j < n_main)
    def _():
        o_ref[...] = _dot(xn_ref[...], w_ref[...])

    @pl.when(j >= n_main)
    def _():
        o_ref[...] = _dot(xn_ref[...], wl_ref[...])


def _proj(x2d, g, mod, w_in, w_lora, layer, rows_per_mod, fixed_row):
    n, d = x2d.shape
    dp, tn = w_in.shape[2], w_lora.shape[2]
    tm = min(512, n)
    n_main = dp // tn
    body = functools.partial(_proj_body, tm=tm, n_main=n_main, rows_per_mod=rows_per_mod, fixed_row=fixed_row)
    return pl.pallas_call(
        body,
        grid=(n // tm, n_main + 1),
        in_specs=[pl.BlockSpec((tm, d), lambda i, j: (i, 0)),
                  pl.BlockSpec((None, 1, d), lambda i, j: (layer, 0, 0)),
                  pl.BlockSpec((None, MOD_ROWS, d), lambda i, j: (layer, 0, 0)),
                  pl.BlockSpec((None, MOD_ROWS, d), lambda i, j: (layer, 0, 1)),
                  pl.BlockSpec((None, d, tn), lambda i, j: (layer, 0, jnp.minimum(j, n_main - 1))),
                  pl.BlockSpec((None, d, tn), lambda i, j: (layer, 0, 0))],
        out_specs=pl.BlockSpec((tm, tn), lambda i, j: (i, j)),
        out_shape=jax.ShapeDtypeStruct((n, dp + tn), F32),
        scratch_shapes=[pltpu.VMEM((tm, d), BF16)],
        compiler_params=_params("parallel", "arbitrary"),
    )(x2d, g, mod, mod, w_in, w_lora)


def _rope(x, cos, sin_signed):
    lane = _iota(x.shape, 1)
    swapped = jnp.where((lane % 64) < 32, pltpu.roll(x, 96, axis=1), pltpu.roll(x, 32, axis=1))
    return x * cos + swapped * sin_signed


def _softmax_pv(parts, sink_col, vall):
    m = sink_col
    for s in parts:
        m = jnp.maximum(m, jnp.max(s, axis=-1, keepdims=True))
    ps = [jnp.exp(s - m) for s in parts]
    denom = jnp.exp(sink_col - m)
    for p in ps:
        denom = denom + jnp.sum(p, axis=-1, keepdims=True)
    p = ps[0] if len(ps) == 1 else jnp.concatenate(ps, axis=1)
    return _dot(p.astype(BF16), vall) / denom


def _attn_body(q_ref, kp_ref, kc_ref, kn_ref, vp_ref, vc_ref, vn_ref, kx_ref, vx_ref,
               cp_ref, cc_ref, cn_ref, sp_ref, sc_ref, sn_ref, sink_ref, o_ref, *, nb, group):
    n = pl.program_id(1)
    blk = ATTN_BLOCK
    cos = (cp_ref[...], cc_ref[...], cn_ref[...])
    sin = (sp_ref[...], sc_ref[...], sn_ref[...])
    qi = _iota((group * blk, 3 * blk), 0) % blk
    kj = _iota((group * blk, 3 * blk), 1)
    in_seq = ((kj >= blk) | (n > 0)) & ((kj < 2 * blk) | (n < nb - 1))
    band_ok = (jnp.abs(kj - blk - qi) <= WINDOW) & in_seq
    scale = HEAD_DIM ** -0.5
    for h in range(KV_HEADS):
        hs = slice(h * HEAD_DIM, (h + 1) * HEAD_DIM)
        kb = [_rope(r[:, hs], c, s) for r, c, s in zip((kp_ref, kc_ref, kn_ref), cos, sin)]
        kall = jnp.concatenate(kb + [kx_ref[:, hs]], axis=0).astype(BF16)
        vall = jnp.concatenate([vp_ref[:, hs], vc_ref[:, hs], vn_ref[:, hs], vx_ref[:, hs]], axis=0).astype(BF16)
        qs = []
        for g in range(group):
            c0 = (h * group + g) * HEAD_DIM
            qs.append(_rope(q_ref[:, c0:c0 + HEAD_DIM], cos[1], sin[1]))
        qh = jnp.concatenate(qs, axis=0).astype(BF16)
        s = _dot_nt(qh, kall) * scale
        s_loc = jnp.where(band_ok, s[:, :3 * blk], MASKED)
        o = _softmax_pv([s_loc, s[:, 3 * blk:]], sink_ref[h], vall)
        for g in range(group):
            c0 = (h * group + g) * HEAD_DIM
            o_ref[:, c0:c0 + HEAD_DIM] = o[g * blk:(g + 1) * blk].astype(o_ref.dtype)


def _latent_attention(px, pc, cos_t, sin_t, sink_col, b, t, l, aq, akv):
    blk = ATTN_BLOCK
    nb = t // blk
    group = aq // akv
    kcol, vcol = aq // akv, aq // akv + 1
    prev = lambda n: jnp.maximum(n - 1, 0)
    nxt = lambda n: jnp.minimum(n + 1, nb - 1)
    kv_spec = lambda col, f: pl.BlockSpec((blk, akv), lambda bi, n: (bi * nb + f(n), col))
    tab_spec = lambda f: pl.BlockSpec((blk, HEAD_DIM), lambda bi, n: (f(n), 0))
    ident = lambda n: n
    body = functools.partial(_attn_body, nb=nb, group=group)
    return pl.pallas_call(
        body,
        grid=(b, nb),
        in_specs=[pl.BlockSpec((blk, aq), lambda bi, n: (bi * nb + n, 0)),
                  kv_spec(kcol, prev), kv_spec(kcol, ident), kv_spec(kcol, nxt),
                  kv_spec(vcol, prev), kv_spec(vcol, ident), kv_spec(vcol, nxt),
                  pl.BlockSpec((l, akv), lambda bi, n: (bi, kcol)),
                  pl.BlockSpec((l, akv), lambda bi, n: (bi, vcol)),
                  tab_spec(prev), tab_spec(ident), tab_spec(nxt),
                  tab_spec(prev), tab_spec(ident), tab_spec(nxt),
                  pl.BlockSpec((KV_HEADS, group * blk, 1), lambda bi, n: (0, 0, 0))],
        out_specs=pl.BlockSpec((blk, aq), lambda bi, n: (bi * nb + n, 0)),
        out_shape=jax.ShapeDtypeStruct((b * t, aq), BF16),
        compiler_params=_params("parallel", "parallel"),
    )(px, px, px, px, px, px, px, pc, pc, cos_t, cos_t, cos_t, sin_t, sin_t, sin_t, sink_col)


def _ctx_attn_body(q_ref, k_ref, v_ref, sink_ref, o_ref, *, group):
    scale = HEAD_DIM ** -0.5
    for h in range(KV_HEADS):
        hs = slice(h * HEAD_DIM, (h + 1) * HEAD_DIM)
        kall = k_ref[:, hs].astype(BF16)
        vall = v_ref[:, hs].astype(BF16)
        for g in range(group):
            c0 = (h * group + g) * HEAD_DIM
            s = _dot_nt(q_ref[:, c0:c0 + HEAD_DIM].astype(BF16), kall) * scale
            o = _softmax_pv([s], sink_ref[h * group + g], vall)
            o_ref[:, c0:c0 + HEAD_DIM] = o.astype(o_ref.dtype)


def _context_attention(pc, sink_rows, b, l, aq, akv):
    group = aq // akv
    kcol, vcol = aq // akv, aq // akv + 1
    return pl.pallas_call(
        functools.partial(_ctx_attn_body, group=group),
        grid=(b,),
        in_specs=[pl.BlockSpec((l, aq), lambda bi: (bi, 0)),
                  pl.BlockSpec((l, akv), lambda bi: (bi, kcol)),
                  pl.BlockSpec((l, akv), lambda bi: (bi, vcol)),
                  pl.BlockSpec((KV_HEADS * group, l, 1), lambda bi: (0, 0, 0))],
        out_specs=pl.BlockSpec((l, aq), lambda bi: (bi, 0)),
        out_shape=jax.ShapeDtypeStruct((b * l, aq), BF16),
        compiler_params=_params("parallel"),
    )(pc, pc, pc, sink_rows)


def _cmlp_body(u_ref, gv_ref, g_ref, ws_ref, bs_ref, o_ref, *, groups):
    u = jax.nn.gelu(u_ref[...])
    gv = jax.nn.gelu(gv_ref[...])
    gvn = gv * lax.rsqrt(jnp.mean(gv * gv, axis=-1, keepdims=True) + EPS) * g_ref[...]
    for gi in range(groups):
        cs = slice(gi * CMLP_CH, (gi + 1) * CMLP_CH)
        mixed = _dot(ws_ref[gi].astype(BF16), gvn[:, cs].astype(BF16)) + bs_ref[gi]
        o_ref[:, cs] = (u[:, cs] * mixed).astype(o_ref.dtype)


def _chunk_mlp(p, norm_g, ws, bs_b, ucol, cw):
    n = p.shape[0]
    groups = cw // CMLP_CH
    ch = CMLP_CHUNK
    return pl.pallas_call(
        functools.partial(_cmlp_body, groups=groups),
        grid=(n // ch,),
        in_specs=[pl.BlockSpec((ch, cw), lambda i: (i, ucol)),
                  pl.BlockSpec((ch, cw), lambda i: (i, ucol + 1)),
                  pl.BlockSpec((1, cw), lambda i: (0, 0)),
                  pl.BlockSpec((groups, ch, ch), lambda i: (0, 0, 0)),
                  pl.BlockSpec((groups, ch, CMLP_CH), lambda i: (0, 0, 0))],
        out_specs=pl.BlockSpec((ch, cw), lambda i: (i, 0)),
        out_shape=jax.ShapeDtypeStruct((n, cw), BF16),
        compiler_params=_params("parallel"),
    )(p, p, norm_g, ws, bs_b)


def _head_sum(x):
    ones = (_iota((128, 128), 0) // RWKV_HEAD == _iota((128, 128), 1) // RWKV_HEAD).astype(F32)
    cols = [_dot(x[:, s * 128:(s + 1) * 128], ones, precision=HIGHEST) for s in range(x.shape[1] // 128)]
    return cols[0] if len(cols) == 1 else jnp.concatenate(cols, axis=1)


def _prep_body(r_ref, k_ref, v_ref, rp_ref, kp_ref, vp_ref, rn_ref, kn_ref, vn_ref,
               cr_ref, ck_ref, cv_ref, hw_ref, ha_ref, w2_ref, a2_ref, w0_ref, a0_ref, kkp_ref, kap_ref,
               ro_ref, ko_ref, vo_ref, kko_ref, lw_ref, bo_ref, kr_ref, *, tr, seq, lora):
    i = pl.program_id(0)
    first = (i * tr) % seq == 0
    last = ((i + 1) * tr) % seq == 0
    row = _iota(r_ref.shape, 0)

    def conv(x_ref, xp_ref, xn_ref, w_ref):
        x = x_ref[...]
        before = jnp.where(first, 0.0, xp_ref[7:8, :])
        after = jnp.where(last, 0.0, xn_ref[0:1, :])
        xm = jnp.where(row == 0, before, pltpu.roll(x, 1, axis=0))
        xp = jnp.where(row == tr - 1, after, pltpu.roll(x, tr - 1, axis=0))
        return xm * w_ref[0:1, :] + x * w_ref[1:2, :] + xp * w_ref[2:3, :]

    r = conv(r_ref, rp_ref, rn_ref, cr_ref)
    k = conv(k_ref, kp_ref, kn_ref, ck_ref)
    v = conv(v_ref, vp_ref, vn_ref, cv_ref)
    kk = k * kkp_ref[...]
    kk = kk * lax.rsqrt(_head_sum(kk * kk) + 1e-12)
    ro_ref[...] = r
    ko_ref[...] = k
    vo_ref[...] = v
    kko_ref[...] = kk
    for z in range(2):
        zs = slice(z * lora, (z + 1) * lora)
        w_raw = w0_ref[z:z + 1, :] + _dot(jnp.tanh(hw_ref[:, zs]), w2_ref[z], precision=HIGHEST)
        softplus_neg = jnp.maximum(-w_raw, 0.0) + jnp.log1p(jnp.exp(-jnp.abs(w_raw)))
        lw_ref[z] = -jnp.exp(-softplus_neg - 0.5)
        a = jax.nn.sigmoid(a0_ref[z:z + 1, :] + _dot(ha_ref[:, zs], a2_ref[z], precision=HIGHEST))
        kr_ref[z] = k * (1.0 + (a - 1.0) * kap_ref[...])
        bo_ref[z] = kk * a


def _rwkv_prep(p, conv_w, w2, a2, w0, a0, kk_p, ka_p, seq, rcol, hcol, rw, lora):
    n = p.shape[0]
    tr = min(256, seq)
    nh = n // 8
    body = functools.partial(_prep_body, tr=tr, seq=seq, lora=lora)
    cur = lambda c: pl.BlockSpec((tr, rw), lambda i: (i, rcol + c))
    prv = lambda c: pl.BlockSpec((8, rw), lambda i: (jnp.maximum(i * (tr // 8) - 1, 0), rcol + c))
    nxt = lambda c: pl.BlockSpec((8, rw), lambda i: (jnp.minimum((i + 1) * (tr // 8), nh - 1), rcol + c))
    cw = lambda c: pl.BlockSpec((3, rw), lambda i: (0, c))
    full2 = lambda shape: pl.BlockSpec(shape, lambda i: (0,) * len(shape))
    shared = pl.BlockSpec((tr, rw), lambda i: (i, 0))
    directed = pl.BlockSpec((2, tr, rw), lambda i: (0, i, 0))
    return pl.pallas_call(
        body,
        grid=(n // tr,),
        in_specs=[cur(0), cur(1), cur(2), prv(0), prv(1), prv(2), nxt(0), nxt(1), nxt(2),
                  cw(0), cw(1), cw(2),
                  pl.BlockSpec((tr, 2 * lora), lambda i: (i, hcol)),
                  pl.BlockSpec((tr, 2 * lora), lambda i: (i, hcol + 1)),
                  full2((2, lora, rw)), full2((2, lora, rw)), full2((2, rw)), full2((2, rw)),
                  full2((1, rw)), full2((1, rw))],
        out_specs=[shared, shared, shared, shared, directed, directed, directed],
        out_shape=[jax.ShapeDtypeStruct((n, rw), F32)] * 4 + [jax.ShapeDtypeStruct((2, n, rw), F32)] * 3,
        compiler_params=_params("parallel"),
    )(p, p, p, p, p, p, p, p, p, conv_w, conv_w, conv_w, p, p, w2, a2, w0, a0, kk_p, ka_p)


def _mm(a, b, passes, kind="nn"):
    fn = {"nn": _dot, "nt": _dot_nt, "tn": _dot_tn}[kind]
    if passes == 6:
        return fn(a, b, precision=HIGHEST)
    a_hi, b_hi = a.astype(BF16), b.astype(BF16)
    if passes == 1:
        return fn(a_hi, b_hi)
    a_lo = (a - a_hi.astype(F32)).astype(BF16)
    b_lo = (b - b_hi.astype(F32)).astype(BF16)
    return fn(a_hi, b_hi) + (fn(a_lo, b_hi) + fn(a_hi, b_lo))


def _unit_tri_inverse(nmats, eye, same16, same32, passes):
    n16 = [jnp.where(same16, n, 0.0) for n in nmats]
    xs = [eye - n for n in n16]
    pw = n16
    for _ in range(3):
        pw = [_mm(p, p, passes) for p in pw]
        xs = [x + _mm(x, p, passes) for x, p in zip(xs, pw)]
    for mask in (same32 & ~same16, ~same32):
        offs = [jnp.where(mask, n, 0.0) for n in nmats]
        xo = [_mm(x, o, passes) for x, o in zip(xs, offs)]
        xs = [x - _mm(t, x, passes) for x, t in zip(xs, xo)]
    return xs


def _scan_body(r_ref, kk_ref, v_ref, lw_ref, b_ref, k_ref, s0_ref, y_ref, sf_ref, st_ref, *, heads, nchunks):
    z = pl.program_id(0)
    c = pl.program_id(3)
    C, K = SCAN_CHUNK, RWKV_HEAD
    pp = SCAN_PASSES

    @pl.when(c == 0)
    def _():
        st_ref[...] = s0_ref[...]

    ti, si = _iota((C, C), 0), _iota((C, C), 1)
    before = (si - ti) * (1 - 2 * z) < 0
    upto = before | (si == ti)
    eye = (si == ti).astype(F32)
    same16 = (ti // 16) == (si // 16)
    same32 = (ti // 32) == (si // 32)

    lw = lw_ref[...]
    lc = _dot(upto.astype(F32), lw, precision=HIGHEST)
    ltot = jnp.sum(lw, axis=0, keepdims=True)
    e_neg = jnp.exp(-lc)
    e_out = jnp.exp(ltot - lc)
    kkt = kk_ref[...] * jnp.exp(lc - lw)
    rt = r_ref[...] * jnp.exp(lc)
    bt = b_ref[...] * e_neg
    kt = k_ref[...] * e_neg
    bh = b_ref[...] * e_out
    kh = k_ref[...] * e_out
    etot = jnp.exp(ltot)
    v = v_ref[...]

    hr = range(heads)
    ls = [slice(h * K, (h + 1) * K) for h in hr]
    ps = [_mm(jnp.concatenate([kkt[:, s], rt[:, s]], axis=0),
              jnp.concatenate([bt[:, s], kt[:, s]], axis=0), pp["pair"], "nt") for s in ls]
    nmats = [jnp.where(before, p[:C, :C], 0.0) for p in ps]
    pkk = [jnp.where(before, p[:C, C:], 0.0) for p in ps]
    prb = [jnp.where(upto, p[C:, :C], 0.0) for p in ps]
    prk = [jnp.where(upto, p[C:, C:], 0.0) for p in ps]
    tinv = _unit_tri_inverse(nmats, eye, same16, same32, pp["inv"])
    tg = [_mm(tinv[h], jnp.concatenate([kkt[:, ls[h]], pkk[h]], axis=1), pp["solve"]) for h in hr]
    qa = [jnp.concatenate([rt[:, ls[h]], prk[h]], axis=1) - _mm(prb[h], tg[h], pp["solve"]) for h in hr]
    m3 = [_mm(bh[:, ls[h]], tg[h], pp["solve"], "tn") for h in hr]
    sv = [jnp.concatenate([st_ref[h], v[:, ls[h]]], axis=0) for h in hr]
    ys = [_mm(qa[h], sv[h], pp["state"]) for h in hr]
    for h in hr:
        decay_diag = eye * jnp.broadcast_to(etot[:, ls[h]], (K, K))
        trans = jnp.concatenate([decay_diag, jnp.zeros((K, C), F32)], axis=1) - m3[h]
        st_ref[h] = _mm(trans, sv[h], pp["state"]) + _mm(kh[:, ls[h]], v[:, ls[h]], pp["state"], "tn")
    y_ref[...] = ys[0] if heads == 1 else jnp.concatenate(ys, axis=1)

    @pl.when(c == nchunks - 1)
    def _():
        sf_ref[...] = st_ref[...]


def _rwkv_scan(r, kk, v, lw, bb, kr, s0, b, seq, rw):
    C, K = SCAN_CHUNK, RWKV_HEAD
    nchunks = seq // C
    heads = min(SCAN_HEADS, rw // K)
    ngroups = rw // (heads * K)
    n = b * seq
    row = lambda z, bi, hg, c: bi * nchunks + c + z * (nchunks - 1 - 2 * c)
    shared = pl.BlockSpec((C, heads * K), lambda z, bi, hg, c: (row(z, bi, hg, c), hg))
    directed = pl.BlockSpec((None, C, heads * K), lambda z, bi, hg, c: (z, row(z, bi, hg, c), hg))
    state = pl.BlockSpec((None, None, heads, K, K), lambda z, bi, hg, c: (z, bi, hg, 0, 0))
    body = functools.partial(_scan_body, heads=heads, nchunks=nchunks)
    return pl.pallas_call(
        body,
        grid=(2, b, ngroups, nchunks),
        in_specs=[shared, shared, shared, directed, directed, directed, state],
        out_specs=[directed, state],
        out_shape=[jax.ShapeDtypeStruct((2, n, rw), F32),
                   jax.ShapeDtypeStruct((2, b, rw // K, K, K), F32)],
        scratch_shapes=[pltpu.VMEM((heads, K, K), F32)],
        compiler_params=_params("parallel", "parallel", "parallel", "arbitrary"),
    )(r, kk, v, lw, bb, kr, s0)


def _rwkv_out_body(yf_ref, yb_ref, r_ref, k_ref, v_ref, g_ref, rk_ref, lnw_ref, lnb_ref, o_ref):
    y = yf_ref[...] + yb_ref[...]
    inv = 1.0 / RWKV_HEAD
    mu = _head_sum(y) * inv
    d = y - mu
    var = _head_sum(d * d) * inv
    yn = d * lax.rsqrt(var + RWKV_GN_EPS) * lnw_ref[...] + lnb_ref[...]
    bonus = _head_sum(r_ref[...] * k_ref[...] * rk_ref[...]) * v_ref[...]
    o_ref[...] = ((yn + bonus) * jax.nn.sigmoid(g_ref[...])).astype(o_ref.dtype)


def _rwkv_output(y, r, k, v, p, gcol, rk, ln_w, ln_b):
    n, rw = r.shape
    tr = min(256, n)
    shared = pl.BlockSpec((tr, rw), lambda i: (i, 0))
    vec = pl.BlockSpec((1, rw), lambda i: (0, 0))
    return pl.pallas_call(
        _rwkv_out_body,
        grid=(n // tr,),
        in_specs=[pl.BlockSpec((None, tr, rw), lambda i: (0, i, 0)),
                  pl.BlockSpec((None, tr, rw), lambda i: (1, i, 0)),
                  shared, shared, shared,
                  pl.BlockSpec((tr, rw), lambda i: (i, gcol)),
                  vec, vec, vec],
        out_specs=shared,
        out_shape=jax.ShapeDtypeStruct((n, rw), BF16),
        compiler_params=_params("parallel"),
    )(y, y, r, k, v, p, rk, ln_w, ln_b)


def _wout_body(a_ref, c_ref, r_ref, wa_ref, wc_ref, wr_ref, x_ref, g_ref, o_ref, *, tm, rows_per_mod, fixed_row):
    row = _mod_row(pl.program_id(0), tm, rows_per_mod, fixed_row)
    acc = _dot(a_ref[...], wa_ref[...]) + _dot(c_ref[...], wc_ref[...]) + _dot(r_ref[...], wr_ref[...])
    o_ref[...] = x_ref[...] + g_ref[pl.ds(row, 1), :] * acc


def _out_proj(attn, cmlp, rwkv, w_out, x2d, mod, layer, rows_per_mod, fixed_row):
    n, d = x2d.shape
    aq, cw, rw = attn.shape[1], cmlp.shape[1], rwkv.shape[1]
    tm = min(512, n)
    tn = min(1024, d)
    body = functools.partial(_wout_body, tm=tm, rows_per_mod=rows_per_mod, fixed_row=fixed_row)
    return pl.pallas_call(
        body,
        grid=(n // tm, d // tn),
        in_specs=[pl.BlockSpec((tm, aq), lambda i, j: (i, 0)),
                  pl.BlockSpec((tm, cw), lambda i, j: (i, 0)),
                  pl.BlockSpec((tm, rw), lambda i, j: (i, 0)),
                  pl.BlockSpec((None, aq, tn), lambda i, j: (layer, 0, j)),
                  pl.BlockSpec((None, cw, tn), lambda i, j: (layer, aq // cw, j)),
                  pl.BlockSpec((None, rw, tn), lambda i, j: (layer, (aq + cw) // rw, j)),
                  pl.BlockSpec((tm, tn), lambda i, j: (i, j)),
                  pl.BlockSpec((None, MOD_ROWS, tn), lambda i, j: (layer, 0, 2 * (d // tn) + j))],
        out_specs=pl.BlockSpec((tm, tn), lambda i, j: (i, j)),
        out_shape=jax.ShapeDtypeStruct((n, d), F32),
        compiler_params=_params("parallel", "parallel"),
    )(attn, cmlp, rwkv, w_out, w_out, w_out, x2d, mod)


def _router_body(x_ref, g_ref, sh_ref, sc_ref, rw_ref, rb_ref, zn_ref, ids_ref, wt_ref, *,
                 tm, rows_per_mod, fixed_row, experts):
    row = _mod_row(pl.program_id(0), tm, rows_per_mod, fixed_row)
    zn = _modulated_norm(x_ref[...], g_ref[...], sh_ref[pl.ds(row, 1), :], sc_ref[pl.ds(row, 1), :])
    zn_ref[...] = zn
    logits = _dot(zn, rw_ref[...], precision=HIGHEST).T
    per_group = experts // N_EXPERT_GROUPS
    scores = [jax.nn.sigmoid(logits[e:e + 1, :]) for e in range(experts)]
    sel = [scores[e] + rb_ref[e:e + 1, :] for e in range(experts)]
    best_val, best_grp = None, None
    for gi in range(N_EXPERT_GROUPS):
        mem = sel[gi * per_group:(gi + 1) * per_group]
        top2 = None
        for a in range(per_group):
            for b2 in range(a + 1, per_group):
                pair = mem[a] + mem[b2]
                top2 = pair if top2 is None else jnp.maximum(top2, pair)
        if gi == 0:
            best_val, best_grp = top2, jnp.zeros(top2.shape, jnp.int32)
        else:
            better = top2 > best_val
            best_grp = jnp.where(better, gi, best_grp)
            best_val = jnp.where(better, top2, best_val)
    chosen, picked = [], []
    for e in range(experts):
        gi = e // per_group
        rank = jnp.zeros(best_grp.shape, jnp.int32)
        for j in range(gi * per_group, (gi + 1) * per_group):
            if j != e:
                ahead = (sel[j] > sel[e]) | ((sel[j] == sel[e]) & (j < e))
                rank = rank + ahead.astype(jnp.int32)
        chosen.append((best_grp == gi) & (rank < TOP_K))
        picked.append(jnp.where(chosen[e], scores[e], 0.0))
    total = picked[0]
    for e in range(1, experts):
        total = total + picked[e]
    zero_i, zero_f = jnp.zeros(total.shape, jnp.int32), jnp.zeros(total.shape, F32)
    seen, ids, wts = zero_i, [zero_i, zero_i], [zero_f, zero_f]
    for e in range(experts):
        gate = picked[e] / total
        for slot in range(TOP_K):
            here = chosen[e] & (seen == slot)
            ids[slot] = jnp.where(here, e, ids[slot])
            wts[slot] = jnp.where(here, gate, wts[slot])
        seen = seen + chosen[e].astype(jnp.int32)
    ids_ref[...] = jnp.concatenate(ids + [jnp.zeros((8 - TOP_K, tm), jnp.int32)], axis=0)
    wt_ref[...] = jnp.concatenate(wts + [jnp.zeros((128 - TOP_K, tm), F32)], axis=0).T


def _router(x2d, g, mod, router_w_pad, router_b_col, layer, rows_per_mod, fixed_row, experts):
    n, d = x2d.shape
    tm = min(256, n)
    body = functools.partial(_router_body, tm=tm, rows_per_mod=rows_per_mod, fixed_row=fixed_row, experts=experts)
    return pl.pallas_call(
        body,
        grid=(n // tm,),
        in_specs=[pl.BlockSpec((tm, d), lambda i: (i, 0)),
                  pl.BlockSpec((None, 1, d), lambda i: (layer, 0, 0)),
                  pl.BlockSpec((None, MOD_ROWS, d), lambda i: (layer, 0, 3)),
                  pl.BlockSpec((None, MOD_ROWS, d), lambda i: (layer, 0, 4)),
                  pl.BlockSpec((d, 128), lambda i: (0, 0)),
                  pl.BlockSpec((128, 1), lambda i: (0, 0))],
        out_specs=[pl.BlockSpec((tm, d), lambda i: (i, 0)),
                   pl.BlockSpec((8, tm), lambda i: (0, i)),
                   pl.BlockSpec((tm, 128), lambda i: (i, 0))],
        out_shape=[jax.ShapeDtypeStruct((n, d), F32), jax.ShapeDtypeStruct((8, n), jnp.int32),
                   jax.ShapeDtypeStruct((n, 128), F32)],
        compiler_params=_params("parallel"),
    )(x2d, g, mod, mod, router_w_pad, router_b_col)


def _route_plan(ids, experts, tm):
    n = ids.shape[1]
    total = TOP_K * n + experts * tm
    flat = ids.reshape(-1)
    onehot = (flat[:, None] == jnp.arange(experts, dtype=jnp.int32)[None, :]).astype(jnp.int32)
    rank = jnp.cumsum(onehot, axis=0) - onehot
    padded = ((jnp.sum(onehot, axis=0) + tm - 1) // tm) * tm
    ends = jnp.cumsum(padded)
    pos = (ends - padded)[flat] + jnp.sum(rank * onehot, axis=1)
    row_token = jnp.zeros((total,), jnp.int32).at[pos].set(jnp.tile(jnp.arange(n, dtype=jnp.int32), TOP_K))
    tile_start = jnp.arange(total // tm, dtype=jnp.int32) * tm
    tile_used = (tile_start < ends[-1]).astype(jnp.int32)
    tile_expert = jnp.minimum(jnp.searchsorted(ends, tile_start, side="right"), experts - 1).astype(jnp.int32)
    last_used = tile_expert[jnp.maximum(ends[-1] // tm - 1, 0)]
    tile_expert = jnp.where(tile_used == 1, tile_expert, last_used)
    return pos.astype(jnp.int32), row_token, tile_expert, tile_used


def _row_copy(src_hbm, row, dst_ref, i, sem):
    return pltpu.make_async_copy(src_hbm.at[pl.ds(row, 1), :], dst_ref.at[pl.ds(i, 1), :], sem)


def _gather_rows(src_hbm, dst_ref, sem, index_of, count):
    def start(i, carry):
        _row_copy(src_hbm, index_of(i), dst_ref, i, sem).start()
        return carry

    def wait(i, carry):
        _row_copy(src_hbm, 0, dst_ref, i, sem).wait()
        return carry

    lax.fori_loop(0, count, start, 0, unroll=8)
    lax.fori_loop(0, count, wait, 0, unroll=8)


def _expert_hidden_body(te_ref, used_ref, tok_ref, zn_hbm, w1_ref, w3_ref, h_ref, xg_ref, sem, *, tm):
    p, f = pl.program_id(0), pl.program_id(1)
    used = used_ref[p] == 1

    @pl.when(used & (f == 0))
    def _():
        _gather_rows(zn_hbm, xg_ref, sem, lambda i: tok_ref[p * tm + i], tm)

    @pl.when(used)
    def _():
        x = xg_ref[...].astype(BF16)
        h1 = _dot(x, w1_ref[...].astype(BF16))
        h3 = _dot(x, w3_ref[...].astype(BF16))
        h_ref[...] = ((h1 * jax.nn.sigmoid(h1)) * h3).astype(h_ref.dtype)

    @pl.when(jnp.logical_not(used))
    def _():
        h_ref[...] = jnp.zeros_like(h_ref)


def _expert_out_body(te_ref, used_ref, h_ref, w2_ref, o_ref):
    o_ref[...] = _dot(h_ref[...], w2_ref[...].astype(BF16))


def _experts(zn, row_token, tile_expert, tile_used, w1, w3, w2, layer, tm):
    n, d = zn.shape
    ff = w1.shape[3]
    total = row_token.shape[0]
    ntiles = total // tm
    tf = min(256, ff)
    n_f = ff // tf
    tn = min(1024, d)
    n_j = d // tn
    hold = lambda used, p, j, last: jnp.where(used[p] == 1, j, last)
    hidden = pl.pallas_call(
        functools.partial(_expert_hidden_body, tm=tm),
        grid_spec=pltpu.PrefetchScalarGridSpec(
            num_scalar_prefetch=3,
            grid=(ntiles, n_f),
            in_specs=[pl.BlockSpec(memory_space=pl.ANY),
                      pl.BlockSpec((None, None, d, tf),
                                   lambda p, f, te, us, tk: (layer, te[p], 0, hold(us, p, f, n_f - 1))),
                      pl.BlockSpec((None, None, d, tf),
                                   lambda p, f, te, us, tk: (layer, te[p], 0, hold(us, p, f, n_f - 1)))],
            out_specs=pl.BlockSpec((tm, tf), lambda p, f, te, us, tk: (p, f)),
            scratch_shapes=[pltpu.VMEM((tm, d), F32), pltpu.SemaphoreType.DMA(())]),
        out_shape=jax.ShapeDtypeStruct((total, ff), BF16),
        compiler_params=_params("arbitrary", "arbitrary"),
    )(tile_expert, tile_used, row_token, zn, w1, w3)
    return pl.pallas_call(
        _expert_out_body,
        grid_spec=pltpu.PrefetchScalarGridSpec(
            num_scalar_prefetch=2,
            grid=(ntiles, n_j),
            in_specs=[pl.BlockSpec((tm, ff), lambda p, j, te, us: (p, 0)),
                      pl.BlockSpec((None, None, ff, tn),
                                   lambda p, j, te, us: (layer, te[p], 0, hold(us, p, j, n_j - 1)))],
            out_specs=pl.BlockSpec((tm, tn), lambda p, j, te, us: (p, j))),
        out_shape=jax.ShapeDtypeStruct((total, d), F32),
        compiler_params=_params("parallel", "parallel"),
    )(tile_expert, tile_used, hidden, w2)


def _combine_body(pos_ref, ys_hbm, wt_ref, x_ref, g2_ref, o_ref, ya_ref, yb_ref, sem_a, sem_b, *,
                  tm, n, rows_per_mod, fixed_row):
    i = pl.program_id(0)
    _gather_rows(ys_hbm, ya_ref, sem_a, lambda r: pos_ref[i * tm + r], tm)
    _gather_rows(ys_hbm, yb_ref, sem_b, lambda r: pos_ref[n + i * tm + r], tm)
    row = _mod_row(i, tm, rows_per_mod, fixed_row)
    wt = wt_ref[...]
    mix = wt[:, 0:1] * ya_ref[...] + wt[:, 1:2] * yb_ref[...]
    o_ref[...] = x_ref[...] + g2_ref[pl.ds(row, 1), :] * mix


def _moe_combine(ys, pos, wts, x2d, mod, layer, rows_per_mod, fixed_row):
    n, d = x2d.shape
    tm = min(256, n)
    body = functools.partial(_combine_body, tm=tm, n=n, rows_per_mod=rows_per_mod, fixed_row=fixed_row)
    return pl.pallas_call(
        body,
        grid_spec=pltpu.PrefetchScalarGridSpec(
            num_scalar_prefetch=1,
            grid=(n // tm,),
            in_specs=[pl.BlockSpec(memory_space=pl.ANY),
                      pl.BlockSpec((tm, 128), lambda i, ps: (i, 0)),
                      pl.BlockSpec((tm, d), lambda i, ps: (i, 0)),
                      pl.BlockSpec((None, MOD_ROWS, d), lambda i, ps: (layer, 0, 5))],
            out_specs=pl.BlockSpec((tm, d), lambda i, ps: (i, 0)),
            scratch_shapes=[pltpu.VMEM((tm, d), F32), pltpu.VMEM((tm, d), F32),
                            pltpu.SemaphoreType.DMA(()), pltpu.SemaphoreType.DMA(())]),
        out_shape=jax.ShapeDtypeStruct((n, d), F32),
        compiler_params=_params("arbitrary"),
    )(pos, ys, wts, x2d, mod)


def _moe(x2d, g, mod, router_w_pad, router_b_col, w1, w3, w2, layer, experts, rows_per_mod, fixed_row):
    n = x2d.shape[0]
    tm = min(512, n)
    zn, ids, wts = _router(x2d, g, mod, router_w_pad, router_b_col, layer, rows_per_mod, fixed_row, experts)
    pos, row_token, tile_expert, tile_used = _route_plan(ids[:TOP_K], experts, tm)
    ys = _experts(zn, row_token, tile_expert, tile_used, w1, w3, w2, layer, tm)
    return _moe_combine(ys, pos, wts, x2d, mod, layer, rows_per_mod, fixed_row)


def _final_body(x_ref, g_ref, o_ref):
    x = x_ref[...]
    o_ref[...] = x * lax.rsqrt(jnp.mean(x * x, axis=-1, keepdims=True) + EPS) * g_ref[...]


def _final_norm(x2d, g):
    n, d = x2d.shape
    tm = min(256, n)
    return pl.pallas_call(
        _final_body,
        grid=(n // tm,),
        in_specs=[pl.BlockSpec((tm, d), lambda i: (i, 0)), pl.BlockSpec((1, d), lambda i: (0, 0))],
        out_specs=pl.BlockSpec((tm, d), lambda i: (i, 0)),
        out_shape=jax.ShapeDtypeStruct((n, d), F32),
        compiler_params=_params("parallel"),
    )(x2d, g)


def _rope_tables(t):
    pos = jnp.arange(t)
    half = HEAD_DIM // 4
    freqs = ROPE_BASE ** (-jnp.arange(half, dtype=F32) / half)
    ang_r = (pos // GRID_W).astype(F32)[:, None] * freqs[None, :]
    ang_c = (pos % GRID_W).astype(F32)[:, None] * freqs[None, :]
    cos_t = jnp.concatenate([jnp.cos(ang_r)] * 2 + [jnp.cos(ang_c)] * 2, axis=1)
    sin_t = jnp.concatenate([-jnp.sin(ang_r), jnp.sin(ang_r), -jnp.sin(ang_c), jnp.sin(ang_c)], axis=1)
    return cos_t, sin_t


def kernel(x, c, ctx, c_ctx, ada_w, ada_b, norm1_g, w_in, rwkv_conv, attn_sink, cmlp_norm_g, cmlp_ws, cmlp_b,
           rwkv_w0, rwkv_w1, rwkv_w2, rwkv_a0, rwkv_a1, rwkv_a2, rwkv_kk, rwkv_ka, rwkv_rk, rwkv_ln_w, rwkv_ln_b,
           w_out, norm2_g, router_w, router_b, moe_w1, moe_w3, moe_w2, final_g):
    b, t, d = x.shape
    l = ctx.shape[1]
    depth = ada_w.shape[0]
    cw = cmlp_norm_g.shape[1]
    rw = rwkv_w0.shape[2]
    lora = rwkv_w1.shape[3]
    experts = router_w.shape[1]
    dp = w_in.shape[2]
    akv = KV_HEADS * HEAD_DIM
    aq = dp - 2 * akv - 2 * cw - 4 * rw
    group = aq // akv
    heads = rw // RWKV_HEAD
    ucol = (aq + 2 * akv) // cw
    rcol = (aq + 2 * akv + 2 * cw) // rw
    gcol = rcol + 3
    hcol = dp // (2 * lora)
    assert (aq + 2 * akv) % cw == 0 and (aq + 2 * akv + 2 * cw) % rw == 0 and dp % (2 * lora) == 0
    assert b + 1 <= MOD_ROWS and (b * t) % l == 0 and t % 256 == 0 and l % 128 == 0

    cpad = jnp.zeros((MOD_ROWS, d), F32).at[:b].set(c).at[b].set(c_ctx)
    mod = _ada(cpad, ada_w, ada_b)
    cos_t, sin_t = _rope_tables(t)
    router_w_pad = jnp.zeros((d, 128), F32).at[:, :experts].set(router_w)
    router_b_col = jnp.zeros((128, 1), F32).at[:experts, 0].set(router_b)
    w_out_bf = w_out.astype(BF16)
    w_in_bf = w_in.astype(BF16)
    w_lora_bf = jnp.concatenate([rwkv_w1[:, 0], rwkv_w1[:, 1], rwkv_a1[:, 0], rwkv_a1[:, 1]], axis=2).astype(BF16)
    assert dp % w_lora_bf.shape[2] == 0
    s_zero = jnp.zeros((2, b, heads, RWKV_HEAD, RWKV_HEAD), F32)

    xs = x.reshape(b * t, d)
    hs = ctx.reshape(b * l, d)
    for layer in range(depth):
        lat = dict(rows_per_mod=t, fixed_row=None)
        con = dict(rows_per_mod=None, fixed_row=b)
        last = layer == depth - 1
        px = _proj(xs, norm1_g.reshape(depth, 1, d), mod, w_in_bf, w_lora_bf, layer, **lat)
        pc = _proj(hs, norm1_g.reshape(depth, 1, d), mod, w_in_bf, w_lora_bf, layer, **con)

        sink = attn_sink[layer]
        sink_col = jnp.repeat(sink.reshape(KV_HEADS, group), ATTN_BLOCK, axis=1)[..., None]
        attn_x = _latent_attention(px, pc, cos_t, sin_t, sink_col, b, t, l, aq, akv)

        bs_b = jnp.broadcast_to(cmlp_b[layer][:, :, None], cmlp_b.shape[1:] + (CMLP_CH,))
        cmlp_x = _chunk_mlp(px, cmlp_norm_g[layer][None], cmlp_ws[layer], bs_b, ucol, cw)

        prep_args = (rwkv_conv[layer], rwkv_w2[layer], rwkv_a2[layer], rwkv_w0[layer], rwkv_a0[layer],
                     rwkv_kk[layer][None], rwkv_ka[layer][None])
        rc, kc, vc, kkc, lwc, bbc, krc = _rwkv_prep(pc, *prep_args, l, rcol, hcol, rw, lora)
        rx, kx, vx, kkx, lwx, bbx, krx = _rwkv_prep(px, *prep_args, t, rcol, hcol, rw, lora)
        y_c, s_ctx = _rwkv_scan(rc, kkc, vc, lwc, bbc, krc, s_zero, b, l, rw)
        y_x, _ = _rwkv_scan(rx, kkx, vx, lwx, bbx, krx, s_ctx, b, t, rw)
        out_args = (rwkv_rk[layer][None], rwkv_ln_w[layer][None], rwkv_ln_b[layer][None])
        rwkv_x = _rwkv_output(y_x, rx, kx, vx, px, gcol, *out_args)

        xs = _out_proj(attn_x, cmlp_x, rwkv_x, w_out_bf, xs, mod, layer, **lat)
        moe_args = (norm2_g.reshape(depth, 1, d), mod, router_w_pad, router_b_col, moe_w1, moe_w3, moe_w2,
                    layer, experts)
        xs = _moe(xs, *moe_args, **lat)

        if not last:
            sink_rows = jnp.broadcast_to(sink[:, None, None], (KV_HEADS * group, l, 1))
            attn_c = _context_attention(pc, sink_rows, b, l, aq, akv)
            cmlp_c = _chunk_mlp(pc, cmlp_norm_g[layer][None], cmlp_ws[layer], bs_b, ucol, cw)
            rwkv_c = _rwkv_output(y_c, rc, kc, vc, pc, gcol, *out_args)
            hs = _out_proj(attn_c, cmlp_c, rwkv_c, w_out_bf, hs, mod, layer, **con)
            hs = _moe(hs, *moe_args, **con)
    return _final_norm(xs, final_g[None]).reshape(b, t, d)
```

```python
import functools

import jax
import jax.numpy as jnp
from jax import lax
from jax.experimental import pallas as pl
from jax.experimental.pallas import tpu as pltpu

F32, BF16 = jnp.float32, jnp.bfloat16
HIGHEST = lax.Precision.HIGHEST

HEAD_DIM = 128
KV_HEADS = 4
WINDOW = 128
ATTN_BLOCK = 128
GRID_W = 64
ROPE_BASE = 10000.0
CMLP_CH = 128
CMLP_CHUNK = 128
RWKV_HEAD = 64
RWKV_GN_EPS = 64e-5
N_EXPERT_GROUPS = 4
TOP_K = 2
N_MOD = 6
EPS = 1e-6
MASKED = -1e30

MOD_ROWS = 8
SCAN_CHUNK = 64
SCAN_HEADS = 16
SCAN_PASSES = {"pair": 1, "inv": 1, "solve": 1, "state": 1}
VMEM_LIMIT_BYTES = 56 * 1024 * 1024


def _params(*sem, **kw):
    return pltpu.CompilerParams(dimension_semantics=sem, vmem_limit_bytes=VMEM_LIMIT_BYTES, **kw)


def _dot(a, b, **kw):
    return jnp.dot(a, b, preferred_element_type=F32, **kw)


def _dot_nt(a, b, **kw):
    return lax.dot_general(a, b, (((1,), (1,)), ((), ())), preferred_element_type=F32, **kw)


def _dot_tn(a, b, **kw):
    return lax.dot_general(a, b, (((0,), (0,)), ((), ())), preferred_element_type=F32, **kw)


def _iota(shape, dim):
    return lax.broadcasted_iota(jnp.int32, shape, dim)


def _ada_body(c_ref, w_ref, b_ref, o_ref):
    c = c_ref[...]
    a = (c * jax.nn.sigmoid(c)).astype(BF16)
    o_ref[...] = _dot(a, w_ref[...].astype(BF16)) + b_ref[...]


def _ada(cpad, ada_w, ada_b):
    depth, d, n = ada_w.shape
    tn = 512
    return pl.pallas_call(
        _ada_body,
        grid=(depth, n // tn),
        in_specs=[pl.BlockSpec((MOD_ROWS, d), lambda l, j: (0, 0)),
                  pl.BlockSpec((None, d, tn), lambda l, j: (l, 0, j)),
                  pl.BlockSpec((None, 1, tn), lambda l, j: (l, 0, j))],
        out_specs=pl.BlockSpec((None, MOD_ROWS, tn), lambda l, j: (l, 0, j)),
        out_shape=jax.ShapeDtypeStruct((depth, MOD_ROWS, n), F32),
        compiler_params=_params("parallel", "parallel"),
    )(cpad, ada_w, ada_b.reshape(depth, 1, n))


def _mod_row(i, tm, rows_per_mod, fixed_row):
    return fixed_row if rows_per_mod is None else (i * tm) // rows_per_mod


def _modulated_norm(x, g, shift, scale):
    y = x * lax.rsqrt(jnp.mean(x * x, axis=-1, keepdims=True) + EPS) * g
    return y * (1.0 + scale) + shift


def _proj_body(x_ref, g_ref, sh_ref, sc_ref, w_ref, wl_ref, o_ref, xn_ref, *, tm, n_main, rows_per_mod, fixed_row):
    j = pl.program_id(1)

    @pl.when(j == 0)
    def _():
        r = _mod_row(pl.program_id(0), tm, rows_per_mod, fixed_row)
        xn = _modulated_norm(x_ref[...], g_ref[...], sh_ref[pl.ds(r, 1), :], sc_ref[pl.ds(r, 1), :])
        xn_ref[...] = xn.astype(BF16)

    @pl.when(j < n_main)
    def _():
        o_ref[...] = _dot(xn_ref[...], w_ref[...])

    @pl.when(j >= n_main)
    def _():
        o_ref[...] = _dot(xn_ref[...], wl_ref[...])


def _proj(x2d, g, mod, w_in, w_lora, layer, rows_per_mod, fixed_row):
    n, d = x2d.shape
    dp, tn = w_in.shape[2], w_lora.shape[2]
    tm = min(512, n)
    n_main = dp // tn
    body = functools.partial(_proj_body, tm=tm, n_main=n_main, rows_per_mod=rows_per_mod, fixed_row=fixed_row)
    return pl.pallas_call(
        body,
        grid=(n // tm, n_main + 1),
        in_specs=[pl.BlockSpec((tm, d), lambda i, j: (i, 0)),
                  pl.BlockSpec((None, 1, d), lambda i, j: (layer, 0, 0)),
                  pl.BlockSpec((None, MOD_ROWS, d), lambda i, j: (layer, 0, 0)),
                  pl.BlockSpec((None, MOD_ROWS, d), lambda i, j: (layer, 0, 1)),
                  pl.BlockSpec((None, d, tn), lambda i, j: (layer, 0, jnp.minimum(j, n_main - 1))),
                  pl.BlockSpec((None, d, tn), lambda i, j: (layer, 0, 0))],
        out_specs=pl.BlockSpec((tm, tn), lambda i, j: (i, j)),
        out_shape=jax.ShapeDtypeStruct((n, dp + tn), F32),
        scratch_shapes=[pltpu.VMEM((tm, d), BF16)],
        compiler_params=_params("parallel", "arbitrary"),
    )(x2d, g, mod, mod, w_in, w_lora)


def _rope(x, cos, sin_signed):
    lane = _iota(x.shape, 1)
    swapped = jnp.where((lane % 64) < 32, pltpu.roll(x, 96, axis=1), pltpu.roll(x, 32, axis=1))
    return x * cos + swapped * sin_signed


def _softmax_pv(parts, sink_col, vall):
    m = sink_col
    for s in parts:
        m = jnp.maximum(m, jnp.max(s, axis=-1, keepdims=True))
    ps = [jnp.exp(s - m) for s in parts]
    denom = jnp.exp(sink_col - m)
    for p in ps:
        denom = denom + jnp.sum(p, axis=-1, keepdims=True)
    p = ps[0] if len(ps) == 1 else jnp.concatenate(ps, axis=1)
    return _dot(p.astype(BF16), vall) / denom


def _attn_body(q_ref, kp_ref, kc_ref, kn_ref, vp_ref, vc_ref, vn_ref, kx_ref, vx_ref,
               cp_ref, cc_ref, cn_ref, sp_ref, sc_ref, sn_ref, sink_ref, o_ref, *, nb, group):
    n = pl.program_id(1)
    blk = ATTN_BLOCK
    cos = (cp_ref[...], cc_ref[...], cn_ref[...])
    sin = (sp_ref[...], sc_ref[...], sn_ref[...])
    qi = _iota((group * blk, 3 * blk), 0) % blk
    kj = _iota((group * blk, 3 * blk), 1)
    in_seq = ((kj >= blk) | (n > 0)) & ((kj < 2 * blk) | (n < nb - 1))
    band_ok = (jnp.abs(kj - blk - qi) <= WINDOW) & in_seq
    scale = HEAD_DIM ** -0.5
    for h in range(KV_HEADS):
        hs = slice(h * HEAD_DIM, (h + 1) * HEAD_DIM)
        kb = [_rope(r[:, hs], c, s) for r, c, s in zip((kp_ref, kc_ref, kn_ref), cos, sin)]
        kall = jnp.concatenate(kb + [kx_ref[:, hs]], axis=0).astype(BF16)
        vall = jnp.concatenate([vp_ref[:, hs], vc_ref[:, hs], vn_ref[:, hs], vx_ref[:, hs]], axis=0).astype(BF16)
        qs = []
        for g in range(group):
            c0 = (h * group + g) * HEAD_DIM
            qs.append(_rope(q_ref[:, c0:c0 + HEAD_DIM], cos[1], sin[1]))
        qh = jnp.concatenate(qs, axis=0).astype(BF16)
        s = _dot_nt(qh, kall) * scale
        s_loc = jnp.where(band_ok, s[:, :3 * blk], MASKED)
        o = _softmax_pv([s_loc, s[:, 3 * blk:]], sink_ref[h], vall)
        for g in range(group):
            c0 = (h * group + g) * HEAD_DIM
            o_ref[:, c0:c0 + HEAD_DIM] = o[g * blk:(g + 1) * blk].astype(o_ref.dtype)


def _latent_attention(px, pc, cos_t, sin_t, sink_col, b, t, l, aq, akv):
    blk = ATTN_BLOCK
    nb = t // blk
    group = aq // akv
    kcol, vcol = aq // akv, aq // akv + 1
    prev = lambda n: jnp.maximum(n - 1, 0)
    nxt = lambda n: jnp.minimum(n + 1, nb - 1)
    kv_spec = lambda col, f: pl.BlockSpec((blk, akv), lambda bi, n: (bi * nb + f(n), col))
    tab_spec = lambda f: pl.BlockSpec((blk, HEAD_DIM), lambda bi, n: (f(n), 0))
    ident = lambda n: n
    body = functools.partial(_attn_body, nb=nb, group=group)
    return pl.pallas_call(
        body,
        grid=(b, nb),
        in_specs=[pl.BlockSpec((blk, aq), lambda bi, n: (bi * nb + n, 0)),
                  kv_spec(kcol, prev), kv_spec(kcol, ident), kv_spec(kcol, nxt),
                  kv_spec(vcol, prev), kv_spec(vcol, ident), kv_spec(vcol, nxt),
                  pl.BlockSpec((l, akv), lambda bi, n: (bi, kcol)),
                  pl.BlockSpec((l, akv), lambda bi, n: (bi, vcol)),
                  tab_spec(prev), tab_spec(ident), tab_spec(nxt),
                  tab_spec(prev), tab_spec(ident), tab_spec(nxt),
                  pl.BlockSpec((KV_HEADS, group * blk, 1), lambda bi, n: (0, 0, 0))],
        out_specs=pl.BlockSpec((blk, aq), lambda bi, n: (bi * nb + n, 0)),
        out_shape=jax.ShapeDtypeStruct((b * t, aq), BF16),
        compiler_params=_params("parallel", "parallel"),
    )(px, px, px, px, px, px, px, pc, pc, cos_t, cos_t, cos_t, sin_t, sin_t, sin_t, sink_col)


def _ctx_attn_body(q_ref, k_ref, v_ref, sink_ref, o_ref, *, group):
    scale = HEAD_DIM ** -0.5
    for h in range(KV_HEADS):
        hs = slice(h * HEAD_DIM, (h + 1) * HEAD_DIM)
        kall = k_ref[:, hs].astype(BF16)
        vall = v_ref[:, hs].astype(BF16)
        for g in range(group):
            c0 = (h * group + g) * HEAD_DIM
            s = _dot_nt(q_ref[:, c0:c0 + HEAD_DIM].astype(BF16), kall) * scale
            o = _softmax_pv([s], sink_ref[h * group + g], vall)
            o_ref[:, c0:c0 + HEAD_DIM] = o.astype(o_ref.dtype)


def _context_attention(pc, sink_rows, b, l, aq, akv):
    group = aq // akv
    kcol, vcol = aq // akv, aq // akv + 1
    return pl.pallas_call(
        functools.partial(_ctx_attn_body, group=group),
        grid=(b,),
        in_specs=[pl.BlockSpec((l, aq), lambda bi: (bi, 0)),
                  pl.BlockSpec((l, akv), lambda bi: (bi, kcol)),
                  pl.BlockSpec((l, akv), lambda bi: (bi, vcol)),
                  pl.BlockSpec((KV_HEADS * group, l, 1), lambda bi: (0, 0, 0))],
        out_specs=pl.BlockSpec((l, aq), lambda bi: (bi, 0)),
        out_shape=jax.ShapeDtypeStruct((b * l, aq), BF16),
        compiler_params=_params("parallel"),
    )(pc, pc, pc, sink_rows)


def _cmlp_body(u_ref, gv_ref, g_ref, ws_ref, bs_ref, o_ref, *, groups):
    u = jax.nn.gelu(u_ref[...])
    gv = jax.nn.gelu(gv_ref[...])
    gvn = gv * lax.rsqrt(jnp.mean(gv * gv, axis=-1, keepdims=True) + EPS) * g_ref[...]
    for gi in range(groups):
        cs = slice(gi * CMLP_CH, (gi + 1) * CMLP_CH)
        mixed = _dot(ws_ref[gi].astype(BF16), gvn[:, cs].astype(BF16)) + bs_ref[gi]
        o_ref[:, cs] = (u[:, cs] * mixed).astype(o_ref.dtype)


def _chunk_mlp(p, norm_g, ws, bs_b, ucol, cw):
    n = p.shape[0]
    groups = cw // CMLP_CH
    ch = CMLP_CHUNK
    return pl.pallas_call(
        functools.partial(_cmlp_body, groups=groups),
        grid=(n // ch,),
        in_specs=[pl.BlockSpec((ch, cw), lambda i: (i, ucol)),
                  pl.BlockSpec((ch, cw), lambda i: (i, ucol + 1)),
                  pl.BlockSpec((1, cw), lambda i: (0, 0)),
                  pl.BlockSpec((groups, ch, ch), lambda i: (0, 0, 0)),
                  pl.BlockSpec((groups, ch, CMLP_CH), lambda i: (0, 0, 0))],
        out_specs=pl.BlockSpec((ch, cw), lambda i: (i, 0)),
        out_shape=jax.ShapeDtypeStruct((n, cw), BF16),
        compiler_params=_params("parallel"),
    )(p, p, norm_g, ws, bs_b)


def _head_sum(x):
    ones = (_iota((128, 128), 0) // RWKV_HEAD == _iota((128, 128), 1) // RWKV_HEAD).astype(F32)
    cols = [_dot(x[:, s * 128:(s + 1) * 128], ones, precision=HIGHEST) for s in range(x.shape[1] // 128)]
    return cols[0] if len(cols) == 1 else jnp.concatenate(cols, axis=1)


def _prep_body(r_ref, k_ref, v_ref, rp_ref, kp_ref, vp_ref, rn_ref, kn_ref, vn_ref,
               cr_ref, ck_ref, cv_ref, hw_ref, ha_ref, w2_ref, a2_ref, w0_ref, a0_ref, kkp_ref, kap_ref,
               ro_ref, ko_ref, vo_ref, kko_ref, lw_ref, bo_ref, kr_ref, *, tr, seq, lora):
    i = pl.program_id(0)
    first = (i * tr) % seq == 0
    last = ((i + 1) * tr) % seq == 0
    row = _iota(r_ref.shape, 0)

    def conv(x_ref, xp_ref, xn_ref, w_ref):
        x = x_ref[...]
        before = jnp.where(first, 0.0, xp_ref[7:8, :])
        after = jnp.where(last, 0.0, xn_ref[0:1, :])
        xm = jnp.where(row == 0, before, pltpu.roll(x, 1, axis=0))
        xp = jnp.where(row == tr - 1, after, pltpu.roll(x, tr - 1, axis=0))
        return xm * w_ref[0:1, :] + x * w_ref[1:2, :] + xp * w_ref[2:3, :]

    r = conv(r_ref, rp_ref, rn_ref, cr_ref)
    k = conv(k_ref, kp_ref, kn_ref, ck_ref)
    v = conv(v_ref, vp_ref, vn_ref, cv_ref)
    kk = k * kkp_ref[...]
    kk = kk * lax.rsqrt(_head_sum(kk * kk) + 1e-12)
    ro_ref[...] = r
    ko_ref[...] = k
    vo_ref[...] = v
    kko_ref[...] = kk
    for z in range(2):
        zs = slice(z * lora, (z + 1) * lora)
        w_raw = w0_ref[z:z + 1, :] + _dot(jnp.tanh(hw_ref[:, zs]).astype(BF16), w2_ref[z].astype(BF16))
        softplus_neg = jnp.maximum(-w_raw, 0.0) + jnp.log1p(jnp.exp(-jnp.abs(w_raw)))
        lw_ref[z] = -jnp.exp(-softplus_neg - 0.5)
        a = jax.nn.sigmoid(a0_ref[z:z + 1, :] + _dot(ha_ref[:, zs].astype(BF16), a2_ref[z].astype(BF16)))
        kr_ref[z] = k * (1.0 + (a - 1.0) * kap_ref[...])
        bo_ref[z] = kk * a


def _rwkv_prep(p, conv_w, w2, a2, w0, a0, kk_p, ka_p, seq, rcol, hcol, rw, lora):
    n = p.shape[0]
    tr = min(256, seq)
    nh = n // 8
    body = functools.partial(_prep_body, tr=tr, seq=seq, lora=lora)
    cur = lambda c: pl.BlockSpec((tr, rw), lambda i: (i, rcol + c))
    prv = lambda c: pl.BlockSpec((8, rw), lambda i: (jnp.maximum(i * (tr // 8) - 1, 0), rcol + c))
    nxt = lambda c: pl.BlockSpec((8, rw), lambda i: (jnp.minimum((i + 1) * (tr // 8), nh - 1), rcol + c))
    cw = lambda c: pl.BlockSpec((3, rw), lambda i: (0, c))
    full2 = lambda shape: pl.BlockSpec(shape, lambda i: (0,) * len(shape))
    shared = pl.BlockSpec((tr, rw), lambda i: (i, 0))
    directed = pl.BlockSpec((2, tr, rw), lambda i: (0, i, 0))
    return pl.pallas_call(
        body,
        grid=(n // tr,),
        in_specs=[cur(0), cur(1), cur(2), prv(0), prv(1), prv(2), nxt(0), nxt(1), nxt(2),
                  cw(0), cw(1), cw(2),
                  pl.BlockSpec((tr, 2 * lora), lambda i: (i, hcol)),
                  pl.BlockSpec((tr, 2 * lora), lambda i: (i, hcol + 1)),
                  full2((2, lora, rw)), full2((2, lora, rw)), full2((2, rw)), full2((2, rw)),
                  full2((1, rw)), full2((1, rw))],
        out_specs=[shared, shared, shared, shared, directed, directed, directed],
        out_shape=[jax.ShapeDtypeStruct((n, rw), F32)] * 4 + [jax.ShapeDtypeStruct((2, n, rw), F32)] * 3,
        compiler_params=_params("parallel"),
    )(p, p, p, p, p, p, p, p, p, conv_w, conv_w, conv_w, p, p, w2, a2, w0, a0, kk_p, ka_p)


def _mm(a, b, passes, kind="nn"):
    fn = {"nn": _dot, "nt": _dot_nt, "tn": _dot_tn}[kind]
    if passes == 6:
        return fn(a, b, precision=HIGHEST)
    a_hi, b_hi = a.astype(BF16), b.astype(BF16)
    if passes == 1:
        return fn(a_hi, b_hi)
    a_lo = (a - a_hi.astype(F32)).astype(BF16)
    b_lo = (b - b_hi.astype(F32)).astype(BF16)
    return fn(a_hi, b_hi) + (fn(a_lo, b_hi) + fn(a_hi, b_lo))


def _unit_tri_inverse(nmats, eye, same16, same32, passes):
    n16 = [jnp.where(same16, n, 0.0) for n in nmats]
    xs = [eye - n for n in n16]
    pw = n16
    for _ in range(3):
        pw = [_mm(p, p, passes) for p in pw]
        xs = [x + _mm(x, p, passes) for x, p in zip(xs, pw)]
    for mask in (same32 & ~same16, ~same32):
        offs = [jnp.where(mask, n, 0.0) for n in nmats]
        xo = [_mm(x, o, passes) for x, o in zip(xs, offs)]
        xs = [x - _mm(t, x, passes) for x, t in zip(xs, xo)]
    return xs


def _scan_body(r_ref, kk_ref, v_ref, lw_ref, b_ref, k_ref, s0_ref, y_ref, sf_ref, st_ref, *, heads, nchunks):
    z = pl.program_id(0)
    c = pl.program_id(3)
    C, K = SCAN_CHUNK, RWKV_HEAD
    pp = SCAN_PASSES

    @pl.when(c == 0)
    def _():
        st_ref[...] = s0_ref[...]

    ti, si = _iota((C, C), 0), _iota((C, C), 1)
    before = (si - ti) * (1 - 2 * z) < 0
    upto = before | (si == ti)
    eye = (si == ti).astype(F32)
    same16 = (ti // 16) == (si // 16)
    same32 = (ti // 32) == (si // 32)

    lw = lw_ref[...]
    lc = _dot(upto.astype(F32), lw, precision=HIGHEST)
    ltot = jnp.sum(lw, axis=0, keepdims=True)
    e_neg = jnp.exp(-lc)
    e_out = jnp.exp(ltot - lc)
    kkt = kk_ref[...] * jnp.exp(lc - lw)
    rt = r_ref[...] * jnp.exp(lc)
    bt = b_ref[...] * e_neg
    kt = k_ref[...] * e_neg
    bh = b_ref[...] * e_out
    kh = k_ref[...] * e_out
    etot = jnp.exp(ltot)
    v = v_ref[...]

    hr = range(heads)
    ls = [slice(h * K, (h + 1) * K) for h in hr]
    ps = [_mm(jnp.concatenate([kkt[:, s], rt[:, s]], axis=0),
              jnp.concatenate([bt[:, s], kt[:, s]], axis=0), pp["pair"], "nt") for s in ls]
    nmats = [jnp.where(before, p[:C, :C], 0.0) for p in ps]
    pkk = [jnp.where(before, p[:C, C:], 0.0) for p in ps]
    prb = [jnp.where(upto, p[C:, :C], 0.0) for p in ps]
    prk = [jnp.where(upto, p[C:, C:], 0.0) for p in ps]
    tinv = _unit_tri_inverse(nmats, eye, same16, same32, pp["inv"])
    tg = [_mm(tinv[h], jnp.concatenate([kkt[:, ls[h]], pkk[h]], axis=1), pp["solve"]) for h in hr]
    qa = [jnp.concatenate([rt[:, ls[h]], prk[h]], axis=1) - _mm(prb[h], tg[h], pp["solve"]) for h in hr]
    m3 = [_mm(bh[:, ls[h]], tg[h], pp["solve"], "tn") for h in hr]
    sv = [jnp.concatenate([st_ref[h], v[:, ls[h]]], axis=0) for h in hr]
    ys = [_mm(qa[h], sv[h], pp["state"]) for h in hr]
    for h in hr:
        decay_diag = eye * jnp.broadcast_to(etot[:, ls[h]], (K, K))
        trans = jnp.concatenate([decay_diag, jnp.zeros((K, C), F32)], axis=1) - m3[h]
        st_ref[h] = _mm(trans, sv[h], pp["state"]) + _mm(kh[:, ls[h]], v[:, ls[h]], pp["state"], "tn")
    y_ref[...] = ys[0] if heads == 1 else jnp.concatenate(ys, axis=1)

    @pl.when(c == nchunks - 1)
    def _():
        sf_ref[...] = st_ref[...]


def _rwkv_scan(r, kk, v, lw, bb, kr, s0, b, seq, rw):
    C, K = SCAN_CHUNK, RWKV_HEAD
    nchunks = seq // C
    heads = min(SCAN_HEADS, rw // K)
    ngroups = rw // (heads * K)
    n = b * seq
    row = lambda z, bi, hg, c: bi * nchunks + c + z * (nchunks - 1 - 2 * c)
    shared = pl.BlockSpec((C, heads * K), lambda z, bi, hg, c: (row(z, bi, hg, c), hg))
    directed = pl.BlockSpec((None, C, heads * K), lambda z, bi, hg, c: (z, row(z, bi, hg, c), hg))
    state = pl.BlockSpec((None, None, heads, K, K), lambda z, bi, hg, c: (z, bi, hg, 0, 0))
    body = functools.partial(_scan_body, heads=heads, nchunks=nchunks)
    return pl.pallas_call(
        body,
        grid=(2, b, ngroups, nchunks),
        in_specs=[shared, shared, shared, directed, directed, directed, state],
        out_specs=[directed, state],
        out_shape=[jax.ShapeDtypeStruct((2, n, rw), F32),
                   jax.ShapeDtypeStruct((2, b, rw // K, K, K), F32)],
        scratch_shapes=[pltpu.VMEM((heads, K, K), F32)],
        compiler_params=_params("parallel", "parallel", "parallel", "arbitrary"),
    )(r, kk, v, lw, bb, kr, s0)


def _rwkv_out_body(yf_ref, yb_ref, r_ref, k_ref, v_ref, g_ref, rk_ref, lnw_ref, lnb_ref, o_ref):
    y = yf_ref[...] + yb_ref[...]
    inv = 1.0 / RWKV_HEAD
    mu = _head_sum(y) * inv
    d = y - mu
    var = _head_sum(d * d) * inv
    yn = d * lax.rsqrt(var + RWKV_GN_EPS) * lnw_ref[...] + lnb_ref[...]
    bonus = _head_sum(r_ref[...] * k_ref[...] * rk_ref[...]) * v_ref[...]
    o_ref[...] = ((yn + bonus) * jax.nn.sigmoid(g_ref[...])).astype(o_ref.dtype)


def _rwkv_output(y, r, k, v, p, gcol, rk, ln_w, ln_b):
    n, rw = r.shape
    tr = min(256, n)
    shared = pl.BlockSpec((tr, rw), lambda i: (i, 0))
    vec = pl.BlockSpec((1, rw), lambda i: (0, 0))
    return pl.pallas_call(
        _rwkv_out_body,
        grid=(n // tr,),
        in_specs=[pl.BlockSpec((None, tr, rw), lambda i: (0, i, 0)),
                  pl.BlockSpec((None, tr, rw), lambda i: (1, i, 0)),
                  shared, shared, shared,
                  pl.BlockSpec((tr, rw), lambda i: (i, gcol)),
                  vec, vec, vec],
        out_specs=shared,
        out_shape=jax.ShapeDtypeStruct((n, rw), BF16),
        compiler_params=_params("parallel"),
    )(y, y, r, k, v, p, rk, ln_w, ln_b)


def _wout_body(a_ref, c_ref, r_ref, wa_ref, wc_ref, wr_ref, x_ref, g_ref, o_ref, *, tm, rows_per_mod, fixed_row):
    row = _mod_row(pl.program_id(0), tm, rows_per_mod, fixed_row)
    acc = _dot(a_ref[...], wa_ref[...]) + _dot(c_ref[...], wc_ref[...]) + _dot(r_ref[...], wr_ref[...])
    o_ref[...] = x_ref[...] + g_ref[pl.ds(row, 1), :] * acc


def _out_proj(attn, cmlp, rwkv, w_out, x2d, mod, layer, rows_per_mod, fixed_row):
    n, d = x2d.shape
    aq, cw, rw = attn.shape[1], cmlp.shape[1], rwkv.shape[1]
    tm = min(512, n)
    tn = min(1024, d)
    body = functools.partial(_wout_body, tm=tm, rows_per_mod=rows_per_mod, fixed_row=fixed_row)
    return pl.pallas_call(
        body,
        grid=(n // tm, d // tn),
        in_specs=[pl.BlockSpec((tm, aq), lambda i, j: (i, 0)),
                  pl.BlockSpec((tm, cw), lambda i, j: (i, 0)),
                  pl.BlockSpec((tm, rw), lambda i, j: (i, 0)),
                  pl.BlockSpec((None, aq, tn), lambda i, j: (layer, 0, j)),
                  pl.BlockSpec((None, cw, tn), lambda i, j: (layer, aq // cw, j)),
                  pl.BlockSpec((None, rw, tn), lambda i, j: (layer, (aq + cw) // rw, j)),
                  pl.BlockSpec((tm, tn), lambda i, j: (i, j)),
                  pl.BlockSpec((None, MOD_ROWS, tn), lambda i, j: (layer, 0, 2 * (d // tn) + j))],
        out_specs=pl.BlockSpec((tm, tn), lambda i, j: (i, j)),
        out_shape=jax.ShapeDtypeStruct((n, d), F32),
        compiler_params=_params("parallel", "parallel"),
    )(attn, cmlp, rwkv, w_out, w_out, w_out, x2d, mod)


def _router_body(x_ref, g_ref, sh_ref, sc_ref, rw_ref, rb_ref, zn_ref, ids_ref, wt_ref, *,
                 tm, rows_per_mod, fixed_row, experts):
    row = _mod_row(pl.program_id(0), tm, rows_per_mod, fixed_row)
    zn = _modulated_norm(x_ref[...], g_ref[...], sh_ref[pl.ds(row, 1), :], sc_ref[pl.ds(row, 1), :])
    zn_ref[...] = zn
    logits = _dot(zn, rw_ref[...], precision=HIGHEST).T
    per_group = experts // N_EXPERT_GROUPS
    scores = [jax.nn.sigmoid(logits[e:e + 1, :]) for e in range(experts)]
    sel = [scores[e] + rb_ref[e:e + 1, :] for e in range(experts)]
    best_val, best_grp = None, None
    for gi in range(N_EXPERT_GROUPS):
        mem = sel[gi * per_group:(gi + 1) * per_group]
        top2 = None
        for a in range(per_group):
            for b2 in range(a + 1, per_group):
                pair = mem[a] + mem[b2]
                top2 = pair if top2 is None else jnp.maximum(top2, pair)
        if gi == 0:
            best_val, best_grp = top2, jnp.zeros(top2.shape, jnp.int32)
        else:
            better = top2 > best_val
            best_grp = jnp.where(better, gi, best_grp)
            best_val = jnp.where(better, top2, best_val)
    chosen, picked = [], []
    for e in range(experts):
        gi = e // per_group
        rank = jnp.zeros(best_grp.shape, jnp.int32)
        for j in range(gi * per_group, (gi + 1) * per_group):
            if j != e:
                ahead = (sel[j] > sel[e]) | ((sel[j] == sel[e]) & (j < e))
                rank = rank + ahead.astype(jnp.int32)
        chosen.append((best_grp == gi) & (rank < TOP_K))
        picked.append(jnp.where(chosen[e], scores[e], 0.0))
    total = picked[0]
    for e in range(1, experts):
        total = total + picked[e]
    zero_i, zero_f = jnp.zeros(total.shape, jnp.int32), jnp.zeros(total.shape, F32)
    seen, ids, wts = zero_i, [zero_i, zero_i], [zero_f, zero_f]
    for e in range(experts):
        gate = picked[e] / total
        for slot in range(TOP_K):
            here = chosen[e] & (seen == slot)
            ids[slot] = jnp.where(here, e, ids[slot])
            wts[slot] = jnp.where(here, gate, wts[slot])
        seen = seen + chosen[e].astype(jnp.int32)
    ids_ref[...] = jnp.concatenate(ids + [jnp.zeros((8 - TOP_K, tm), jnp.int32)], axis=0)
    wt_ref[...] = jnp.concatenate(wts + [jnp.zeros((128 - TOP_K, tm), F32)], axis=0).T


def _router(x2d, g, mod, router_w_pad, router_b_col, layer, rows_per_mod, fixed_row, experts):
    n, d = x2d.shape
    tm = min(256, n)
    body = functools.partial(_router_body, tm=tm, rows_per_mod=rows_per_mod, fixed_row=fixed_row, experts=experts)
    return pl.pallas_call(
        body,
        grid=(n // tm,),
        in_specs=[pl.BlockSpec((tm, d), lambda i: (i, 0)),
                  pl.BlockSpec((None, 1, d), lambda i: (layer, 0, 0)),
                  pl.BlockSpec((None, MOD_ROWS, d), lambda i: (layer, 0, 3)),
                  pl.BlockSpec((None, MOD_ROWS, d), lambda i: (layer, 0, 4)),
                  pl.BlockSpec((d, 128), lambda i: (0, 0)),
                  pl.BlockSpec((128, 1), lambda i: (0, 0))],
        out_specs=[pl.BlockSpec((tm, d), lambda i: (i, 0)),
                   pl.BlockSpec((8, tm), lambda i: (0, i)),
                   pl.BlockSpec((tm, 128), lambda i: (i, 0))],
        out_shape=[jax.ShapeDtypeStruct((n, d), F32), jax.ShapeDtypeStruct((8, n), jnp.int32),
                   jax.ShapeDtypeStruct((n, 128), F32)],
        compiler_params=_params("parallel"),
    )(x2d, g, mod, mod, router_w_pad, router_b_col)


def _route_plan(ids, experts, tm):
    n = ids.shape[1]
    total = TOP_K * n + experts * tm
    flat = ids.reshape(-1)
    onehot = (flat[:, None] == jnp.arange(experts, dtype=jnp.int32)[None, :]).astype(jnp.int32)
    rank = jnp.cumsum(onehot, axis=0) - onehot
    padded = ((jnp.sum(onehot, axis=0) + tm - 1) // tm) * tm
    ends = jnp.cumsum(padded)
    pos = (ends - padded)[flat] + jnp.sum(rank * onehot, axis=1)
    row_token = jnp.zeros((total,), jnp.int32).at[pos].set(jnp.tile(jnp.arange(n, dtype=jnp.int32), TOP_K))
    tile_start = jnp.arange(total // tm, dtype=jnp.int32) * tm
    tile_used = (tile_start < ends[-1]).astype(jnp.int32)
    tile_expert = jnp.minimum(jnp.searchsorted(ends, tile_start, side="right"), experts - 1).astype(jnp.int32)
    last_used = tile_expert[jnp.maximum(ends[-1] // tm - 1, 0)]
    tile_expert = jnp.where(tile_used == 1, tile_expert, last_used)
    return pos.astype(jnp.int32), row_token, tile_expert, tile_used


def _row_copy(src_hbm, row, dst_ref, i, sem):
    return pltpu.make_async_copy(src_hbm.at[pl.ds(row, 1), :], dst_ref.at[pl.ds(i, 1), :], sem)


def _gather_rows(src_hbm, dst_ref, sem, index_of, count):
    def start(i, carry):
        _row_copy(src_hbm, index_of(i), dst_ref, i, sem).start()
        return carry

    lax.fori_loop(0, count, start, 0, unroll=8)
    pltpu.make_async_copy(src_hbm.at[pl.ds(0, count), :], dst_ref, sem).wait()


def _expert_hidden_body(te_ref, used_ref, tok_ref, zn_hbm, w1_ref, w3_ref, h_ref, xg_ref, sem, *, tm):
    p, f = pl.program_id(0), pl.program_id(1)
    used = used_ref[p] == 1

    @pl.when(used & (f == 0))
    def _():
        _gather_rows(zn_hbm, xg_ref, sem, lambda i: tok_ref[p * tm + i], tm)

    @pl.when(used)
    def _():
        x = xg_ref[...].astype(BF16)
        h1 = _dot(x, w1_ref[...].astype(BF16))
        h3 = _dot(x, w3_ref[...].astype(BF16))
        h_ref[...] = ((h1 * jax.nn.sigmoid(h1)) * h3).astype(h_ref.dtype)

    @pl.when(jnp.logical_not(used))
    def _():
        h_ref[...] = jnp.zeros_like(h_ref)


def _expert_out_body(te_ref, used_ref, h_ref, w2_ref, o_ref):
    o_ref[...] = _dot(h_ref[...], w2_ref[...].astype(BF16))


def _experts(zn, row_token, tile_expert, tile_used, w1, w3, w2, layer, tm):
    n, d = zn.shape
    ff = w1.shape[3]
    total = row_token.shape[0]
    ntiles = total // tm
    tf = min(256, ff)
    n_f = ff // tf
    tn = min(2048, d)
    n_j = d // tn
    hold = lambda used, p, j, last: jnp.where(used[p] == 1, j, last)
    hidden = pl.pallas_call(
        functools.partial(_expert_hidden_body, tm=tm),
        grid_spec=pltpu.PrefetchScalarGridSpec(
            num_scalar_prefetch=3,
            grid=(ntiles, n_f),
            in_specs=[pl.BlockSpec(memory_space=pl.ANY),
                      pl.BlockSpec((None, None, d, tf),
                                   lambda p, f, te, us, tk: (layer, te[p], 0, hold(us, p, f, n_f - 1))),
                      pl.BlockSpec((None, None, d, tf),
                                   lambda p, f, te, us, tk: (layer, te[p], 0, hold(us, p, f, n_f - 1)))],
            out_specs=pl.BlockSpec((tm, tf), lambda p, f, te, us, tk: (p, f)),
            scratch_shapes=[pltpu.VMEM((tm, d), F32), pltpu.SemaphoreType.DMA(())]),
        out_shape=jax.ShapeDtypeStruct((total, ff), BF16),
        compiler_params=_params("arbitrary", "arbitrary", disable_bounds_checks=True),
    )(tile_expert, tile_used, row_token, zn, w1, w3)
    return pl.pallas_call(
        _expert_out_body,
        grid_spec=pltpu.PrefetchScalarGridSpec(
            num_scalar_prefetch=2,
            grid=(ntiles, n_j),
            in_specs=[pl.BlockSpec((tm, ff), lambda p, j, te, us: (p, 0)),
                      pl.BlockSpec((None, None, ff, tn),
                                   lambda p, j, te, us: (layer, te[p], 0, hold(us, p, j, n_j - 1)))],
            out_specs=pl.BlockSpec((tm, tn), lambda p, j, te, us: (p, j))),
        out_shape=jax.ShapeDtypeStruct((total, d), F32),
        compiler_params=_params("parallel", "parallel"),
    )(tile_expert, tile_used, hidden, w2)


def _combine_body(pos_ref, ys_hbm, wt_ref, x_ref, g2_ref, fg_ref, o_ref, ya_ref, yb_ref, sem_a, sem_b, *,
                  tm, n, rows_per_mod, fixed_row, final_norm):
    i = pl.program_id(0)
    _gather_rows(ys_hbm, ya_ref, sem_a, lambda r: pos_ref[i * tm + r], tm)
    _gather_rows(ys_hbm, yb_ref, sem_b, lambda r: pos_ref[n + i * tm + r], tm)
    row = _mod_row(i, tm, rows_per_mod, fixed_row)
    wt = wt_ref[...]
    mix = wt[:, 0:1] * ya_ref[...] + wt[:, 1:2] * yb_ref[...]
    out = x_ref[...] + g2_ref[pl.ds(row, 1), :] * mix
    if final_norm:
        out = out * lax.rsqrt(jnp.mean(out * out, axis=-1, keepdims=True) + EPS) * fg_ref[...]
    o_ref[...] = out


def _moe_combine(ys, pos, wts, x2d, mod, final_g, layer, rows_per_mod, fixed_row, final_norm):
    n, d = x2d.shape
    tm = min(256, n)
    body = functools.partial(_combine_body, tm=tm, n=n, rows_per_mod=rows_per_mod, fixed_row=fixed_row,
                             final_norm=final_norm)
    return pl.pallas_call(
        body,
        grid_spec=pltpu.PrefetchScalarGridSpec(
            num_scalar_prefetch=1,
            grid=(n // tm,),
            in_specs=[pl.BlockSpec(memory_space=pl.ANY),
                      pl.BlockSpec((tm, 128), lambda i, ps: (i, 0)),
                      pl.BlockSpec((tm, d), lambda i, ps: (i, 0)),
                      pl.BlockSpec((None, MOD_ROWS, d), lambda i, ps: (layer, 0, 5)),
                      pl.BlockSpec((1, d), lambda i, ps: (0, 0))],
            out_specs=pl.BlockSpec((tm, d), lambda i, ps: (i, 0)),
            scratch_shapes=[pltpu.VMEM((tm, d), F32), pltpu.VMEM((tm, d), F32),
                            pltpu.SemaphoreType.DMA(()), pltpu.SemaphoreType.DMA(())]),
        out_shape=jax.ShapeDtypeStruct((n, d), F32),
        compiler_params=_params("arbitrary", disable_bounds_checks=True),
    )(pos, ys, wts, x2d, mod, final_g)


def _moe(x2d, g, mod, router_w_pad, router_b_col, w1, w3, w2, final_g, layer, experts,
         rows_per_mod, fixed_row, final_norm=False):
    n = x2d.shape[0]
    tm = 512 if n >= 8 * 512 else 128
    zn, ids, wts = _router(x2d, g, mod, router_w_pad, router_b_col, layer, rows_per_mod, fixed_row, experts)
    pos, row_token, tile_expert, tile_used = _route_plan(ids[:TOP_K], experts, tm)
    ys = _experts(zn, row_token, tile_expert, tile_used, w1, w3, w2, layer, tm)
    return _moe_combine(ys, pos, wts, x2d, mod, final_g, layer, rows_per_mod, fixed_row, final_norm)


def _rope_tables(t):
    pos = jnp.arange(t)
    half = HEAD_DIM // 4
    freqs = ROPE_BASE ** (-jnp.arange(half, dtype=F32) / half)
    ang_r = (pos // GRID_W).astype(F32)[:, None] * freqs[None, :]
    ang_c = (pos % GRID_W).astype(F32)[:, None] * freqs[None, :]
    cos_t = jnp.concatenate([jnp.cos(ang_r)] * 2 + [jnp.cos(ang_c)] * 2, axis=1)
    sin_t = jnp.concatenate([-jnp.sin(ang_r), jnp.sin(ang_r), -jnp.sin(ang_c), jnp.sin(ang_c)], axis=1)
    return cos_t, sin_t


def kernel(x, c, ctx, c_ctx, ada_w, ada_b, norm1_g, w_in, rwkv_conv, attn_sink, cmlp_norm_g, cmlp_ws, cmlp_b,
           rwkv_w0, rwkv_w1, rwkv_w2, rwkv_a0, rwkv_a1, rwkv_a2, rwkv_kk, rwkv_ka, rwkv_rk, rwkv_ln_w, rwkv_ln_b,
           w_out, norm2_g, router_w, router_b, moe_w1, moe_w3, moe_w2, final_g):
    b, t, d = x.shape
    l = ctx.shape[1]
    depth = ada_w.shape[0]
    cw = cmlp_norm_g.shape[1]
    rw = rwkv_w0.shape[2]
    lora = rwkv_w1.shape[3]
    experts = router_w.shape[1]
    dp = w_in.shape[2]
    akv = KV_HEADS * HEAD_DIM
    aq = dp - 2 * akv - 2 * cw - 4 * rw
    group = aq // akv
    heads = rw // RWKV_HEAD
    ucol = (aq + 2 * akv) // cw
    rcol = (aq + 2 * akv + 2 * cw) // rw
    gcol = rcol + 3
    hcol = dp // (2 * lora)
    assert (aq + 2 * akv) % cw == 0 and (aq + 2 * akv + 2 * cw) % rw == 0 and dp % (2 * lora) == 0
    assert b + 1 <= MOD_ROWS and (b * t) % l == 0 and t % 256 == 0 and l % 128 == 0

    cpad = jnp.zeros((MOD_ROWS, d), F32).at[:b].set(c).at[b].set(c_ctx)
    mod = _ada(cpad, ada_w, ada_b)
    cos_t, sin_t = _rope_tables(t)
    router_w_pad = jnp.zeros((d, 128), F32).at[:, :experts].set(router_w)
    router_b_col = jnp.zeros((128, 1), F32).at[:experts, 0].set(router_b)
    w_out_bf = w_out.astype(BF16)
    w_in_bf = w_in.astype(BF16)
    w_lora_bf = jnp.concatenate([rwkv_w1[:, 0], rwkv_w1[:, 1], rwkv_a1[:, 0], rwkv_a1[:, 1]], axis=2).astype(BF16)
    assert dp % w_lora_bf.shape[2] == 0
    s_zero = jnp.zeros((2, b, heads, RWKV_HEAD, RWKV_HEAD), F32)

    xs = x.reshape(b * t, d)
    hs = ctx.reshape(b * l, d)
    for layer in range(depth):
        lat = dict(rows_per_mod=t, fixed_row=None)
        con = dict(rows_per_mod=None, fixed_row=b)
        last = layer == depth - 1
        px = _proj(xs, norm1_g.reshape(depth, 1, d), mod, w_in_bf, w_lora_bf, layer, **lat)
        pc = _proj(hs, norm1_g.reshape(depth, 1, d), mod, w_in_bf, w_lora_bf, layer, **con)

        sink = attn_sink[layer]
        sink_col = jnp.repeat(sink.reshape(KV_HEADS, group), ATTN_BLOCK, axis=1)[..., None]
        attn_x = _latent_attention(px, pc, cos_t, sin_t, sink_col, b, t, l, aq, akv)

        bs_b = jnp.broadcast_to(cmlp_b[layer][:, :, None], cmlp_b.shape[1:] + (CMLP_CH,))
        cmlp_x = _chunk_mlp(px, cmlp_norm_g[layer][None], cmlp_ws[layer], bs_b, ucol, cw)

        prep_args = (rwkv_conv[layer], rwkv_w2[layer], rwkv_a2[layer], rwkv_w0[layer], rwkv_a0[layer],
                     rwkv_kk[layer][None], rwkv_ka[layer][None])
        rc, kc, vc, kkc, lwc, bbc, krc = _rwkv_prep(pc, *prep_args, l, rcol, hcol, rw, lora)
        rx, kx, vx, kkx, lwx, bbx, krx = _rwkv_prep(px, *prep_args, t, rcol, hcol, rw, lora)
        y_c, s_ctx = _rwkv_scan(rc, kkc, vc, lwc, bbc, krc, s_zero, b, l, rw)
        y_x, _ = _rwkv_scan(rx, kkx, vx, lwx, bbx, krx, s_ctx, b, t, rw)
        out_args = (rwkv_rk[layer][None], rwkv_ln_w[layer][None], rwkv_ln_b[layer][None])
        rwkv_x = _rwkv_output(y_x, rx, kx, vx, px, gcol, *out_args)

        xs = _out_proj(attn_x, cmlp_x, rwkv_x, w_out_bf, xs, mod, layer, **lat)
        moe_args = (norm2_g.reshape(depth, 1, d), mod, router_w_pad, router_b_col, moe_w1, moe_w3, moe_w2,
                    final_g[None], layer, experts)
        xs = _moe(xs, *moe_args, final_norm=last, **lat)

        if not last:
            sink_rows = jnp.broadcast_to(sink[:, None, None], (KV_HEADS * group, l, 1))
            attn_c = _context_attention(pc, sink_rows, b, l, aq, akv)
            cmlp_c = _chunk_mlp(pc, cmlp_norm_g[layer][None], cmlp_ws[layer], bs_b, ucol, cw)
            rwkv_c = _rwkv_output(y_c, rc, kc, vc, pc, gcol, *out_args)
            hs = _out_proj(attn_c, cmlp_c, rwkv_c, w_out_bf, hs, mod, layer, **con)
            hs = _moe(hs, *moe_args, **con)
    return xs.reshape(b, t, d)
```

```python
import functools

import jax
import jax.numpy as jnp
from jax import lax
from jax.experimental import pallas as pl
from jax.experimental.pallas import tpu as pltpu

F32, BF16 = jnp.float32, jnp.bfloat16
HIGHEST = lax.Precision.HIGHEST

HEAD_DIM = 128
KV_HEADS = 4
WINDOW = 128
ATTN_BLOCK = 128
GRID_W = 64
ROPE_BASE = 10000.0
CMLP_CH = 128
CMLP_CHUNK = 128
RWKV_HEAD = 64
RWKV_GN_EPS = 64e-5
N_EXPERT_GROUPS = 4
TOP_K = 2
N_MOD = 6
EPS = 1e-6
MASKED = -1e30

MOD_ROWS = 8
SCAN_CHUNK = 64
SCAN_HEADS = 16
SCAN_PASSES = {"pair": 1, "inv": 1, "solve": 1, "state": 1}
VMEM_LIMIT_BYTES = 56 * 1024 * 1024


def _params(*sem, **kw):
    return pltpu.CompilerParams(dimension_semantics=sem, vmem_limit_bytes=VMEM_LIMIT_BYTES, **kw)


def _dot(a, b, **kw):
    return jnp.dot(a, b, preferred_element_type=F32, **kw)


def _dot_nt(a, b, **kw):
    return lax.dot_general(a, b, (((1,), (1,)), ((), ())), preferred_element_type=F32, **kw)


def _dot_tn(a, b, **kw):
    return lax.dot_general(a, b, (((0,), (0,)), ((), ())), preferred_element_type=F32, **kw)


def _iota(shape, dim):
    return lax.broadcasted_iota(jnp.int32, shape, dim)


def _ada_body(c_ref, w_ref, b_ref, o_ref):
    c = c_ref[...]
    a = (c * jax.nn.sigmoid(c)).astype(BF16)
    o_ref[...] = _dot(a, w_ref[...].astype(BF16)) + b_ref[...]


def _ada(cpad, ada_w, ada_b):
    depth, d, n = ada_w.shape
    tn = 512
    return pl.pallas_call(
        _ada_body,
        grid=(depth, n // tn),
        in_specs=[pl.BlockSpec((MOD_ROWS, d), lambda l, j: (0, 0)),
                  pl.BlockSpec((None, d, tn), lambda l, j: (l, 0, j)),
                  pl.BlockSpec((None, 1, tn), lambda l, j: (l, 0, j))],
        out_specs=pl.BlockSpec((None, MOD_ROWS, tn), lambda l, j: (l, 0, j)),
        out_shape=jax.ShapeDtypeStruct((depth, MOD_ROWS, n), F32),
        compiler_params=_params("parallel", "parallel"),
    )(cpad, ada_w, ada_b.reshape(depth, 1, n))


def _mod_row(i, tm, rows_per_mod, fixed_row):
    return fixed_row if rows_per_mod is None else (i * tm) // rows_per_mod


def _modulated_norm(x, g, shift, scale):
    y = x * lax.rsqrt(jnp.mean(x * x, axis=-1, keepdims=True) + EPS) * g
    return y * (1.0 + scale) + shift


def _proj_body(x_ref, g_ref, sh_ref, sc_ref, w_ref, wl_ref, o_ref, xn_ref, *, tm, n_main, rows_per_mod, fixed_row):
    j = pl.program_id(1)

    @pl.when(j == 0)
    def _():
        r = _mod_row(pl.program_id(0), tm, rows_per_mod, fixed_row)
        xn = _modulated_norm(x_ref[...], g_ref[...], sh_ref[pl.ds(r, 1), :], sc_ref[pl.ds(r, 1), :])
        xn_ref[...] = xn.astype(BF16)

    @pl.when(j < n_main)
    def _():
        o_ref[...] = _dot(xn_ref[...], w_ref[...])

    @pl.when(j >= n_main)
    def _():
        o_ref[...] = _dot(xn_ref[...], wl_ref[...])


def _proj(x2d, g, mod, w_in, w_lora, layer, rows_per_mod, fixed_row):
    n, d = x2d.shape
    dp, tn = w_in.shape[2], w_lora.shape[2]
    tm = min(512, n)
    n_main = dp // tn
    body = functools.partial(_proj_body, tm=tm, n_main=n_main, rows_per_mod=rows_per_mod, fixed_row=fixed_row)
    return pl.pallas_call(
        body,
        grid=(n // tm, n_main + 1),
        in_specs=[pl.BlockSpec((tm, d), lambda i, j: (i, 0)),
                  pl.BlockSpec((None, 1, d), lambda i, j: (layer, 0, 0)),
                  pl.BlockSpec((None, MOD_ROWS, d), lambda i, j: (layer, 0, 0)),
                  pl.BlockSpec((None, MOD_ROWS, d), lambda i, j: (layer, 0, 1)),
                  pl.BlockSpec((None, d, tn), lambda i, j: (layer, 0, jnp.minimum(j, n_main - 1))),
                  pl.BlockSpec((None, d, tn), lambda i, j: (layer, 0, 0))],
        out_specs=pl.BlockSpec((tm, tn), lambda i, j: (i, j)),
        out_shape=jax.ShapeDtypeStruct((n, dp + tn), F32),
        scratch_shapes=[pltpu.VMEM((tm, d), BF16)],
        compiler_params=_params("parallel", "arbitrary"),
    )(x2d, g, mod, mod, w_in, w_lora)


def _rope(x, cos, sin_signed):
    lane = _iota(x.shape, 1)
    swapped = jnp.where((lane % 64) < 32, pltpu.roll(x, 96, axis=1), pltpu.roll(x, 32, axis=1))
    return x * cos + swapped * sin_signed


def _softmax_pv(parts, sink_col, vall):
    m = sink_col
    for s in parts:
        m = jnp.maximum(m, jnp.max(s, axis=-1, keepdims=True))
    ps = [jnp.exp(s - m) for s in parts]
    denom = jnp.exp(sink_col - m)
    for p in ps:
        denom = denom + jnp.sum(p, axis=-1, keepdims=True)
    p = ps[0] if len(ps) == 1 else jnp.concatenate(ps, axis=1)
    return _dot(p.astype(BF16), vall) / denom


def _attn_body(q_ref, kp_ref, kc_ref, kn_ref, vp_ref, vc_ref, vn_ref, kx_ref, vx_ref,
               cp_ref, cc_ref, cn_ref, sp_ref, sc_ref, sn_ref, sink_ref, o_ref, *, nb, group):
    n = pl.program_id(1)
    blk = ATTN_BLOCK
    cos = (cp_ref[...], cc_ref[...], cn_ref[...])
    sin = (sp_ref[...], sc_ref[...], sn_ref[...])
    qi = _iota((group * blk, 3 * blk), 0) % blk
    kj = _iota((group * blk, 3 * blk), 1)
    in_seq = ((kj >= blk) | (n > 0)) & ((kj < 2 * blk) | (n < nb - 1))
    band_ok = (jnp.abs(kj - blk - qi) <= WINDOW) & in_seq
    scale = HEAD_DIM ** -0.5
    for h in range(KV_HEADS):
        hs = slice(h * HEAD_DIM, (h + 1) * HEAD_DIM)
        kb = [_rope(r[:, hs], c, s) for r, c, s in zip((kp_ref, kc_ref, kn_ref), cos, sin)]
        kall = jnp.concatenate(kb + [kx_ref[:, hs]], axis=0).astype(BF16)
        vall = jnp.concatenate([vp_ref[:, hs], vc_ref[:, hs], vn_ref[:, hs], vx_ref[:, hs]], axis=0).astype(BF16)
        qs = []
        for g in range(group):
            c0 = (h * group + g) * HEAD_DIM
            qs.append(_rope(q_ref[:, c0:c0 + HEAD_DIM], cos[1], sin[1]))
        qh = jnp.concatenate(qs, axis=0).astype(BF16)
        s = _dot_nt(qh, kall) * scale
        s_loc = jnp.where(band_ok, s[:, :3 * blk], MASKED)
        o = _softmax_pv([s_loc, s[:, 3 * blk:]], sink_ref[h], vall)
        for g in range(group):
            c0 = (h * group + g) * HEAD_DIM
            o_ref[:, c0:c0 + HEAD_DIM] = o[g * blk:(g + 1) * blk].astype(o_ref.dtype)


def _latent_attention(px, pc, cos_t, sin_t, sink_col, b, t, l, aq, akv):
    blk = ATTN_BLOCK
    nb = t // blk
    group = aq // akv
    kcol, vcol = aq // akv, aq // akv + 1
    prev = lambda n: jnp.maximum(n - 1, 0)
    nxt = lambda n: jnp.minimum(n + 1, nb - 1)
    kv_spec = lambda col, f: pl.BlockSpec((blk, akv), lambda bi, n: (bi * nb + f(n), col))
    tab_spec = lambda f: pl.BlockSpec((blk, HEAD_DIM), lambda bi, n: (f(n), 0))
    ident = lambda n: n
    body = functools.partial(_attn_body, nb=nb, group=group)
    return pl.pallas_call(
        body,
        grid=(b, nb),
        in_specs=[pl.BlockSpec((blk, aq), lambda bi, n: (bi * nb + n, 0)),
                  kv_spec(kcol, prev), kv_spec(kcol, ident), kv_spec(kcol, nxt),
                  kv_spec(vcol, prev), kv_spec(vcol, ident), kv_spec(vcol, nxt),
                  pl.BlockSpec((l, akv), lambda bi, n: (bi, kcol)),
                  pl.BlockSpec((l, akv), lambda bi, n: (bi, vcol)),
                  tab_spec(prev), tab_spec(ident), tab_spec(nxt),
                  tab_spec(prev), tab_spec(ident), tab_spec(nxt),
                  pl.BlockSpec((KV_HEADS, group * blk, 1), lambda bi, n: (0, 0, 0))],
        out_specs=pl.BlockSpec((blk, aq), lambda bi, n: (bi * nb + n, 0)),
        out_shape=jax.ShapeDtypeStruct((b * t, aq), BF16),
        compiler_params=_params("parallel", "parallel"),
    )(px, px, px, px, px, px, px, pc, pc, cos_t, cos_t, cos_t, sin_t, sin_t, sin_t, sink_col)


def _ctx_attn_body(q_ref, k_ref, v_ref, sink_ref, o_ref, *, group):
    scale = HEAD_DIM ** -0.5
    for h in range(KV_HEADS):
        hs = slice(h * HEAD_DIM, (h + 1) * HEAD_DIM)
        kall = k_ref[:, hs].astype(BF16)
        vall = v_ref[:, hs].astype(BF16)
        for g in range(group):
            c0 = (h * group + g) * HEAD_DIM
            s = _dot_nt(q_ref[:, c0:c0 + HEAD_DIM].astype(BF16), kall) * scale
            o = _softmax_pv([s], sink_ref[h * group + g], vall)
            o_ref[:, c0:c0 + HEAD_DIM] = o.astype(o_ref.dtype)


def _context_attention(pc, sink_rows, b, l, aq, akv):
    group = aq // akv
    kcol, vcol = aq // akv, aq // akv + 1
    return pl.pallas_call(
        functools.partial(_ctx_attn_body, group=group),
        grid=(b,),
        in_specs=[pl.BlockSpec((l, aq), lambda bi: (bi, 0)),
                  pl.BlockSpec((l, akv), lambda bi: (bi, kcol)),
                  pl.BlockSpec((l, akv), lambda bi: (bi, vcol)),
                  pl.BlockSpec((KV_HEADS * group, l, 1), lambda bi: (0, 0, 0))],
        out_specs=pl.BlockSpec((l, aq), lambda bi: (bi, 0)),
        out_shape=jax.ShapeDtypeStruct((b * l, aq), BF16),
        compiler_params=_params("parallel"),
    )(pc, pc, pc, sink_rows)


def _cmlp_body(u_ref, gv_ref, g_ref, ws_ref, bs_ref, o_ref, *, groups):
    u = jax.nn.gelu(u_ref[...])
    gv = jax.nn.gelu(gv_ref[...])
    gvn = gv * lax.rsqrt(jnp.mean(gv * gv, axis=-1, keepdims=True) + EPS) * g_ref[...]
    for gi in range(groups):
        cs = slice(gi * CMLP_CH, (gi + 1) * CMLP_CH)
        mixed = _dot(ws_ref[gi].astype(BF16), gvn[:, cs].astype(BF16)) + bs_ref[gi]
        o_ref[:, cs] = (u[:, cs] * mixed).astype(o_ref.dtype)


def _chunk_mlp(p, norm_g, ws, bs_b, ucol, cw):
    n = p.shape[0]
    groups = cw // CMLP_CH
    ch = CMLP_CHUNK
    return pl.pallas_call(
        functools.partial(_cmlp_body, groups=groups),
        grid=(n // ch,),
        in_specs=[pl.BlockSpec((ch, cw), lambda i: (i, ucol)),
                  pl.BlockSpec((ch, cw), lambda i: (i, ucol + 1)),
                  pl.BlockSpec((1, cw), lambda i: (0, 0)),
                  pl.BlockSpec((groups, ch, ch), lambda i: (0, 0, 0)),
                  pl.BlockSpec((groups, ch, CMLP_CH), lambda i: (0, 0, 0))],
        out_specs=pl.BlockSpec((ch, cw), lambda i: (i, 0)),
        out_shape=jax.ShapeDtypeStruct((n, cw), BF16),
        compiler_params=_params("parallel"),
    )(p, p, norm_g, ws, bs_b)


def _head_sum(x):
    ones = (_iota((128, 128), 0) // RWKV_HEAD == _iota((128, 128), 1) // RWKV_HEAD).astype(F32)
    cols = [_dot(x[:, s * 128:(s + 1) * 128], ones, precision=HIGHEST) for s in range(x.shape[1] // 128)]
    return cols[0] if len(cols) == 1 else jnp.concatenate(cols, axis=1)


def _prep_body(r_ref, k_ref, v_ref, rp_ref, kp_ref, vp_ref, rn_ref, kn_ref, vn_ref,
               cr_ref, ck_ref, cv_ref, hw_ref, ha_ref, w2_ref, a2_ref, w0_ref, a0_ref, kkp_ref, kap_ref,
               ro_ref, ko_ref, vo_ref, kko_ref, lw_ref, bo_ref, kr_ref, *, tr, seq, lora):
    i = pl.program_id(0)
    first = (i * tr) % seq == 0
    last = ((i + 1) * tr) % seq == 0
    row = _iota(r_ref.shape, 0)

    def conv(x_ref, xp_ref, xn_ref, w_ref):
        x = x_ref[...]
        before = jnp.where(first, 0.0, xp_ref[7:8, :])
        after = jnp.where(last, 0.0, xn_ref[0:1, :])
        xm = jnp.where(row == 0, before, pltpu.roll(x, 1, axis=0))
        xp = jnp.where(row == tr - 1, after, pltpu.roll(x, tr - 1, axis=0))
        return xm * w_ref[0:1, :] + x * w_ref[1:2, :] + xp * w_ref[2:3, :]

    r = conv(r_ref, rp_ref, rn_ref, cr_ref)
    k = conv(k_ref, kp_ref, kn_ref, ck_ref)
    v = conv(v_ref, vp_ref, vn_ref, cv_ref)
    kk = k * kkp_ref[...]
    kk = kk * lax.rsqrt(_head_sum(kk * kk) + 1e-12)
    ro_ref[...] = r
    ko_ref[...] = k
    vo_ref[...] = v
    kko_ref[...] = kk
    for z in range(2):
        zs = slice(z * lora, (z + 1) * lora)
        w_raw = w0_ref[z:z + 1, :] + _dot(jnp.tanh(hw_ref[:, zs]).astype(BF16), w2_ref[z].astype(BF16))
        softplus_neg = jnp.maximum(-w_raw, 0.0) + jnp.log1p(jnp.exp(-jnp.abs(w_raw)))
        lw_ref[z] = -jnp.exp(-softplus_neg - 0.5)
        a = jax.nn.sigmoid(a0_ref[z:z + 1, :] + _dot(ha_ref[:, zs].astype(BF16), a2_ref[z].astype(BF16)))
        kr_ref[z] = k * (1.0 + (a - 1.0) * kap_ref[...])
        bo_ref[z] = kk * a


def _rwkv_prep(p, conv_w, w2, a2, w0, a0, kk_p, ka_p, seq, rcol, hcol, rw, lora):
    n = p.shape[0]
    tr = min(256, seq)
    nh = n // 8
    body = functools.partial(_prep_body, tr=tr, seq=seq, lora=lora)
    cur = lambda c: pl.BlockSpec((tr, rw), lambda i: (i, rcol + c))
    prv = lambda c: pl.BlockSpec((8, rw), lambda i: (jnp.maximum(i * (tr // 8) - 1, 0), rcol + c))
    nxt = lambda c: pl.BlockSpec((8, rw), lambda i: (jnp.minimum((i + 1) * (tr // 8), nh - 1), rcol + c))
    cw = lambda c: pl.BlockSpec((3, rw), lambda i: (0, c))
    full2 = lambda shape: pl.BlockSpec(shape, lambda i: (0,) * len(shape))
    shared = pl.BlockSpec((tr, rw), lambda i: (i, 0))
    directed = pl.BlockSpec((2, tr, rw), lambda i: (0, i, 0))
    return pl.pallas_call(
        body,
        grid=(n // tr,),
        in_specs=[cur(0), cur(1), cur(2), prv(0), prv(1), prv(2), nxt(0), nxt(1), nxt(2),
                  cw(0), cw(1), cw(2),
                  pl.BlockSpec((tr, 2 * lora), lambda i: (i, hcol)),
                  pl.BlockSpec((tr, 2 * lora), lambda i: (i, hcol + 1)),
                  full2((2, lora, rw)), full2((2, lora, rw)), full2((2, rw)), full2((2, rw)),
                  full2((1, rw)), full2((1, rw))],
        out_specs=[shared, shared, shared, shared, directed, directed, directed],
        out_shape=[jax.ShapeDtypeStruct((n, rw), F32)] * 4 + [jax.ShapeDtypeStruct((2, n, rw), F32)] * 3,
        compiler_params=_params("parallel"),
    )(p, p, p, p, p, p, p, p, p, conv_w, conv_w, conv_w, p, p, w2, a2, w0, a0, kk_p, ka_p)


def _mm(a, b, passes, kind="nn"):
    fn = {"nn": _dot, "nt": _dot_nt, "tn": _dot_tn}[kind]
    if passes == 6:
        return fn(a, b, precision=HIGHEST)
    a_hi, b_hi = a.astype(BF16), b.astype(BF16)
    if passes == 1:
        return fn(a_hi, b_hi)
    a_lo = (a - a_hi.astype(F32)).astype(BF16)
    b_lo = (b - b_hi.astype(F32)).astype(BF16)
    return fn(a_hi, b_hi) + (fn(a_lo, b_hi) + fn(a_hi, b_lo))


def _unit_tri_inverse(nmats, eye, same16, same32, passes):
    n16 = [jnp.where(same16, n, 0.0) for n in nmats]
    xs = [eye - n for n in n16]
    pw = n16
    for _ in range(3):
        pw = [_mm(p, p, passes) for p in pw]
        xs = [x + _mm(x, p, passes) for x, p in zip(xs, pw)]
    for mask in (same32 & ~same16, ~same32):
        offs = [jnp.where(mask, n, 0.0) for n in nmats]
        xo = [_mm(x, o, passes) for x, o in zip(xs, offs)]
        xs = [x - _mm(t, x, passes) for x, t in zip(xs, xo)]
    return xs


def _scan_body(r_ref, kk_ref, v_ref, lw_ref, b_ref, k_ref, s0_ref, y_ref, sf_ref, st_ref, *, heads, nchunks):
    z = pl.program_id(0)
    c = pl.program_id(3)
    C, K = SCAN_CHUNK, RWKV_HEAD
    pp = SCAN_PASSES

    @pl.when(c == 0)
    def _():
        st_ref[...] = s0_ref[...]

    ti, si = _iota((C, C), 0), _iota((C, C), 1)
    before = (si - ti) * (1 - 2 * z) < 0
    upto = before | (si == ti)
    eye = (si == ti).astype(F32)
    same16 = (ti // 16) == (si // 16)
    same32 = (ti // 32) == (si // 32)

    lw = lw_ref[...]
    lc = _dot(upto.astype(F32), lw, precision=HIGHEST)
    ltot = jnp.sum(lw, axis=0, keepdims=True)
    e_neg = jnp.exp(-lc)
    e_out = jnp.exp(ltot - lc)
    kkt = kk_ref[...] * jnp.exp(lc - lw)
    rt = r_ref[...] * jnp.exp(lc)
    bt = b_ref[...] * e_neg
    kt = k_ref[...] * e_neg
    bh = b_ref[...] * e_out
    kh = k_ref[...] * e_out
    etot = jnp.exp(ltot)
    v = v_ref[...]

    hr = range(heads)
    ls = [slice(h * K, (h + 1) * K) for h in hr]
    ps = [_mm(jnp.concatenate([kkt[:, s], rt[:, s]], axis=0),
              jnp.concatenate([bt[:, s], kt[:, s]], axis=0), pp["pair"], "nt") for s in ls]
    nmats = [jnp.where(before, p[:C, :C], 0.0) for p in ps]
    pkk = [jnp.where(before, p[:C, C:], 0.0) for p in ps]
    prb = [jnp.where(upto, p[C:, :C], 0.0) for p in ps]
    prk = [jnp.where(upto, p[C:, C:], 0.0) for p in ps]
    tinv = _unit_tri_inverse(nmats, eye, same16, same32, pp["inv"])
    tg = [_mm(tinv[h], jnp.concatenate([kkt[:, ls[h]], pkk[h]], axis=1), pp["solve"]) for h in hr]
    qa = [jnp.concatenate([rt[:, ls[h]], prk[h]], axis=1) - _mm(prb[h], tg[h], pp["solve"]) for h in hr]
    m3 = [_mm(bh[:, ls[h]], tg[h], pp["solve"], "tn") for h in hr]
    sv = [jnp.concatenate([st_ref[h], v[:, ls[h]]], axis=0) for h in hr]
    ys = [_mm(qa[h], sv[h], pp["state"]) for h in hr]
    for h in hr:
        decay_diag = eye * jnp.broadcast_to(etot[:, ls[h]], (K, K))
        trans = jnp.concatenate([decay_diag, jnp.zeros((K, C), F32)], axis=1) - m3[h]
        st_ref[h] = _mm(trans, sv[h], pp["state"]) + _mm(kh[:, ls[h]], v[:, ls[h]], pp["state"], "tn")
    y_ref[...] = ys[0] if heads == 1 else jnp.concatenate(ys, axis=1)

    @pl.when(c == nchunks - 1)
    def _():
        sf_ref[...] = st_ref[...]


def _rwkv_scan(r, kk, v, lw, bb, kr, s0, b, seq, rw):
    C, K = SCAN_CHUNK, RWKV_HEAD
    nchunks = seq // C
    heads = min(SCAN_HEADS, rw // K)
    ngroups = rw // (heads * K)
    n = b * seq
    row = lambda z, bi, hg, c: bi * nchunks + c + z * (nchunks - 1 - 2 * c)
    shared = pl.BlockSpec((C, heads * K), lambda z, bi, hg, c: (row(z, bi, hg, c), hg))
    directed = pl.BlockSpec((None, C, heads * K), lambda z, bi, hg, c: (z, row(z, bi, hg, c), hg))
    state = pl.BlockSpec((None, None, heads, K, K), lambda z, bi, hg, c: (z, bi, hg, 0, 0))
    body = functools.partial(_scan_body, heads=heads, nchunks=nchunks)
    return pl.pallas_call(
        body,
        grid=(2, b, ngroups, nchunks),
        in_specs=[shared, shared, shared, directed, directed, directed, state],
        out_specs=[directed, state],
        out_shape=[jax.ShapeDtypeStruct((2, n, rw), F32),
                   jax.ShapeDtypeStruct((2, b, rw // K, K, K), F32)],
        scratch_shapes=[pltpu.VMEM((heads, K, K), F32)],
        compiler_params=_params("parallel", "parallel", "parallel", "arbitrary"),
    )(r, kk, v, lw, bb, kr, s0)


def _rwkv_out_body(yf_ref, yb_ref, r_ref, k_ref, v_ref, g_ref, rk_ref, lnw_ref, lnb_ref, o_ref):
    y = yf_ref[...] + yb_ref[...]
    inv = 1.0 / RWKV_HEAD
    mu = _head_sum(y) * inv
    d = y - mu
    var = _head_sum(d * d) * inv
    yn = d * lax.rsqrt(var + RWKV_GN_EPS) * lnw_ref[...] + lnb_ref[...]
    bonus = _head_sum(r_ref[...] * k_ref[...] * rk_ref[...]) * v_ref[...]
    o_ref[...] = ((yn + bonus) * jax.nn.sigmoid(g_ref[...])).astype(o_ref.dtype)


def _rwkv_output(y, r, k, v, p, gcol, rk, ln_w, ln_b):
    n, rw = r.shape
    tr = min(256, n)
    shared = pl.BlockSpec((tr, rw), lambda i: (i, 0))
    vec = pl.BlockSpec((1, rw), lambda i: (0, 0))
    return pl.pallas_call(
        _rwkv_out_body,
        grid=(n // tr,),
        in_specs=[pl.BlockSpec((None, tr, rw), lambda i: (0, i, 0)),
                  pl.BlockSpec((None, tr, rw), lambda i: (1, i, 0)),
                  shared, shared, shared,
                  pl.BlockSpec((tr, rw), lambda i: (i, gcol)),
                  vec, vec, vec],
        out_specs=shared,
        out_shape=jax.ShapeDtypeStruct((n, rw), BF16),
        compiler_params=_params("parallel"),
    )(y, y, r, k, v, p, rk, ln_w, ln_b)


def _wout_body(a_ref, c_ref, r_ref, wa_ref, wc_ref, wr_ref, x_ref, g_ref, o_ref, *, tm, rows_per_mod, fixed_row):
    row = _mod_row(pl.program_id(0), tm, rows_per_mod, fixed_row)
    acc = _dot(a_ref[...], wa_ref[...]) + _dot(c_ref[...], wc_ref[...]) + _dot(r_ref[...], wr_ref[...])
    o_ref[...] = x_ref[...] + g_ref[pl.ds(row, 1), :] * acc


def _out_proj(attn, cmlp, rwkv, w_out, x2d, mod, layer, rows_per_mod, fixed_row):
    n, d = x2d.shape
    aq, cw, rw = attn.shape[1], cmlp.shape[1], rwkv.shape[1]
    tm = min(512, n)
    tn = min(1024, d)
    body = functools.partial(_wout_body, tm=tm, rows_per_mod=rows_per_mod, fixed_row=fixed_row)
    return pl.pallas_call(
        body,
        grid=(n // tm, d // tn),
        in_specs=[pl.BlockSpec((tm, aq), lambda i, j: (i, 0)),
                  pl.BlockSpec((tm, cw), lambda i, j: (i, 0)),
                  pl.BlockSpec((tm, rw), lambda i, j: (i, 0)),
                  pl.BlockSpec((None, aq, tn), lambda i, j: (layer, 0, j)),
                  pl.BlockSpec((None, cw, tn), lambda i, j: (layer, aq // cw, j)),
                  pl.BlockSpec((None, rw, tn), lambda i, j: (layer, (aq + cw) // rw, j)),
                  pl.BlockSpec((tm, tn), lambda i, j: (i, j)),
                  pl.BlockSpec((None, MOD_ROWS, tn), lambda i, j: (layer, 0, 2 * (d // tn) + j))],
        out_specs=pl.BlockSpec((tm, tn), lambda i, j: (i, j)),
        out_shape=jax.ShapeDtypeStruct((n, d), F32),
        compiler_params=_params("parallel", "parallel"),
    )(attn, cmlp, rwkv, w_out, w_out, w_out, x2d, mod)


def _router_body(x_ref, g_ref, sh_ref, sc_ref, rw_ref, rb_ref, zn_ref, ids_ref, wt_ref, *,
                 tm, rows_per_mod, fixed_row, experts):
    row = _mod_row(pl.program_id(0), tm, rows_per_mod, fixed_row)
    zn = _modulated_norm(x_ref[...], g_ref[...], sh_ref[pl.ds(row, 1), :], sc_ref[pl.ds(row, 1), :])
    zn_ref[...] = zn
    logits = _dot(zn, rw_ref[...], precision=HIGHEST).T
    per_group = experts // N_EXPERT_GROUPS
    scores = [jax.nn.sigmoid(logits[e:e + 1, :]) for e in range(experts)]
    sel = [scores[e] + rb_ref[e:e + 1, :] for e in range(experts)]
    best_val, best_grp = None, None
    for gi in range(N_EXPERT_GROUPS):
        mem = sel[gi * per_group:(gi + 1) * per_group]
        top2 = None
        for a in range(per_group):
            for b2 in range(a + 1, per_group):
                pair = mem[a] + mem[b2]
                top2 = pair if top2 is None else jnp.maximum(top2, pair)
        if gi == 0:
            best_val, best_grp = top2, jnp.zeros(top2.shape, jnp.int32)
        else:
            better = top2 > best_val
            best_grp = jnp.where(better, gi, best_grp)
            best_val = jnp.where(better, top2, best_val)
    chosen, picked = [], []
    for e in range(experts):
        gi = e // per_group
        rank = jnp.zeros(best_grp.shape, jnp.int32)
        for j in range(gi * per_group, (gi + 1) * per_group):
            if j != e:
                ahead = (sel[j] > sel[e]) | ((sel[j] == sel[e]) & (j < e))
                rank = rank + ahead.astype(jnp.int32)
        chosen.append((best_grp == gi) & (rank < TOP_K))
        picked.append(jnp.where(chosen[e], scores[e], 0.0))
    total = picked[0]
    for e in range(1, experts):
        total = total + picked[e]
    zero_i, zero_f = jnp.zeros(total.shape, jnp.int32), jnp.zeros(total.shape, F32)
    seen, ids, wts = zero_i, [zero_i, zero_i], [zero_f, zero_f]
    for e in range(experts):
        gate = picked[e] / total
        for slot in range(TOP_K):
            here = chosen[e] & (seen == slot)
            ids[slot] = jnp.where(here, e, ids[slot])
            wts[slot] = jnp.where(here, gate, wts[slot])
        seen = seen + chosen[e].astype(jnp.int32)
    ids_ref[...] = jnp.concatenate(ids + [jnp.zeros((8 - TOP_K, tm), jnp.int32)], axis=0)
    wt_ref[...] = jnp.concatenate(wts + [jnp.zeros((128 - TOP_K, tm), F32)], axis=0).T


def _router(x2d, g, mod, router_w_pad, router_b_col, layer, rows_per_mod, fixed_row, experts):
    n, d = x2d.shape
    tm = min(256, n)
    body = functools.partial(_router_body, tm=tm, rows_per_mod=rows_per_mod, fixed_row=fixed_row, experts=experts)
    return pl.pallas_call(
        body,
        grid=(n // tm,),
        in_specs=[pl.BlockSpec((tm, d), lambda i: (i, 0)),
                  pl.BlockSpec((None, 1, d), lambda i: (layer, 0, 0)),
                  pl.BlockSpec((None, MOD_ROWS, d), lambda i: (layer, 0, 3)),
                  pl.BlockSpec((None, MOD_ROWS, d), lambda i: (layer, 0, 4)),
                  pl.BlockSpec((d, 128), lambda i: (0, 0)),
                  pl.BlockSpec((128, 1), lambda i: (0, 0))],
        out_specs=[pl.BlockSpec((tm, d), lambda i: (i, 0)),
                   pl.BlockSpec((8, tm), lambda i: (0, i)),
                   pl.BlockSpec((tm, 128), lambda i: (i, 0))],
        out_shape=[jax.ShapeDtypeStruct((n, d), F32), jax.ShapeDtypeStruct((8, n), jnp.int32),
                   jax.ShapeDtypeStruct((n, 128), F32)],
        compiler_params=_params("parallel"),
    )(x2d, g, mod, mod, router_w_pad, router_b_col)


def _route_plan(ids, experts, tm):
    n = ids.shape[1]
    total = TOP_K * n + experts * tm
    flat = ids.reshape(-1)
    onehot = (flat[:, None] == jnp.arange(experts, dtype=jnp.int32)[None, :]).astype(jnp.int32)
    rank = jnp.cumsum(onehot, axis=0) - onehot
    padded = ((jnp.sum(onehot, axis=0) + tm - 1) // tm) * tm
    ends = jnp.cumsum(padded)
    pos = (ends - padded)[flat] + jnp.sum(rank * onehot, axis=1)
    row_token = jnp.zeros((total,), jnp.int32).at[pos].set(jnp.tile(jnp.arange(n, dtype=jnp.int32), TOP_K))
    tile_start = jnp.arange(total // tm, dtype=jnp.int32) * tm
    tile_used = (tile_start < ends[-1]).astype(jnp.int32)
    tile_expert = jnp.minimum(jnp.searchsorted(ends, tile_start, side="right"), experts - 1).astype(jnp.int32)
    last_used = tile_expert[jnp.maximum(ends[-1] // tm - 1, 0)]
    tile_expert = jnp.where(tile_used == 1, tile_expert, last_used)
    return pos.astype(jnp.int32), row_token, tile_expert, tile_used


def _row_copy(src_hbm, row, dst_ref, i, sem):
    return pltpu.make_async_copy(src_hbm.at[pl.ds(row, 1), :], dst_ref.at[pl.ds(i, 1), :], sem)


def _start_rows(src_hbm, dst_ref, sem, index_of):
    def start(i, carry):
        _row_copy(src_hbm, index_of(i), dst_ref, i, sem).start()
        return carry

    lax.fori_loop(0, dst_ref.shape[0], start, 0, unroll=8)


def _wait_rows(src_hbm, dst_ref, sem):
    pltpu.make_async_copy(src_hbm.at[pl.ds(0, dst_ref.shape[0]), :], dst_ref, sem).wait()


def _expert_hidden_body(te_ref, used_ref, tok_ref, zn_hbm, w1_ref, w3_ref, h_ref, xg_ref, sem, *, tm, ntiles):
    p, f = pl.program_id(0), pl.program_id(1)
    used = used_ref[p] == 1
    slot = p % 2
    rows_of = lambda tile: (lambda i: tok_ref[tile * tm + i])

    @pl.when(used & (f == 0) & (p == 0))
    def _():
        _start_rows(zn_hbm, xg_ref.at[0], sem.at[0], rows_of(0))

    @pl.when(used & (f == 0))
    def _():
        _wait_rows(zn_hbm, xg_ref.at[slot], sem.at[slot])

        @pl.when((p + 1 < ntiles) & (used_ref[jnp.minimum(p + 1, ntiles - 1)] == 1))
        def _():
            _start_rows(zn_hbm, xg_ref.at[1 - slot], sem.at[1 - slot], rows_of(p + 1))

    @pl.when(used)
    def _():
        x = xg_ref[slot].astype(BF16)
        h1 = _dot(x, w1_ref[...].astype(BF16))
        h3 = _dot(x, w3_ref[...].astype(BF16))
        h_ref[...] = ((h1 * jax.nn.sigmoid(h1)) * h3).astype(h_ref.dtype)

    @pl.when(jnp.logical_not(used))
    def _():
        h_ref[...] = jnp.zeros_like(h_ref)


def _expert_out_body(te_ref, used_ref, h_ref, w2_ref, o_ref):
    o_ref[...] = _dot(h_ref[...], w2_ref[...].astype(BF16))


def _experts(zn, row_token, tile_expert, tile_used, w1, w3, w2, layer, tm):
    n, d = zn.shape
    ff = w1.shape[3]
    total = row_token.shape[0]
    ntiles = total // tm
    tf = min(256, ff)
    n_f = ff // tf
    tn = min(2048, d)
    n_j = d // tn
    hold = lambda used, p, j, last: jnp.where(used[p] == 1, j, last)
    hidden = pl.pallas_call(
        functools.partial(_expert_hidden_body, tm=tm, ntiles=ntiles),
        grid_spec=pltpu.PrefetchScalarGridSpec(
            num_scalar_prefetch=3,
            grid=(ntiles, n_f),
            in_specs=[pl.BlockSpec(memory_space=pl.ANY),
                      pl.BlockSpec((None, None, d, tf),
                                   lambda p, f, te, us, tk: (layer, te[p], 0, hold(us, p, f, n_f - 1))),
                      pl.BlockSpec((None, None, d, tf),
                                   lambda p, f, te, us, tk: (layer, te[p], 0, hold(us, p, f, n_f - 1)))],
            out_specs=pl.BlockSpec((tm, tf), lambda p, f, te, us, tk: (p, f)),
            scratch_shapes=[pltpu.VMEM((2, tm, d), F32), pltpu.SemaphoreType.DMA((2,))]),
        out_shape=jax.ShapeDtypeStruct((total, ff), BF16),
        compiler_params=_params("arbitrary", "arbitrary", disable_bounds_checks=True),
    )(tile_expert, tile_used, row_token, zn, w1, w3)
    return pl.pallas_call(
        _expert_out_body,
        grid_spec=pltpu.PrefetchScalarGridSpec(
            num_scalar_prefetch=2,
            grid=(n_j, ntiles),
            in_specs=[pl.BlockSpec((tm, ff), lambda j, p, te, us: (p, 0)),
                      pl.BlockSpec((None, None, ff, tn), lambda j, p, te, us: (layer, te[p], 0, j))],
            out_specs=pl.BlockSpec((tm, tn), lambda j, p, te, us: (p, j))),
        out_shape=jax.ShapeDtypeStruct((total, d), F32),
        compiler_params=_params("parallel", "parallel"),
    )(tile_expert, tile_used, hidden, w2)


def _combine_body(pos_ref, ys_hbm, wt_ref, x_ref, g2_ref, fg_ref, o_ref, ya_ref, yb_ref, sem_a, sem_b, *,
                  tm, n, rows_per_mod, fixed_row, final_norm):
    i = pl.program_id(0)
    slot = i % 2

    def start(tile, s):
        _start_rows(ys_hbm, ya_ref.at[s], sem_a.at[s], lambda r: pos_ref[tile * tm + r])
        _start_rows(ys_hbm, yb_ref.at[s], sem_b.at[s], lambda r: pos_ref[n + tile * tm + r])

    @pl.when(i == 0)
    def _():
        start(0, 0)

    @pl.when(i + 1 < n // tm)
    def _():
        start(i + 1, 1 - slot)

    _wait_rows(ys_hbm, ya_ref.at[slot], sem_a.at[slot])
    _wait_rows(ys_hbm, yb_ref.at[slot], sem_b.at[slot])
    row = _mod_row(i, tm, rows_per_mod, fixed_row)
    wt = wt_ref[...]
    mix = wt[:, 0:1] * ya_ref[slot] + wt[:, 1:2] * yb_ref[slot]
    out = x_ref[...] + g2_ref[pl.ds(row, 1), :] * mix
    if final_norm:
        out = out * lax.rsqrt(jnp.mean(out * out, axis=-1, keepdims=True) + EPS) * fg_ref[...]
    o_ref[...] = out


def _moe_combine(ys, pos, wts, x2d, mod, final_g, layer, rows_per_mod, fixed_row, final_norm):
    n, d = x2d.shape
    tm = min(256, n)
    body = functools.partial(_combine_body, tm=tm, n=n, rows_per_mod=rows_per_mod, fixed_row=fixed_row,
                             final_norm=final_norm)
    return pl.pallas_call(
        body,
        grid_spec=pltpu.PrefetchScalarGridSpec(
            num_scalar_prefetch=1,
            grid=(n // tm,),
            in_specs=[pl.BlockSpec(memory_space=pl.ANY),
                      pl.BlockSpec((tm, 128), lambda i, ps: (i, 0)),
                      pl.BlockSpec((tm, d), lambda i, ps: (i, 0)),
                      pl.BlockSpec((None, MOD_ROWS, d), lambda i, ps: (layer, 0, 5)),
                      pl.BlockSpec((1, d), lambda i, ps: (0, 0))],
            out_specs=pl.BlockSpec((tm, d), lambda i, ps: (i, 0)),
            scratch_shapes=[pltpu.VMEM((2, tm, d), F32), pltpu.VMEM((2, tm, d), F32),
                            pltpu.SemaphoreType.DMA((2,)), pltpu.SemaphoreType.DMA((2,))]),
        out_shape=jax.ShapeDtypeStruct((n, d), F32),
        compiler_params=_params("arbitrary", disable_bounds_checks=True),
    )(pos, ys, wts, x2d, mod, final_g)


def _moe(x2d, g, mod, router_w_pad, router_b_col, w1, w3, w2, final_g, layer, experts,
         rows_per_mod, fixed_row, final_norm=False):
    n = x2d.shape[0]
    tm = 512 if n >= 8 * 512 else 128
    zn, ids, wts = _router(x2d, g, mod, router_w_pad, router_b_col, layer, rows_per_mod, fixed_row, experts)
    pos, row_token, tile_expert, tile_used = _route_plan(ids[:TOP_K], experts, tm)
    ys = _experts(zn, row_token, tile_expert, tile_used, w1, w3, w2, layer, tm)
    return _moe_combine(ys, pos, wts, x2d, mod, final_g, layer, rows_per_mod, fixed_row, final_norm)


def _rope_tables(t):
    pos = jnp.arange(t)
    half = HEAD_DIM // 4
    freqs = ROPE_BASE ** (-jnp.arange(half, dtype=F32) / half)
    ang_r = (pos // GRID_W).astype(F32)[:, None] * freqs[None, :]
    ang_c = (pos % GRID_W).astype(F32)[:, None] * freqs[None, :]
    cos_t = jnp.concatenate([jnp.cos(ang_r)] * 2 + [jnp.cos(ang_c)] * 2, axis=1)
    sin_t = jnp.concatenate([-jnp.sin(ang_r), jnp.sin(ang_r), -jnp.sin(ang_c), jnp.sin(ang_c)], axis=1)
    return cos_t, sin_t


def kernel(x, c, ctx, c_ctx, ada_w, ada_b, norm1_g, w_in, rwkv_conv, attn_sink, cmlp_norm_g, cmlp_ws, cmlp_b,
           rwkv_w0, rwkv_w1, rwkv_w2, rwkv_a0, rwkv_a1, rwkv_a2, rwkv_kk, rwkv_ka, rwkv_rk, rwkv_ln_w, rwkv_ln_b,
           w_out, norm2_g, router_w, router_b, moe_w1, moe_w3, moe_w2, final_g):
    b, t, d = x.shape
    l = ctx.shape[1]
    depth = ada_w.shape[0]
    cw = cmlp_norm_g.shape[1]
    rw = rwkv_w0.shape[2]
    lora = rwkv_w1.shape[3]
    experts = router_w.shape[1]
    dp = w_in.shape[2]
    akv = KV_HEADS * HEAD_DIM
    aq = dp - 2 * akv - 2 * cw - 4 * rw
    group = aq // akv
    heads = rw // RWKV_HEAD
    ucol = (aq + 2 * akv) // cw
    rcol = (aq + 2 * akv + 2 * cw) // rw
    gcol = rcol + 3
    hcol = dp // (2 * lora)
    assert (aq + 2 * akv) % cw == 0 and (aq + 2 * akv + 2 * cw) % rw == 0 and dp % (2 * lora) == 0
    assert b + 1 <= MOD_ROWS and (b * t) % l == 0 and t % 256 == 0 and l % 128 == 0

    cpad = jnp.zeros((MOD_ROWS, d), F32).at[:b].set(c).at[b].set(c_ctx)
    mod = _ada(cpad, ada_w, ada_b)
    cos_t, sin_t = _rope_tables(t)
    router_w_pad = jnp.zeros((d, 128), F32).at[:, :experts].set(router_w)
    router_b_col = jnp.zeros((128, 1), F32).at[:experts, 0].set(router_b)
    w_out_bf = w_out.astype(BF16)
    w_in_bf = w_in.astype(BF16)
    w_lora_bf = jnp.concatenate([rwkv_w1[:, 0], rwkv_w1[:, 1], rwkv_a1[:, 0], rwkv_a1[:, 1]], axis=2).astype(BF16)
    assert dp % w_lora_bf.shape[2] == 0
    s_zero = jnp.zeros((2, b, heads, RWKV_HEAD, RWKV_HEAD), F32)

    xs = x.reshape(b * t, d)
    hs = ctx.reshape(b * l, d)
    for layer in range(depth):
        lat = dict(rows_per_mod=t, fixed_row=None)
        con = dict(rows_per_mod=None, fixed_row=b)
        last = layer == depth - 1
        px = _proj(xs, norm1_g.reshape(depth, 1, d), mod, w_in_bf, w_lora_bf, layer, **lat)
        pc = _proj(hs, norm1_g.reshape(depth, 1, d), mod, w_in_bf, w_lora_bf, layer, **con)

        sink = attn_sink[layer]
        sink_col = jnp.repeat(sink.reshape(KV_HEADS, group), ATTN_BLOCK, axis=1)[..., None]
        attn_x = _latent_attention(px, pc, cos_t, sin_t, sink_col, b, t, l, aq, akv)

        bs_b = jnp.broadcast_to(cmlp_b[layer][:, :, None], cmlp_b.shape[1:] + (CMLP_CH,))
        cmlp_x = _chunk_mlp(px, cmlp_norm_g[layer][None], cmlp_ws[layer], bs_b, ucol, cw)

        prep_args = (rwkv_conv[layer], rwkv_w2[layer], rwkv_a2[layer], rwkv_w0[layer], rwkv_a0[layer],
                     rwkv_kk[layer][None], rwkv_ka[layer][None])
        rc, kc, vc, kkc, lwc, bbc, krc = _rwkv_prep(pc, *prep_args, l, rcol, hcol, rw, lora)
        rx, kx, vx, kkx, lwx, bbx, krx = _rwkv_prep(px, *prep_args, t, rcol, hcol, rw, lora)
        y_c, s_ctx = _rwkv_scan(rc, kkc, vc, lwc, bbc, krc, s_zero, b, l, rw)
        y_x, _ = _rwkv_scan(rx, kkx, vx, lwx, bbx, krx, s_ctx, b, t, rw)
        out_args = (rwkv_rk[layer][None], rwkv_ln_w[layer][None], rwkv_ln_b[layer][None])
        rwkv_x = _rwkv_output(y_x, rx, kx, vx, px, gcol, *out_args)

        xs = _out_proj(attn_x, cmlp_x, rwkv_x, w_out_bf, xs, mod, layer, **lat)
        moe_args = (norm2_g.reshape(depth, 1, d), mod, router_w_pad, router_b_col, moe_w1, moe_w3, moe_w2,
                    final_g[None], layer, experts)
        xs = _moe(xs, *moe_args, final_norm=last, **lat)

        if not last:
            sink_rows = jnp.broadcast_to(sink[:, None, None], (KV_HEADS * group, l, 1))
            attn_c = _context_attention(pc, sink_rows, b, l, aq, akv)
            cmlp_c = _chunk_mlp(pc, cmlp_norm_g[layer][None], cmlp_ws[layer], bs_b, ucol, cw)
            rwkv_c = _rwkv_output(y_c, rc, kc, vc, pc, gcol, *out_args)
            hs = _out_proj(attn_c, cmlp_c, rwkv_c, w_out_bf, hs, mod, layer, **con)
            hs = _moe(hs, *moe_args, **con)
    return xs.reshape(b, t, d)
```

```python
import functools

import jax
import jax.numpy as jnp
from jax import lax
from jax.experimental import pallas as pl
from jax.experimental.pallas import tpu as pltpu

F32, BF16 = jnp.float32, jnp.bfloat16
HIGHEST = lax.Precision.HIGHEST

HEAD_DIM = 128
KV_HEADS = 4
WINDOW = 128
ATTN_BLOCK = 128
GRID_W = 64
ROPE_BASE = 10000.0
CMLP_CH = 128
CMLP_CHUNK = 128
RWKV_HEAD = 64
RWKV_GN_EPS = 64e-5
N_EXPERT_GROUPS = 4
TOP_K = 2
N_MOD = 6
EPS = 1e-6
MASKED = -1e30

MOD_ROWS = 8
SCAN_CHUNK = 64
SCAN_HEADS = 16
SCAN_PASSES = {"pair": 1, "inv": 1, "solve": 1, "state": 1}
GATHER_PRIORITY = (1,)
COMBINE_PRIORITY = (0, 1)
VMEM_LIMIT_BYTES = 56 * 1024 * 1024


def _params(*sem, **kw):
    return pltpu.CompilerParams(dimension_semantics=sem, vmem_limit_bytes=VMEM_LIMIT_BYTES, **kw)


def _dot(a, b, **kw):
    return jnp.dot(a, b, preferred_element_type=F32, **kw)


def _dot_nt(a, b, **kw):
    return lax.dot_general(a, b, (((1,), (1,)), ((), ())), preferred_element_type=F32, **kw)


def _dot_tn(a, b, **kw):
    return lax.dot_general(a, b, (((0,), (0,)), ((), ())), preferred_element_type=F32, **kw)


def _iota(shape, dim):
    return lax.broadcasted_iota(jnp.int32, shape, dim)


def _ada_body(c_ref, w_ref, b_ref, o_ref):
    c = c_ref[...]
    a = (c * jax.nn.sigmoid(c)).astype(BF16)
    o_ref[...] = _dot(a, w_ref[...].astype(BF16)) + b_ref[...]


def _ada(cpad, ada_w, ada_b):
    depth, d, n = ada_w.shape
    tn = 512
    return pl.pallas_call(
        _ada_body,
        grid=(depth, n // tn),
        in_specs=[pl.BlockSpec((MOD_ROWS, d), lambda l, j: (0, 0)),
                  pl.BlockSpec((None, d, tn), lambda l, j: (l, 0, j)),
                  pl.BlockSpec((None, 1, tn), lambda l, j: (l, 0, j))],
        out_specs=pl.BlockSpec((None, MOD_ROWS, tn), lambda l, j: (l, 0, j)),
        out_shape=jax.ShapeDtypeStruct((depth, MOD_ROWS, n), F32),
        compiler_params=_params("parallel", "parallel"),
    )(cpad, ada_w, ada_b.reshape(depth, 1, n))


def _mod_row(i, tm, rows_per_mod, fixed_row):
    return fixed_row if rows_per_mod is None else (i * tm) // rows_per_mod


def _modulated_norm(x, g, shift, scale):
    y = x * lax.rsqrt(jnp.mean(x * x, axis=-1, keepdims=True) + EPS) * g
    return y * (1.0 + scale) + shift


def _proj_body(x_ref, g_ref, sh_ref, sc_ref, w_ref, wl_ref, o_ref, xn_ref, *, tm, n_main, rows_per_mod, fixed_row):
    j = pl.program_id(1)

    @pl.when(j == 0)
    def _():
        r = _mod_row(pl.program_id(0), tm, rows_per_mod, fixed_row)
        xn = _modulated_norm(x_ref[...], g_ref[...], sh_ref[pl.ds(r, 1), :], sc_ref[pl.ds(r, 1), :])
        xn_ref[...] = xn.astype(BF16)

    @pl.when(j < n_main)
    def _():
        o_ref[...] = _dot(xn_ref[...], w_ref[...])

    @pl.when(j >= n_main)
    def _():
        o_ref[...] = _dot(xn_ref[...], wl_ref[...])


def _proj(x2d, g, mod, w_in, w_lora, layer, rows_per_mod, fixed_row):
    n, d = x2d.shape
    dp, tn = w_in.shape[2], w_lora.shape[2]
    tm = min(512, n)
    n_main = dp // tn
    body = functools.partial(_proj_body, tm=tm, n_main=n_main, rows_per_mod=rows_per_mod, fixed_row=fixed_row)
    return pl.pallas_call(
        body,
        grid=(n // tm, n_main + 1),
        in_specs=[pl.BlockSpec((tm, d), lambda i, j: (i, 0)),
                  pl.BlockSpec((None, 1, d), lambda i, j: (layer, 0, 0)),
                  pl.BlockSpec((None, MOD_ROWS, d), lambda i, j: (layer, 0, 0)),
                  pl.BlockSpec((None, MOD_ROWS, d), lambda i, j: (layer, 0, 1)),
                  pl.BlockSpec((None, d, tn), lambda i, j: (layer, 0, jnp.minimum(j, n_main - 1))),
                  pl.BlockSpec((None, d, tn), lambda i, j: (layer, 0, 0))],
        out_specs=pl.BlockSpec((tm, tn), lambda i, j: (i, j)),
        out_shape=jax.ShapeDtypeStruct((n, dp + tn), F32),
        scratch_shapes=[pltpu.VMEM((tm, d), BF16)],
        compiler_params=_params("parallel", "arbitrary"),
    )(x2d, g, mod, mod, w_in, w_lora)


def _rope(x, cos, sin_signed):
    lane = _iota(x.shape, 1)
    swapped = jnp.where((lane % 64) < 32, pltpu.roll(x, 96, axis=1), pltpu.roll(x, 32, axis=1))
    return x * cos + swapped * sin_signed


def _softmax_pv(parts, sink_col, vall):
    m = sink_col
    for s in parts:
        m = jnp.maximum(m, jnp.max(s, axis=-1, keepdims=True))
    ps = [jnp.exp(s - m) for s in parts]
    denom = jnp.exp(sink_col - m)
    for p in ps:
        denom = denom + jnp.sum(p, axis=-1, keepdims=True)
    p = ps[0] if len(ps) == 1 else jnp.concatenate(ps, axis=1)
    return _dot(p.astype(BF16), vall) / denom


def _attn_body(q_ref, kp_ref, kc_ref, kn_ref, vp_ref, vc_ref, vn_ref, kx_ref, vx_ref,
               cp_ref, cc_ref, cn_ref, sp_ref, sc_ref, sn_ref, sink_ref, o_ref, *, nb, group):
    n = pl.program_id(1)
    blk = ATTN_BLOCK
    cos = (cp_ref[...], cc_ref[...], cn_ref[...])
    sin = (sp_ref[...], sc_ref[...], sn_ref[...])
    qi = _iota((group * blk, 3 * blk), 0) % blk
    kj = _iota((group * blk, 3 * blk), 1)
    in_seq = ((kj >= blk) | (n > 0)) & ((kj < 2 * blk) | (n < nb - 1))
    band_ok = (jnp.abs(kj - blk - qi) <= WINDOW) & in_seq
    scale = HEAD_DIM ** -0.5
    for h in range(KV_HEADS):
        hs = slice(h * HEAD_DIM, (h + 1) * HEAD_DIM)
        kb = [_rope(r[:, hs], c, s) for r, c, s in zip((kp_ref, kc_ref, kn_ref), cos, sin)]
        kall = jnp.concatenate(kb + [kx_ref[:, hs]], axis=0).astype(BF16)
        vall = jnp.concatenate([vp_ref[:, hs], vc_ref[:, hs], vn_ref[:, hs], vx_ref[:, hs]], axis=0).astype(BF16)
        qs = []
        for g in range(group):
            c0 = (h * group + g) * HEAD_DIM
            qs.append(_rope(q_ref[:, c0:c0 + HEAD_DIM], cos[1], sin[1]))
        qh = jnp.concatenate(qs, axis=0).astype(BF16)
        s = _dot_nt(qh, kall) * scale
        s_loc = jnp.where(band_ok, s[:, :3 * blk], MASKED)
        o = _softmax_pv([s_loc, s[:, 3 * blk:]], sink_ref[h], vall)
        for g in range(group):
            c0 = (h * group + g) * HEAD_DIM
            o_ref[:, c0:c0 + HEAD_DIM] = o[g * blk:(g + 1) * blk].astype(o_ref.dtype)


def _latent_attention(px, pc, cos_t, sin_t, sink_col, b, t, l, aq, akv):
    blk = ATTN_BLOCK
    nb = t // blk
    group = aq // akv
    kcol, vcol = aq // akv, aq // akv + 1
    prev = lambda n: jnp.maximum(n - 1, 0)
    nxt = lambda n: jnp.minimum(n + 1, nb - 1)
    kv_spec = lambda col, f: pl.BlockSpec((blk, akv), lambda bi, n: (bi * nb + f(n), col))
    tab_spec = lambda f: pl.BlockSpec((blk, HEAD_DIM), lambda bi, n: (f(n), 0))
    ident = lambda n: n
    body = functools.partial(_attn_body, nb=nb, group=group)
    return pl.pallas_call(
        body,
        grid=(b, nb),
        in_specs=[pl.BlockSpec((blk, aq), lambda bi, n: (bi * nb + n, 0)),
                  kv_spec(kcol, prev), kv_spec(kcol, ident), kv_spec(kcol, nxt),
                  kv_spec(vcol, prev), kv_spec(vcol, ident), kv_spec(vcol, nxt),
                  pl.BlockSpec((l, akv), lambda bi, n: (bi, kcol)),
                  pl.BlockSpec((l, akv), lambda bi, n: (bi, vcol)),
                  tab_spec(prev), tab_spec(ident), tab_spec(nxt),
                  tab_spec(prev), tab_spec(ident), tab_spec(nxt),
                  pl.BlockSpec((KV_HEADS, group * blk, 1), lambda bi, n: (0, 0, 0))],
        out_specs=pl.BlockSpec((blk, aq), lambda bi, n: (bi * nb + n, 0)),
        out_shape=jax.ShapeDtypeStruct((b * t, aq), BF16),
        compiler_params=_params("parallel", "parallel"),
    )(px, px, px, px, px, px, px, pc, pc, cos_t, cos_t, cos_t, sin_t, sin_t, sin_t, sink_col)


def _ctx_attn_body(q_ref, k_ref, v_ref, sink_ref, o_ref, *, group):
    scale = HEAD_DIM ** -0.5
    for h in range(KV_HEADS):
        hs = slice(h * HEAD_DIM, (h + 1) * HEAD_DIM)
        kall = k_ref[:, hs].astype(BF16)
        vall = v_ref[:, hs].astype(BF16)
        for g in range(group):
            c0 = (h * group + g) * HEAD_DIM
            s = _dot_nt(q_ref[:, c0:c0 + HEAD_DIM].astype(BF16), kall) * scale
            o = _softmax_pv([s], sink_ref[h * group + g], vall)
            o_ref[:, c0:c0 + HEAD_DIM] = o.astype(o_ref.dtype)


def _context_attention(pc, sink_rows, b, l, aq, akv):
    group = aq // akv
    kcol, vcol = aq // akv, aq // akv + 1
    return pl.pallas_call(
        functools.partial(_ctx_attn_body, group=group),
        grid=(b,),
        in_specs=[pl.BlockSpec((l, aq), lambda bi: (bi, 0)),
                  pl.BlockSpec((l, akv), lambda bi: (bi, kcol)),
                  pl.BlockSpec((l, akv), lambda bi: (bi, vcol)),
                  pl.BlockSpec((KV_HEADS * group, l, 1), lambda bi: (0, 0, 0))],
        out_specs=pl.BlockSpec((l, aq), lambda bi: (bi, 0)),
        out_shape=jax.ShapeDtypeStruct((b * l, aq), BF16),
        compiler_params=_params("parallel"),
    )(pc, pc, pc, sink_rows)


def _cmlp_body(u_ref, gv_ref, g_ref, ws_ref, bs_ref, o_ref, *, groups):
    u = jax.nn.gelu(u_ref[...])
    gv = jax.nn.gelu(gv_ref[...])
    gvn = gv * lax.rsqrt(jnp.mean(gv * gv, axis=-1, keepdims=True) + EPS) * g_ref[...]
    for gi in range(groups):
        cs = slice(gi * CMLP_CH, (gi + 1) * CMLP_CH)
        mixed = _dot(ws_ref[gi].astype(BF16), gvn[:, cs].astype(BF16)) + bs_ref[gi]
        o_ref[:, cs] = (u[:, cs] * mixed).astype(o_ref.dtype)


def _chunk_mlp(p, norm_g, ws, bs_b, ucol, cw):
    n = p.shape[0]
    groups = cw // CMLP_CH
    ch = CMLP_CHUNK
    return pl.pallas_call(
        functools.partial(_cmlp_body, groups=groups),
        grid=(n // ch,),
        in_specs=[pl.BlockSpec((ch, cw), lambda i: (i, ucol)),
                  pl.BlockSpec((ch, cw), lambda i: (i, ucol + 1)),
                  pl.BlockSpec((1, cw), lambda i: (0, 0)),
                  pl.BlockSpec((groups, ch, ch), lambda i: (0, 0, 0)),
                  pl.BlockSpec((groups, ch, CMLP_CH), lambda i: (0, 0, 0))],
        out_specs=pl.BlockSpec((ch, cw), lambda i: (i, 0)),
        out_shape=jax.ShapeDtypeStruct((n, cw), BF16),
        compiler_params=_params("parallel"),
    )(p, p, norm_g, ws, bs_b)


def _head_sum(x):
    ones = (_iota((128, 128), 0) // RWKV_HEAD == _iota((128, 128), 1) // RWKV_HEAD).astype(F32)
    cols = [_dot(x[:, s * 128:(s + 1) * 128], ones, precision=HIGHEST) for s in range(x.shape[1] // 128)]
    return cols[0] if len(cols) == 1 else jnp.concatenate(cols, axis=1)


def _prep_body(r_ref, k_ref, v_ref, rp_ref, kp_ref, vp_ref, rn_ref, kn_ref, vn_ref,
               cr_ref, ck_ref, cv_ref, hw_ref, ha_ref, w2_ref, a2_ref, w0_ref, a0_ref, kkp_ref, kap_ref,
               ro_ref, ko_ref, vo_ref, kko_ref, lw_ref, bo_ref, kr_ref, *, tr, seq, lora):
    i = pl.program_id(0)
    first = (i * tr) % seq == 0
    last = ((i + 1) * tr) % seq == 0
    row = _iota(r_ref.shape, 0)

    def conv(x_ref, xp_ref, xn_ref, w_ref):
        x = x_ref[...]
        before = jnp.where(first, 0.0, xp_ref[7:8, :])
        after = jnp.where(last, 0.0, xn_ref[0:1, :])
        xm = jnp.where(row == 0, before, pltpu.roll(x, 1, axis=0))
        xp = jnp.where(row == tr - 1, after, pltpu.roll(x, tr - 1, axis=0))
        return xm * w_ref[0:1, :] + x * w_ref[1:2, :] + xp * w_ref[2:3, :]

    r = conv(r_ref, rp_ref, rn_ref, cr_ref)
    k = conv(k_ref, kp_ref, kn_ref, ck_ref)
    v = conv(v_ref, vp_ref, vn_ref, cv_ref)
    kk = k * kkp_ref[...]
    kk = kk * lax.rsqrt(_head_sum(kk * kk) + 1e-12)
    ro_ref[...] = r
    ko_ref[...] = k
    vo_ref[...] = v
    kko_ref[...] = kk
    for z in range(2):
        zs = slice(z * lora, (z + 1) * lora)
        w_raw = w0_ref[z:z + 1, :] + _dot(jnp.tanh(hw_ref[:, zs]).astype(BF16), w2_ref[z].astype(BF16))
        softplus_neg = jnp.maximum(-w_raw, 0.0) + jnp.log1p(jnp.exp(-jnp.abs(w_raw)))
        lw_ref[z] = -jnp.exp(-softplus_neg - 0.5)
        a = jax.nn.sigmoid(a0_ref[z:z + 1, :] + _dot(ha_ref[:, zs].astype(BF16), a2_ref[z].astype(BF16)))
        kr_ref[z] = k * (1.0 + (a - 1.0) * kap_ref[...])
        bo_ref[z] = kk * a


def _rwkv_prep(p, conv_w, w2, a2, w0, a0, kk_p, ka_p, seq, rcol, hcol, rw, lora):
    n = p.shape[0]
    tr = min(256, seq)
    nh = n // 8
    body = functools.partial(_prep_body, tr=tr, seq=seq, lora=lora)
    cur = lambda c: pl.BlockSpec((tr, rw), lambda i: (i, rcol + c))
    prv = lambda c: pl.BlockSpec((8, rw), lambda i: (jnp.maximum(i * (tr // 8) - 1, 0), rcol + c))
    nxt = lambda c: pl.BlockSpec((8, rw), lambda i: (jnp.minimum((i + 1) * (tr // 8), nh - 1), rcol + c))
    cw = lambda c: pl.BlockSpec((3, rw), lambda i: (0, c))
    full2 = lambda shape: pl.BlockSpec(shape, lambda i: (0,) * len(shape))
    shared = pl.BlockSpec((tr, rw), lambda i: (i, 0))
    directed = pl.BlockSpec((2, tr, rw), lambda i: (0, i, 0))
    return pl.pallas_call(
        body,
        grid=(n // tr,),
        in_specs=[cur(0), cur(1), cur(2), prv(0), prv(1), prv(2), nxt(0), nxt(1), nxt(2),
                  cw(0), cw(1), cw(2),
                  pl.BlockSpec((tr, 2 * lora), lambda i: (i, hcol)),
                  pl.BlockSpec((tr, 2 * lora), lambda i: (i, hcol + 1)),
                  full2((2, lora, rw)), full2((2, lora, rw)), full2((2, rw)), full2((2, rw)),
                  full2((1, rw)), full2((1, rw))],
        out_specs=[shared, shared, shared, shared, directed, directed, directed],
        out_shape=[jax.ShapeDtypeStruct((n, rw), F32)] * 4 + [jax.ShapeDtypeStruct((2, n, rw), F32)] * 3,
        compiler_params=_params("parallel"),
    )(p, p, p, p, p, p, p, p, p, conv_w, conv_w, conv_w, p, p, w2, a2, w0, a0, kk_p, ka_p)


def _mm(a, b, passes, kind="nn"):
    fn = {"nn": _dot, "nt": _dot_nt, "tn": _dot_tn}[kind]
    if passes == 6:
        return fn(a, b, precision=HIGHEST)
    a_hi, b_hi = a.astype(BF16), b.astype(BF16)
    if passes == 1:
        return fn(a_hi, b_hi)
    a_lo = (a - a_hi.astype(F32)).astype(BF16)
    b_lo = (b - b_hi.astype(F32)).astype(BF16)
    return fn(a_hi, b_hi) + (fn(a_lo, b_hi) + fn(a_hi, b_lo))


def _unit_tri_inverse(nmats, eye, same16, same32, passes):
    n16 = [jnp.where(same16, n, 0.0) for n in nmats]
    xs = [eye - n for n in n16]
    pw = n16
    for _ in range(3):
        pw = [_mm(p, p, passes) for p in pw]
        xs = [x + _mm(x, p, passes) for x, p in zip(xs, pw)]
    for mask in (same32 & ~same16, ~same32):
        offs = [jnp.where(mask, n, 0.0) for n in nmats]
        xo = [_mm(x, o, passes) for x, o in zip(xs, offs)]
        xs = [x - _mm(t, x, passes) for x, t in zip(xs, xo)]
    return xs


def _scan_body(r_ref, kk_ref, v_ref, lw_ref, b_ref, k_ref, s0_ref, y_ref, sf_ref, st_ref, *, heads, nchunks):
    z = pl.program_id(0)
    c = pl.program_id(3)
    C, K = SCAN_CHUNK, RWKV_HEAD
    pp = SCAN_PASSES

    @pl.when(c == 0)
    def _():
        st_ref[...] = s0_ref[...]

    ti, si = _iota((C, C), 0), _iota((C, C), 1)
    before = (si - ti) * (1 - 2 * z) < 0
    upto = before | (si == ti)
    eye = (si == ti).astype(F32)
    same16 = (ti // 16) == (si // 16)
    same32 = (ti // 32) == (si // 32)

    lw = lw_ref[...]
    lc = _dot(upto.astype(F32), lw, precision=HIGHEST)
    ltot = jnp.sum(lw, axis=0, keepdims=True)
    e_neg = jnp.exp(-lc)
    e_out = jnp.exp(ltot - lc)
    kkt = kk_ref[...] * jnp.exp(lc - lw)
    rt = r_ref[...] * jnp.exp(lc)
    bt = b_ref[...] * e_neg
    kt = k_ref[...] * e_neg
    bh = b_ref[...] * e_out
    kh = k_ref[...] * e_out
    etot = jnp.exp(ltot)
    v = v_ref[...]

    hr = range(heads)
    ls = [slice(h * K, (h + 1) * K) for h in hr]
    ps = [_mm(jnp.concatenate([kkt[:, s], rt[:, s]], axis=0),
              jnp.concatenate([bt[:, s], kt[:, s]], axis=0), pp["pair"], "nt") for s in ls]
    nmats = [jnp.where(before, p[:C, :C], 0.0) for p in ps]
    pkk = [jnp.where(before, p[:C, C:], 0.0) for p in ps]
    prb = [jnp.where(upto, p[C:, :C], 0.0) for p in ps]
    prk = [jnp.where(upto, p[C:, C:], 0.0) for p in ps]
    tinv = _unit_tri_inverse(nmats, eye, same16, same32, pp["inv"])
    tg = [_mm(tinv[h], jnp.concatenate([kkt[:, ls[h]], pkk[h]], axis=1), pp["solve"]) for h in hr]
    qa = [jnp.concatenate([rt[:, ls[h]], prk[h]], axis=1) - _mm(prb[h], tg[h], pp["solve"]) for h in hr]
    m3 = [_mm(bh[:, ls[h]], tg[h], pp["solve"], "tn") for h in hr]
    sv = [jnp.concatenate([st_ref[h], v[:, ls[h]]], axis=0) for h in hr]
    ys = [_mm(qa[h], sv[h], pp["state"]) for h in hr]
    for h in hr:
        decay_diag = eye * jnp.broadcast_to(etot[:, ls[h]], (K, K))
        trans = jnp.concatenate([decay_diag, jnp.zeros((K, C), F32)], axis=1) - m3[h]
        st_ref[h] = _mm(trans, sv[h], pp["state"]) + _mm(kh[:, ls[h]], v[:, ls[h]], pp["state"], "tn")
    y_ref[...] = ys[0] if heads == 1 else jnp.concatenate(ys, axis=1)

    @pl.when(c == nchunks - 1)
    def _():
        sf_ref[...] = st_ref[...]


def _rwkv_scan(r, kk, v, lw, bb, kr, s0, b, seq, rw):
    C, K = SCAN_CHUNK, RWKV_HEAD
    nchunks = seq // C
    heads = min(SCAN_HEADS, rw // K)
    ngroups = rw // (heads * K)
    n = b * seq
    row = lambda z, bi, hg, c: bi * nchunks + c + z * (nchunks - 1 - 2 * c)
    shared = pl.BlockSpec((C, heads * K), lambda z, bi, hg, c: (row(z, bi, hg, c), hg))
    directed = pl.BlockSpec((None, C, heads * K), lambda z, bi, hg, c: (z, row(z, bi, hg, c), hg))
    state = pl.BlockSpec((None, None, heads, K, K), lambda z, bi, hg, c: (z, bi, hg, 0, 0))
    body = functools.partial(_scan_body, heads=heads, nchunks=nchunks)
    return pl.pallas_call(
        body,
        grid=(2, b, ngroups, nchunks),
        in_specs=[shared, shared, shared, directed, directed, directed, state],
        out_specs=[directed, state],
        out_shape=[jax.ShapeDtypeStruct((2, n, rw), F32),
                   jax.ShapeDtypeStruct((2, b, rw // K, K, K), F32)],
        scratch_shapes=[pltpu.VMEM((heads, K, K), F32)],
        compiler_params=_params("parallel", "parallel", "parallel", "arbitrary"),
    )(r, kk, v, lw, bb, kr, s0)


def _rwkv_out_body(yf_ref, yb_ref, r_ref, k_ref, v_ref, g_ref, rk_ref, lnw_ref, lnb_ref, o_ref):
    y = yf_ref[...] + yb_ref[...]
    inv = 1.0 / RWKV_HEAD
    mu = _head_sum(y) * inv
    d = y - mu
    var = _head_sum(d * d) * inv
    yn = d * lax.rsqrt(var + RWKV_GN_EPS) * lnw_ref[...] + lnb_ref[...]
    bonus = _head_sum(r_ref[...] * k_ref[...] * rk_ref[...]) * v_ref[...]
    o_ref[...] = ((yn + bonus) * jax.nn.sigmoid(g_ref[...])).astype(o_ref.dtype)


def _rwkv_output(y, r, k, v, p, gcol, rk, ln_w, ln_b):
    n, rw = r.shape
    tr = min(256, n)
    shared = pl.BlockSpec((tr, rw), lambda i: (i, 0))
    vec = pl.BlockSpec((1, rw), lambda i: (0, 0))
    return pl.pallas_call(
        _rwkv_out_body,
        grid=(n // tr,),
        in_specs=[pl.BlockSpec((None, tr, rw), lambda i: (0, i, 0)),
                  pl.BlockSpec((None, tr, rw), lambda i: (1, i, 0)),
                  shared, shared, shared,
                  pl.BlockSpec((tr, rw), lambda i: (i, gcol)),
                  vec, vec, vec],
        out_specs=shared,
        out_shape=jax.ShapeDtypeStruct((n, rw), BF16),
        compiler_params=_params("parallel"),
    )(y, y, r, k, v, p, rk, ln_w, ln_b)


def _wout_body(a_ref, c_ref, r_ref, wa_ref, wc_ref, wr_ref, x_ref, g_ref, o_ref, *, tm, rows_per_mod, fixed_row):
    row = _mod_row(pl.program_id(0), tm, rows_per_mod, fixed_row)
    acc = _dot(a_ref[...], wa_ref[...]) + _dot(c_ref[...], wc_ref[...]) + _dot(r_ref[...], wr_ref[...])
    o_ref[...] = x_ref[...] + g_ref[pl.ds(row, 1), :] * acc


def _out_proj(attn, cmlp, rwkv, w_out, x2d, mod, layer, rows_per_mod, fixed_row):
    n, d = x2d.shape
    aq, cw, rw = attn.shape[1], cmlp.shape[1], rwkv.shape[1]
    tm = min(512, n)
    tn = min(1024, d)
    body = functools.partial(_wout_body, tm=tm, rows_per_mod=rows_per_mod, fixed_row=fixed_row)
    return pl.pallas_call(
        body,
        grid=(n // tm, d // tn),
        in_specs=[pl.BlockSpec((tm, aq), lambda i, j: (i, 0)),
                  pl.BlockSpec((tm, cw), lambda i, j: (i, 0)),
                  pl.BlockSpec((tm, rw), lambda i, j: (i, 0)),
                  pl.BlockSpec((None, aq, tn), lambda i, j: (layer, 0, j)),
                  pl.BlockSpec((None, cw, tn), lambda i, j: (layer, aq // cw, j)),
                  pl.BlockSpec((None, rw, tn), lambda i, j: (layer, (aq + cw) // rw, j)),
                  pl.BlockSpec((tm, tn), lambda i, j: (i, j)),
                  pl.BlockSpec((None, MOD_ROWS, tn), lambda i, j: (layer, 0, 2 * (d // tn) + j))],
        out_specs=pl.BlockSpec((tm, tn), lambda i, j: (i, j)),
        out_shape=jax.ShapeDtypeStruct((n, d), F32),
        compiler_params=_params("parallel", "parallel"),
    )(attn, cmlp, rwkv, w_out, w_out, w_out, x2d, mod)


def _router_body(x_ref, g_ref, sh_ref, sc_ref, rw_ref, rb_ref, zn_ref, ids_ref, wt_ref, *,
                 tm, rows_per_mod, fixed_row, experts):
    row = _mod_row(pl.program_id(0), tm, rows_per_mod, fixed_row)
    zn = _modulated_norm(x_ref[...], g_ref[...], sh_ref[pl.ds(row, 1), :], sc_ref[pl.ds(row, 1), :])
    zn_ref[...] = zn
    logits = _dot(zn, rw_ref[...], precision=HIGHEST).T
    per_group = experts // N_EXPERT_GROUPS
    scores = [jax.nn.sigmoid(logits[e:e + 1, :]) for e in range(experts)]
    sel = [scores[e] + rb_ref[e:e + 1, :] for e in range(experts)]
    best_val, best_grp = None, None
    for gi in range(N_EXPERT_GROUPS):
        mem = sel[gi * per_group:(gi + 1) * per_group]
        top2 = None
        for a in range(per_group):
            for b2 in range(a + 1, per_group):
                pair = mem[a] + mem[b2]
                top2 = pair if top2 is None else jnp.maximum(top2, pair)
        if gi == 0:
            best_val, best_grp = top2, jnp.zeros(top2.shape, jnp.int32)
        else:
            better = top2 > best_val
            best_grp = jnp.where(better, gi, best_grp)
            best_val = jnp.where(better, top2, best_val)
    chosen, picked = [], []
    for e in range(experts):
        gi = e // per_group
        rank = jnp.zeros(best_grp.shape, jnp.int32)
        for j in range(gi * per_group, (gi + 1) * per_group):
            if j != e:
                ahead = (sel[j] > sel[e]) | ((sel[j] == sel[e]) & (j < e))
                rank = rank + ahead.astype(jnp.int32)
        chosen.append((best_grp == gi) & (rank < TOP_K))
        picked.append(jnp.where(chosen[e], scores[e], 0.0))
    total = picked[0]
    for e in range(1, experts):
        total = total + picked[e]
    zero_i, zero_f = jnp.zeros(total.shape, jnp.int32), jnp.zeros(total.shape, F32)
    seen, ids, wts = zero_i, [zero_i, zero_i], [zero_f, zero_f]
    for e in range(experts):
        gate = picked[e] / total
        for slot in range(TOP_K):
            here = chosen[e] & (seen == slot)
            ids[slot] = jnp.where(here, e, ids[slot])
            wts[slot] = jnp.where(here, gate, wts[slot])
        seen = seen + chosen[e].astype(jnp.int32)
    ids_ref[...] = jnp.concatenate(ids + [jnp.zeros((8 - TOP_K, tm), jnp.int32)], axis=0)
    wt_ref[...] = jnp.concatenate(wts + [jnp.zeros((128 - TOP_K, tm), F32)], axis=0).T


def _router(x2d, g, mod, router_w_pad, router_b_col, layer, rows_per_mod, fixed_row, experts):
    n, d = x2d.shape
    tm = min(256, n)
    body = functools.partial(_router_body, tm=tm, rows_per_mod=rows_per_mod, fixed_row=fixed_row, experts=experts)
    return pl.pallas_call(
        body,
        grid=(n // tm,),
        in_specs=[pl.BlockSpec((tm, d), lambda i: (i, 0)),
                  pl.BlockSpec((None, 1, d), lambda i: (layer, 0, 0)),
                  pl.BlockSpec((None, MOD_ROWS, d), lambda i: (layer, 0, 3)),
                  pl.BlockSpec((None, MOD_ROWS, d), lambda i: (layer, 0, 4)),
                  pl.BlockSpec((d, 128), lambda i: (0, 0)),
                  pl.BlockSpec((128, 1), lambda i: (0, 0))],
        out_specs=[pl.BlockSpec((tm, d), lambda i: (i, 0)),
                   pl.BlockSpec((8, tm), lambda i: (0, i)),
                   pl.BlockSpec((tm, 128), lambda i: (i, 0))],
        out_shape=[jax.ShapeDtypeStruct((n, d), F32), jax.ShapeDtypeStruct((8, n), jnp.int32),
                   jax.ShapeDtypeStruct((n, 128), F32)],
        compiler_params=_params("parallel"),
    )(x2d, g, mod, mod, router_w_pad, router_b_col)


def _route_plan(ids, experts, tm):
    n = ids.shape[1]
    total = TOP_K * n + experts * tm
    flat = ids.reshape(-1)
    onehot = (flat[:, None] == jnp.arange(experts, dtype=jnp.int32)[None, :]).astype(jnp.int32)
    rank = jnp.cumsum(onehot, axis=0) - onehot
    padded = ((jnp.sum(onehot, axis=0) + tm - 1) // tm) * tm
    ends = jnp.cumsum(padded)
    pos = (ends - padded)[flat] + jnp.sum(rank * onehot, axis=1)
    row_token = jnp.zeros((total,), jnp.int32).at[pos].set(jnp.tile(jnp.arange(n, dtype=jnp.int32), TOP_K))
    tile_start = jnp.arange(total // tm, dtype=jnp.int32) * tm
    tile_used = (tile_start < ends[-1]).astype(jnp.int32)
    tile_expert = jnp.minimum(jnp.searchsorted(ends, tile_start, side="right"), experts - 1).astype(jnp.int32)
    last_used = tile_expert[jnp.maximum(ends[-1] // tm - 1, 0)]
    tile_expert = jnp.where(tile_used == 1, tile_expert, last_used)
    return pos.astype(jnp.int32), row_token, tile_expert, tile_used


def _row_copy(src_hbm, row, dst_ref, i, sem):
    return pltpu.make_async_copy(src_hbm.at[pl.ds(row, 1), :], dst_ref.at[pl.ds(i, 1), :], sem)


def _start_rows(src_hbm, dst_ref, sem, index_of, priorities):
    k = len(priorities)

    def start(i, carry):
        for j, prio in enumerate(priorities):
            _row_copy(src_hbm, index_of(i * k + j), dst_ref, i * k + j, sem).start(priority=prio)
        return carry

    lax.fori_loop(0, dst_ref.shape[0] // k, start, 0, unroll=8 // k)


def _wait_rows(src_hbm, dst_ref, sem):
    pltpu.make_async_copy(src_hbm.at[pl.ds(0, dst_ref.shape[0]), :], dst_ref, sem).wait()


def _expert_hidden_body(te_ref, used_ref, tok_ref, zn_hbm, w1_ref, w3_ref, h_ref, xg_ref, sem, *, tm, ntiles):
    p, f = pl.program_id(0), pl.program_id(1)
    used = used_ref[p] == 1
    slot = p % 2
    rows_of = lambda tile: (lambda i: tok_ref[tile * tm + i])

    @pl.when(used & (f == 0) & (p == 0))
    def _():
        _start_rows(zn_hbm, xg_ref.at[0], sem.at[0], rows_of(0), GATHER_PRIORITY)

    @pl.when(used & (f == 0))
    def _():
        _wait_rows(zn_hbm, xg_ref.at[slot], sem.at[slot])

        @pl.when((p + 1 < ntiles) & (used_ref[jnp.minimum(p + 1, ntiles - 1)] == 1))
        def _():
            _start_rows(zn_hbm, xg_ref.at[1 - slot], sem.at[1 - slot], rows_of(p + 1), GATHER_PRIORITY)

    @pl.when(used)
    def _():
        x = xg_ref[slot].astype(BF16)
        h1 = _dot(x, w1_ref[...].astype(BF16))
        h3 = _dot(x, w3_ref[...].astype(BF16))
        h_ref[...] = ((h1 * jax.nn.sigmoid(h1)) * h3).astype(h_ref.dtype)

    @pl.when(jnp.logical_not(used))
    def _():
        h_ref[...] = jnp.zeros_like(h_ref)


def _expert_out_body(te_ref, used_ref, h_ref, w2_ref, o_ref):
    o_ref[...] = _dot(h_ref[...], w2_ref[...].astype(BF16))


def _experts(zn, row_token, tile_expert, tile_used, w1, w3, w2, layer, tm):
    n, d = zn.shape
    ff = w1.shape[3]
    total = row_token.shape[0]
    ntiles = total // tm
    tf = min(256, ff)
    n_f = ff // tf
    tn = min(2048, d)
    n_j = d // tn
    hold = lambda used, p, j, last: jnp.where(used[p] == 1, j, last)
    hidden = pl.pallas_call(
        functools.partial(_expert_hidden_body, tm=tm, ntiles=ntiles),
        grid_spec=pltpu.PrefetchScalarGridSpec(
            num_scalar_prefetch=3,
            grid=(ntiles, n_f),
            in_specs=[pl.BlockSpec(memory_space=pl.ANY),
                      pl.BlockSpec((None, None, d, tf),
                                   lambda p, f, te, us, tk: (layer, te[p], 0, hold(us, p, f, n_f - 1))),
                      pl.BlockSpec((None, None, d, tf),
                                   lambda p, f, te, us, tk: (layer, te[p], 0, hold(us, p, f, n_f - 1)))],
            out_specs=pl.BlockSpec((tm, tf), lambda p, f, te, us, tk: (p, f)),
            scratch_shapes=[pltpu.VMEM((2, tm, d), F32), pltpu.SemaphoreType.DMA((2,))]),
        out_shape=jax.ShapeDtypeStruct((total, ff), BF16),
        compiler_params=_params("arbitrary", "arbitrary", disable_bounds_checks=True),
    )(tile_expert, tile_used, row_token, zn, w1, w3)
    return pl.pallas_call(
        _expert_out_body,
        grid_spec=pltpu.PrefetchScalarGridSpec(
            num_scalar_prefetch=2,
            grid=(n_j, ntiles),
            in_specs=[pl.BlockSpec((tm, ff), lambda j, p, te, us: (p, 0)),
                      pl.BlockSpec((None, None, ff, tn), lambda j, p, te, us: (layer, te[p], 0, j))],
            out_specs=pl.BlockSpec((tm, tn), lambda j, p, te, us: (p, j))),
        out_shape=jax.ShapeDtypeStruct((total, d), F32),
        compiler_params=_params("parallel", "parallel"),
    )(tile_expert, tile_used, hidden, w2)


def _combine_body(pos_ref, ys_hbm, wt_ref, x_ref, g2_ref, fg_ref, o_ref, ya_ref, yb_ref, sem_a, sem_b, *,
                  tm, n, rows_per_mod, fixed_row, final_norm):
    i = pl.program_id(0)
    slot = i % 2

    def start(tile, s):
        _start_rows(ys_hbm, ya_ref.at[s], sem_a.at[s], lambda r: pos_ref[tile * tm + r], COMBINE_PRIORITY)
        _start_rows(ys_hbm, yb_ref.at[s], sem_b.at[s], lambda r: pos_ref[n + tile * tm + r], COMBINE_PRIORITY)

    @pl.when(i == 0)
    def _():
        start(0, 0)

    @pl.when(i + 1 < n // tm)
    def _():
        start(i + 1, 1 - slot)

    _wait_rows(ys_hbm, ya_ref.at[slot], sem_a.at[slot])
    _wait_rows(ys_hbm, yb_ref.at[slot], sem_b.at[slot])
    row = _mod_row(i, tm, rows_per_mod, fixed_row)
    wt = wt_ref[...]
    mix = wt[:, 0:1] * ya_ref[slot] + wt[:, 1:2] * yb_ref[slot]
    out = x_ref[...] + g2_ref[pl.ds(row, 1), :] * mix
    if final_norm:
        out = out * lax.rsqrt(jnp.mean(out * out, axis=-1, keepdims=True) + EPS) * fg_ref[...]
    o_ref[...] = out


def _moe_combine(ys, pos, wts, x2d, mod, final_g, layer, rows_per_mod, fixed_row, final_norm):
    n, d = x2d.shape
    tm = min(256, n)
    body = functools.partial(_combine_body, tm=tm, n=n, rows_per_mod=rows_per_mod, fixed_row=fixed_row,
                             final_norm=final_norm)
    return pl.pallas_call(
        body,
        grid_spec=pltpu.PrefetchScalarGridSpec(
            num_scalar_prefetch=1,
            grid=(n // tm,),
            in_specs=[pl.BlockSpec(memory_space=pl.ANY),
                      pl.BlockSpec((tm, 128), lambda i, ps: (i, 0)),
                      pl.BlockSpec((tm, d), lambda i, ps: (i, 0)),
                      pl.BlockSpec((None, MOD_ROWS, d), lambda i, ps: (layer, 0, 5)),
                      pl.BlockSpec((1, d), lambda i, ps: (0, 0))],
            out_specs=pl.BlockSpec((tm, d), lambda i, ps: (i, 0)),
            scratch_shapes=[pltpu.VMEM((2, tm, d), F32), pltpu.VMEM((2, tm, d), F32),
                            pltpu.SemaphoreType.DMA((2,)), pltpu.SemaphoreType.DMA((2,))]),
        out_shape=jax.ShapeDtypeStruct((n, d), F32),
        compiler_params=_params("arbitrary", disable_bounds_checks=True),
    )(pos, ys, wts, x2d, mod, final_g)


def _moe(x2d, g, mod, router_w_pad, router_b_col, w1, w3, w2, final_g, layer, experts,
         rows_per_mod, fixed_row, final_norm=False):
    n = x2d.shape[0]
    tm = 512 if n >= 8 * 512 else 128
    zn, ids, wts = _router(x2d, g, mod, router_w_pad, router_b_col, layer, rows_per_mod, fixed_row, experts)
    pos, row_token, tile_expert, tile_used = _route_plan(ids[:TOP_K], experts, tm)
    ys = _experts(zn, row_token, tile_expert, tile_used, w1, w3, w2, layer, tm)
    return _moe_combine(ys, pos, wts, x2d, mod, final_g, layer, rows_per_mod, fixed_row, final_norm)


def _rope_tables(t):
    pos = jnp.arange(t)
    half = HEAD_DIM // 4
    freqs = ROPE_BASE ** (-jnp.arange(half, dtype=F32) / half)
    ang_r = (pos // GRID_W).astype(F32)[:, None] * freqs[None, :]
    ang_c = (pos % GRID_W).astype(F32)[:, None] * freqs[None, :]
    cos_t = jnp.concatenate([jnp.cos(ang_r)] * 2 + [jnp.cos(ang_c)] * 2, axis=1)
    sin_t = jnp.concatenate([-jnp.sin(ang_r), jnp.sin(ang_r), -jnp.sin(ang_c), jnp.sin(ang_c)], axis=1)
    return cos_t, sin_t


def kernel(x, c, ctx, c_ctx, ada_w, ada_b, norm1_g, w_in, rwkv_conv, attn_sink, cmlp_norm_g, cmlp_ws, cmlp_b,
           rwkv_w0, rwkv_w1, rwkv_w2, rwkv_a0, rwkv_a1, rwkv_a2, rwkv_kk, rwkv_ka, rwkv_rk, rwkv_ln_w, rwkv_ln_b,
           w_out, norm2_g, router_w, router_b, moe_w1, moe_w3, moe_w2, final_g):
    b, t, d = x.shape
    l = ctx.shape[1]
    depth = ada_w.shape[0]
    cw = cmlp_norm_g.shape[1]
    rw = rwkv_w0.shape[2]
    lora = rwkv_w1.shape[3]
    experts = router_w.shape[1]
    dp = w_in.shape[2]
    akv = KV_HEADS * HEAD_DIM
    aq = dp - 2 * akv - 2 * cw - 4 * rw
    group = aq // akv
    heads = rw // RWKV_HEAD
    ucol = (aq + 2 * akv) // cw
    rcol = (aq + 2 * akv + 2 * cw) // rw
    gcol = rcol + 3
    hcol = dp // (2 * lora)
    assert (aq + 2 * akv) % cw == 0 and (aq + 2 * akv + 2 * cw) % rw == 0 and dp % (2 * lora) == 0
    assert b + 1 <= MOD_ROWS and (b * t) % l == 0 and t % 256 == 0 and l % 128 == 0

    cpad = jnp.zeros((MOD_ROWS, d), F32).at[:b].set(c).at[b].set(c_ctx)
    mod = _ada(cpad, ada_w, ada_b)
    cos_t, sin_t = _rope_tables(t)
    router_w_pad = jnp.zeros((d, 128), F32).at[:, :experts].set(router_w)
    router_b_col = jnp.zeros((128, 1), F32).at[:experts, 0].set(router_b)
    w_out_bf = w_out.astype(BF16)
    w_in_bf = w_in.astype(BF16)
    w_lora_bf = jnp.concatenate([rwkv_w1[:, 0], rwkv_w1[:, 1], rwkv_a1[:, 0], rwkv_a1[:, 1]], axis=2).astype(BF16)
    assert dp % w_lora_bf.shape[2] == 0
    s_zero = jnp.zeros((2, b, heads, RWKV_HEAD, RWKV_HEAD), F32)

    xs = x.reshape(b * t, d)
    hs = ctx.reshape(b * l, d)
    for layer in range(depth):
        lat = dict(rows_per_mod=t, fixed_row=None)
        con = dict(rows_per_mod=None, fixed_row=b)
        last = layer == depth - 1
        px = _proj(xs, norm1_g.reshape(depth, 1, d), mod, w_in_bf, w_lora_bf, layer, **lat)
        pc = _proj(hs, norm1_g.reshape(depth, 1, d), mod, w_in_bf, w_lora_bf, layer, **con)

        sink = attn_sink[layer]
        sink_col = jnp.repeat(sink.reshape(KV_HEADS, group), ATTN_BLOCK, axis=1)[..., None]
        attn_x = _latent_attention(px, pc, cos_t, sin_t, sink_col, b, t, l, aq, akv)

        bs_b = jnp.broadcast_to(cmlp_b[layer][:, :, None], cmlp_b.shape[1:] + (CMLP_CH,))
        cmlp_x = _chunk_mlp(px, cmlp_norm_g[layer][None], cmlp_ws[layer], bs_b, ucol, cw)

        prep_args = (rwkv_conv[layer], rwkv_w2[layer], rwkv_a2[layer], rwkv_w0[layer], rwkv_a0[layer],
                     rwkv_kk[layer][None], rwkv_ka[layer][None])
        rc, kc, vc, kkc, lwc, bbc, krc = _rwkv_prep(pc, *prep_args, l, rcol, hcol, rw, lora)
        rx, kx, vx, kkx, lwx, bbx, krx = _rwkv_prep(px, *prep_args, t, rcol, hcol, rw, lora)
        y_c, s_ctx = _rwkv_scan(rc, kkc, vc, lwc, bbc, krc, s_zero, b, l, rw)
        y_x, _ = _rwkv_scan(rx, kkx, vx, lwx, bbx, krx, s_ctx, b, t, rw)
        out_args = (rwkv_rk[layer][None], rwkv_ln_w[layer][None], rwkv_ln_b[layer][None])
        rwkv_x = _rwkv_output(y_x, rx, kx, vx, px, gcol, *out_args)

        xs = _out_proj(attn_x, cmlp_x, rwkv_x, w_out_bf, xs, mod, layer, **lat)
        moe_args = (norm2_g.reshape(depth, 1, d), mod, router_w_pad, router_b_col, moe_w1, moe_w3, moe_w2,
                    final_g[None], layer, experts)
        xs = _moe(xs, *moe_args, final_norm=last, **lat)

        if not last:
            sink_rows = jnp.broadcast_to(sink[:, None, None], (KV_HEADS * group, l, 1))
            attn_c = _context_attention(pc, sink_rows, b, l, aq, akv)
            cmlp_c = _chunk_mlp(pc, cmlp_norm_g[layer][None], cmlp_ws[layer], bs_b, ucol, cw)
            rwkv_c = _rwkv_output(y_c, rc, kc, vc, pc, gcol, *out_args)
            hs = _out_proj(attn_c, cmlp_c, rwkv_c, w_out_bf, hs, mod, layer, **con)
            hs = _moe(hs, *moe_args, **con)
    return xs.reshape(b, t, d)
```

```python
import functools

import jax
import jax.numpy as jnp
from jax import lax
from jax.experimental import pallas as pl
from jax.experimental.pallas import tpu as pltpu

F32, BF16 = jnp.float32, jnp.bfloat16
HIGHEST = lax.Precision.HIGHEST

HEAD_DIM = 128
KV_HEADS = 4
WINDOW = 128
ATTN_BLOCK = 128
GRID_W = 64
ROPE_BASE = 10000.0
CMLP_CH = 128
CMLP_CHUNK = 128
RWKV_HEAD = 64
RWKV_GN_EPS = 64e-5
N_EXPERT_GROUPS = 4
TOP_K = 2
N_MOD = 6
EPS = 1e-6
MASKED = -1e30

MOD_ROWS = 8
SCAN_CHUNK = 64
SCAN_HEADS = 16
SCAN_PASSES = {"pair": 1, "inv": 1, "solve": 1, "state": 1}
GATHER_PRIORITY = (1,)
COMBINE_PRIORITY = (0, 1)
VMEM_LIMIT_BYTES = 56 * 1024 * 1024


def _params(*sem, **kw):
    return pltpu.CompilerParams(dimension_semantics=sem, vmem_limit_bytes=VMEM_LIMIT_BYTES, **kw)


def _dot(a, b, **kw):
    return jnp.dot(a, b, preferred_element_type=F32, **kw)


def _dot_nt(a, b, **kw):
    return lax.dot_general(a, b, (((1,), (1,)), ((), ())), preferred_element_type=F32, **kw)


def _dot_tn(a, b, **kw):
    return lax.dot_general(a, b, (((0,), (0,)), ((), ())), preferred_element_type=F32, **kw)


def _iota(shape, dim):
    return lax.broadcasted_iota(jnp.int32, shape, dim)


def _ada_body(c_ref, w_ref, b_ref, o_ref):
    c = c_ref[...]
    a = (c * jax.nn.sigmoid(c)).astype(BF16)
    o_ref[...] = _dot(a, w_ref[...].astype(BF16)) + b_ref[...]


def _ada(cpad, ada_w, ada_b):
    depth, d, n = ada_w.shape
    tn = 512
    return pl.pallas_call(
        _ada_body,
        grid=(depth, n // tn),
        in_specs=[pl.BlockSpec((MOD_ROWS, d), lambda l, j: (0, 0)),
                  pl.BlockSpec((None, d, tn), lambda l, j: (l, 0, j)),
                  pl.BlockSpec((None, 1, tn), lambda l, j: (l, 0, j))],
        out_specs=pl.BlockSpec((None, MOD_ROWS, tn), lambda l, j: (l, 0, j)),
        out_shape=jax.ShapeDtypeStruct((depth, MOD_ROWS, n), F32),
        compiler_params=_params("parallel", "parallel"),
    )(cpad, ada_w, ada_b.reshape(depth, 1, n))


def _mod_row(i, tm, rows_per_mod, fixed_row):
    return fixed_row if rows_per_mod is None else (i * tm) // rows_per_mod


def _modulated_norm(x, g, shift, scale):
    y = x * lax.rsqrt(jnp.mean(x * x, axis=-1, keepdims=True) + EPS) * g
    return y * (1.0 + scale) + shift


def _proj_body(x_ref, g_ref, sh_ref, sc_ref, w_ref, wl_ref, o_ref, xn_ref, *, tm, n_main, rows_per_mod, fixed_row):
    j = pl.program_id(1)

    @pl.when(j == 0)
    def _():
        r = _mod_row(pl.program_id(0), tm, rows_per_mod, fixed_row)
        xn = _modulated_norm(x_ref[...], g_ref[...], sh_ref[pl.ds(r, 1), :], sc_ref[pl.ds(r, 1), :])
        xn_ref[...] = xn.astype(BF16)

    @pl.when(j < n_main)
    def _():
        o_ref[...] = _dot(xn_ref[...], w_ref[...])

    @pl.when(j >= n_main)
    def _():
        o_ref[...] = _dot(xn_ref[...], wl_ref[...])


def _proj(x2d, g, mod, w_in, w_lora, layer, rows_per_mod, fixed_row):
    n, d = x2d.shape
    n_main, tn = w_in.shape[1], w_in.shape[3]
    dp = n_main * tn
    tm = min(512, n)
    body = functools.partial(_proj_body, tm=tm, n_main=n_main, rows_per_mod=rows_per_mod, fixed_row=fixed_row)
    return pl.pallas_call(
        body,
        grid=(n // tm, n_main + 1),
        in_specs=[pl.BlockSpec((tm, d), lambda i, j: (i, 0)),
                  pl.BlockSpec((None, 1, d), lambda i, j: (layer, 0, 0)),
                  pl.BlockSpec((None, MOD_ROWS, d), lambda i, j: (layer, 0, 0)),
                  pl.BlockSpec((None, MOD_ROWS, d), lambda i, j: (layer, 0, 1)),
                  pl.BlockSpec((None, None, d, tn), lambda i, j: (layer, jnp.minimum(j, n_main - 1), 0, 0)),
                  pl.BlockSpec((None, d, tn), lambda i, j: (layer, 0, 0))],
        out_specs=pl.BlockSpec((tm, tn), lambda i, j: (i, j)),
        out_shape=jax.ShapeDtypeStruct((n, dp + tn), F32),
        scratch_shapes=[pltpu.VMEM((tm, d), BF16)],
        compiler_params=_params("parallel", "arbitrary"),
    )(x2d, g, mod, mod, w_in, w_lora)


def _rope(x, cos, sin_signed):
    lane = _iota(x.shape, 1)
    swapped = jnp.where((lane % 64) < 32, pltpu.roll(x, 96, axis=1), pltpu.roll(x, 32, axis=1))
    return x * cos + swapped * sin_signed


def _softmax_pv(parts, sink_col, vall):
    m = sink_col
    for s in parts:
        m = jnp.maximum(m, jnp.max(s, axis=-1, keepdims=True))
    ps = [jnp.exp(s - m) for s in parts]
    denom = jnp.exp(sink_col - m)
    for p in ps:
        denom = denom + jnp.sum(p, axis=-1, keepdims=True)
    p = ps[0] if len(ps) == 1 else jnp.concatenate(ps, axis=1)
    return _dot(p.astype(BF16), vall) / denom


def _attn_body(q_ref, kp_ref, kc_ref, kn_ref, vp_ref, vc_ref, vn_ref, kx_ref, vx_ref,
               cp_ref, cc_ref, cn_ref, sp_ref, sc_ref, sn_ref, sink_ref, o_ref, *, nb, group):
    n = pl.program_id(1)
    blk = ATTN_BLOCK
    cos = (cp_ref[...], cc_ref[...], cn_ref[...])
    sin = (sp_ref[...], sc_ref[...], sn_ref[...])
    qi = _iota((group * blk, 3 * blk), 0) % blk
    kj = _iota((group * blk, 3 * blk), 1)
    in_seq = ((kj >= blk) | (n > 0)) & ((kj < 2 * blk) | (n < nb - 1))
    band_ok = (jnp.abs(kj - blk - qi) <= WINDOW) & in_seq
    scale = HEAD_DIM ** -0.5
    for h in range(KV_HEADS):
        hs = slice(h * HEAD_DIM, (h + 1) * HEAD_DIM)
        kb = [_rope(r[:, hs], c, s) for r, c, s in zip((kp_ref, kc_ref, kn_ref), cos, sin)]
        kall = jnp.concatenate(kb + [kx_ref[:, hs]], axis=0).astype(BF16)
        vall = jnp.concatenate([vp_ref[:, hs], vc_ref[:, hs], vn_ref[:, hs], vx_ref[:, hs]], axis=0).astype(BF16)
        qs = []
        for g in range(group):
            c0 = (h * group + g) * HEAD_DIM
            qs.append(_rope(q_ref[:, c0:c0 + HEAD_DIM], cos[1], sin[1]))
        qh = jnp.concatenate(qs, axis=0).astype(BF16)
        s = _dot_nt(qh, kall) * scale
        s_loc = jnp.where(band_ok, s[:, :3 * blk], MASKED)
        o = _softmax_pv([s_loc, s[:, 3 * blk:]], sink_ref[h], vall)
        for g in range(group):
            c0 = (h * group + g) * HEAD_DIM
            o_ref[:, c0:c0 + HEAD_DIM] = o[g * blk:(g + 1) * blk].astype(o_ref.dtype)


def _latent_attention(px, pc, cos_t, sin_t, sink_col, b, t, l, aq, akv):
    blk = ATTN_BLOCK
    nb = t // blk
    group = aq // akv
    kcol, vcol = aq // akv, aq // akv + 1
    prev = lambda n: jnp.maximum(n - 1, 0)
    nxt = lambda n: jnp.minimum(n + 1, nb - 1)
    kv_spec = lambda col, f: pl.BlockSpec((blk, akv), lambda bi, n: (bi * nb + f(n), col))
    tab_spec = lambda f: pl.BlockSpec((blk, HEAD_DIM), lambda bi, n: (f(n), 0))
    ident = lambda n: n
    body = functools.partial(_attn_body, nb=nb, group=group)
    return pl.pallas_call(
        body,
        grid=(b, nb),
        in_specs=[pl.BlockSpec((blk, aq), lambda bi, n: (bi * nb + n, 0)),
                  kv_spec(kcol, prev), kv_spec(kcol, ident), kv_spec(kcol, nxt),
                  kv_spec(vcol, prev), kv_spec(vcol, ident), kv_spec(vcol, nxt),
                  pl.BlockSpec((l, akv), lambda bi, n: (bi, kcol)),
                  pl.BlockSpec((l, akv), lambda bi, n: (bi, vcol)),
                  tab_spec(prev), tab_spec(ident), tab_spec(nxt),
                  tab_spec(prev), tab_spec(ident), tab_spec(nxt),
                  pl.BlockSpec((KV_HEADS, group * blk, 1), lambda bi, n: (0, 0, 0))],
        out_specs=pl.BlockSpec((blk, aq), lambda bi, n: (bi * nb + n, 0)),
        out_shape=jax.ShapeDtypeStruct((b * t, aq), BF16),
        compiler_params=_params("parallel", "parallel"),
    )(px, px, px, px, px, px, px, pc, pc, cos_t, cos_t, cos_t, sin_t, sin_t, sin_t, sink_col)


def _ctx_attn_body(q_ref, k_ref, v_ref, sink_ref, o_ref, *, group):
    scale = HEAD_DIM ** -0.5
    for h in range(KV_HEADS):
        hs = slice(h * HEAD_DIM, (h + 1) * HEAD_DIM)
        kall = k_ref[:, hs].astype(BF16)
        vall = v_ref[:, hs].astype(BF16)
        for g in range(group):
            c0 = (h * group + g) * HEAD_DIM
            s = _dot_nt(q_ref[:, c0:c0 + HEAD_DIM].astype(BF16), kall) * scale
            o = _softmax_pv([s], sink_ref[h * group + g], vall)
            o_ref[:, c0:c0 + HEAD_DIM] = o.astype(o_ref.dtype)


def _context_attention(pc, sink_rows, b, l, aq, akv):
    group = aq // akv
    kcol, vcol = aq // akv, aq // akv + 1
    return pl.pallas_call(
        functools.partial(_ctx_attn_body, group=group),
        grid=(b,),
        in_specs=[pl.BlockSpec((l, aq), lambda bi: (bi, 0)),
                  pl.BlockSpec((l, akv), lambda bi: (bi, kcol)),
                  pl.BlockSpec((l, akv), lambda bi: (bi, vcol)),
                  pl.BlockSpec((KV_HEADS * group, l, 1), lambda bi: (0, 0, 0))],
        out_specs=pl.BlockSpec((l, aq), lambda bi: (bi, 0)),
        out_shape=jax.ShapeDtypeStruct((b * l, aq), BF16),
        compiler_params=_params("parallel"),
    )(pc, pc, pc, sink_rows)


def _cmlp_body(u_ref, gv_ref, g_ref, ws_ref, bs_ref, o_ref, *, groups):
    u = jax.nn.gelu(u_ref[...])
    gv = jax.nn.gelu(gv_ref[...])
    gvn = gv * lax.rsqrt(jnp.mean(gv * gv, axis=-1, keepdims=True) + EPS) * g_ref[...]
    for gi in range(groups):
        cs = slice(gi * CMLP_CH, (gi + 1) * CMLP_CH)
        mixed = _dot(ws_ref[gi].astype(BF16), gvn[:, cs].astype(BF16)) + bs_ref[gi]
        o_ref[:, cs] = (u[:, cs] * mixed).astype(o_ref.dtype)


def _chunk_mlp(p, norm_g, ws, bs_b, ucol, cw):
    n = p.shape[0]
    groups = cw // CMLP_CH
    ch = CMLP_CHUNK
    return pl.pallas_call(
        functools.partial(_cmlp_body, groups=groups),
        grid=(n // ch,),
        in_specs=[pl.BlockSpec((ch, cw), lambda i: (i, ucol)),
                  pl.BlockSpec((ch, cw), lambda i: (i, ucol + 1)),
                  pl.BlockSpec((1, cw), lambda i: (0, 0)),
                  pl.BlockSpec((groups, ch, ch), lambda i: (0, 0, 0)),
                  pl.BlockSpec((groups, ch, CMLP_CH), lambda i: (0, 0, 0))],
        out_specs=pl.BlockSpec((ch, cw), lambda i: (i, 0)),
        out_shape=jax.ShapeDtypeStruct((n, cw), BF16),
        compiler_params=_params("parallel"),
    )(p, p, norm_g, ws, bs_b)


def _head_sum(x):
    ones = (_iota((128, 128), 0) // RWKV_HEAD == _iota((128, 128), 1) // RWKV_HEAD).astype(F32)
    cols = [_dot(x[:, s * 128:(s + 1) * 128], ones, precision=HIGHEST) for s in range(x.shape[1] // 128)]
    return cols[0] if len(cols) == 1 else jnp.concatenate(cols, axis=1)


def _prep_body(r_ref, k_ref, v_ref, rp_ref, kp_ref, vp_ref, rn_ref, kn_ref, vn_ref,
               cr_ref, ck_ref, cv_ref, hw_ref, ha_ref, w2_ref, a2_ref, w0_ref, a0_ref, kkp_ref, kap_ref,
               ro_ref, ko_ref, vo_ref, kko_ref, lw_ref, bo_ref, kr_ref, *, tr, seq, lora):
    i = pl.program_id(0)
    first = (i * tr) % seq == 0
    last = ((i + 1) * tr) % seq == 0
    row = _iota(r_ref.shape, 0)

    def conv(x_ref, xp_ref, xn_ref, w_ref):
        x = x_ref[...]
        before = jnp.where(first, 0.0, xp_ref[7:8, :])
        after = jnp.where(last, 0.0, xn_ref[0:1, :])
        xm = jnp.where(row == 0, before, pltpu.roll(x, 1, axis=0))
        xp = jnp.where(row == tr - 1, after, pltpu.roll(x, tr - 1, axis=0))
        return xm * w_ref[0:1, :] + x * w_ref[1:2, :] + xp * w_ref[2:3, :]

    r = conv(r_ref, rp_ref, rn_ref, cr_ref)
    k = conv(k_ref, kp_ref, kn_ref, ck_ref)
    v = conv(v_ref, vp_ref, vn_ref, cv_ref)
    kk = k * kkp_ref[...]
    kk = kk * lax.rsqrt(_head_sum(kk * kk) + 1e-12)
    ro_ref[...] = r
    ko_ref[...] = k
    vo_ref[...] = v
    kko_ref[...] = kk
    for z in range(2):
        zs = slice(z * lora, (z + 1) * lora)
        w_raw = w0_ref[z:z + 1, :] + _dot(jnp.tanh(hw_ref[:, zs]).astype(BF16), w2_ref[z].astype(BF16))
        softplus_neg = jnp.maximum(-w_raw, 0.0) + jnp.log1p(jnp.exp(-jnp.abs(w_raw)))
        lw_ref[z] = -jnp.exp(-softplus_neg - 0.5)
        a = jax.nn.sigmoid(a0_ref[z:z + 1, :] + _dot(ha_ref[:, zs].astype(BF16), a2_ref[z].astype(BF16)))
        kr_ref[z] = k * (1.0 + (a - 1.0) * kap_ref[...])
        bo_ref[z] = kk * a


def _rwkv_prep(p, conv_w, w2, a2, w0, a0, kk_p, ka_p, seq, rcol, hcol, rw, lora):
    n = p.shape[0]
    tr = min(256, seq)
    nh = n // 8
    body = functools.partial(_prep_body, tr=tr, seq=seq, lora=lora)
    cur = lambda c: pl.BlockSpec((tr, rw), lambda i: (i, rcol + c))
    prv = lambda c: pl.BlockSpec((8, rw), lambda i: (jnp.maximum(i * (tr // 8) - 1, 0), rcol + c))
    nxt = lambda c: pl.BlockSpec((8, rw), lambda i: (jnp.minimum((i + 1) * (tr // 8), nh - 1), rcol + c))
    cw = lambda c: pl.BlockSpec((3, rw), lambda i: (0, c))
    full2 = lambda shape: pl.BlockSpec(shape, lambda i: (0,) * len(shape))
    shared = pl.BlockSpec((tr, rw), lambda i: (i, 0))
    directed = pl.BlockSpec((2, tr, rw), lambda i: (0, i, 0))
    return pl.pallas_call(
        body,
        grid=(n // tr,),
        in_specs=[cur(0), cur(1), cur(2), prv(0), prv(1), prv(2), nxt(0), nxt(1), nxt(2),
                  cw(0), cw(1), cw(2),
                  pl.BlockSpec((tr, 2 * lora), lambda i: (i, hcol)),
                  pl.BlockSpec((tr, 2 * lora), lambda i: (i, hcol + 1)),
                  full2((2, lora, rw)), full2((2, lora, rw)), full2((2, rw)), full2((2, rw)),
                  full2((1, rw)), full2((1, rw))],
        out_specs=[shared, shared, shared, shared, directed, directed, directed],
        out_shape=[jax.ShapeDtypeStruct((n, rw), F32)] * 4 + [jax.ShapeDtypeStruct((2, n, rw), F32)] * 3,
        compiler_params=_params("parallel"),
    )(p, p, p, p, p, p, p, p, p, conv_w, conv_w, conv_w, p, p, w2, a2, w0, a0, kk_p, ka_p)


def _mm(a, b, passes, kind="nn"):
    fn = {"nn": _dot, "nt": _dot_nt, "tn": _dot_tn}[kind]
    if passes == 6:
        return fn(a, b, precision=HIGHEST)
    a_hi, b_hi = a.astype(BF16), b.astype(BF16)
    if passes == 1:
        return fn(a_hi, b_hi)
    a_lo = (a - a_hi.astype(F32)).astype(BF16)
    b_lo = (b - b_hi.astype(F32)).astype(BF16)
    return fn(a_hi, b_hi) + (fn(a_lo, b_hi) + fn(a_hi, b_lo))


def _unit_tri_inverse(nmats, eye, same16, same32, passes):
    n16 = [jnp.where(same16, n, 0.0) for n in nmats]
    xs = [eye - n for n in n16]
    pw = n16
    for _ in range(3):
        pw = [_mm(p, p, passes) for p in pw]
        xs = [x + _mm(x, p, passes) for x, p in zip(xs, pw)]
    for mask in (same32 & ~same16, ~same32):
        offs = [jnp.where(mask, n, 0.0) for n in nmats]
        xo = [_mm(x, o, passes) for x, o in zip(xs, offs)]
        xs = [x - _mm(t, x, passes) for x, t in zip(xs, xo)]
    return xs


def _scan_body(r_ref, kk_ref, v_ref, lw_ref, b_ref, k_ref, s0_ref, y_ref, sf_ref, st_ref, *, heads, nchunks):
    z = pl.program_id(0)
    c = pl.program_id(3)
    C, K = SCAN_CHUNK, RWKV_HEAD
    pp = SCAN_PASSES

    @pl.when(c == 0)
    def _():
        st_ref[...] = s0_ref[...]

    ti, si = _iota((C, C), 0), _iota((C, C), 1)
    before = (si - ti) * (1 - 2 * z) < 0
    upto = before | (si == ti)
    eye = (si == ti).astype(F32)
    same16 = (ti // 16) == (si // 16)
    same32 = (ti // 32) == (si // 32)

    lw = lw_ref[...]
    lc = _dot(upto.astype(F32), lw, precision=HIGHEST)
    ltot = jnp.sum(lw, axis=0, keepdims=True)
    e_neg = jnp.exp(-lc)
    e_out = jnp.exp(ltot - lc)
    kkt = kk_ref[...] * jnp.exp(lc - lw)
    rt = r_ref[...] * jnp.exp(lc)
    bt = b_ref[...] * e_neg
    kt = k_ref[...] * e_neg
    bh = b_ref[...] * e_out
    kh = k_ref[...] * e_out
    etot = jnp.exp(ltot)
    v = v_ref[...]

    hr = range(heads)
    ls = [slice(h * K, (h + 1) * K) for h in hr]
    ps = [_mm(jnp.concatenate([kkt[:, s], rt[:, s]], axis=0),
              jnp.concatenate([bt[:, s], kt[:, s]], axis=0), pp["pair"], "nt") for s in ls]
    nmats = [jnp.where(before, p[:C, :C], 0.0) for p in ps]
    pkk = [jnp.where(before, p[:C, C:], 0.0) for p in ps]
    prb = [jnp.where(upto, p[C:, :C], 0.0) for p in ps]
    prk = [jnp.where(upto, p[C:, C:], 0.0) for p in ps]
    tinv = _unit_tri_inverse(nmats, eye, same16, same32, pp["inv"])
    tg = [_mm(tinv[h], jnp.concatenate([kkt[:, ls[h]], pkk[h]], axis=1), pp["solve"]) for h in hr]
    qa = [jnp.concatenate([rt[:, ls[h]], prk[h]], axis=1) - _mm(prb[h], tg[h], pp["solve"]) for h in hr]
    m3 = [_mm(bh[:, ls[h]], tg[h], pp["solve"], "tn") for h in hr]
    sv = [jnp.concatenate([st_ref[h], v[:, ls[h]]], axis=0) for h in hr]
    ys = [_mm(qa[h], sv[h], pp["state"]) for h in hr]
    for h in hr:
        decay_diag = eye * jnp.broadcast_to(etot[:, ls[h]], (K, K))
        trans = jnp.concatenate([decay_diag, jnp.zeros((K, C), F32)], axis=1) - m3[h]
        st_ref[h] = _mm(trans, sv[h], pp["state"]) + _mm(kh[:, ls[h]], v[:, ls[h]], pp["state"], "tn")
    y_ref[...] = ys[0] if heads == 1 else jnp.concatenate(ys, axis=1)

    @pl.when(c == nchunks - 1)
    def _():
        sf_ref[...] = st_ref[...]


def _rwkv_scan(r, kk, v, lw, bb, kr, s0, b, seq, rw):
    C, K = SCAN_CHUNK, RWKV_HEAD
    nchunks = seq // C
    heads = min(SCAN_HEADS, rw // K)
    ngroups = rw // (heads * K)
    n = b * seq
    row = lambda z, bi, hg, c: bi * nchunks + c + z * (nchunks - 1 - 2 * c)
    shared = pl.BlockSpec((C, heads * K), lambda z, bi, hg, c: (row(z, bi, hg, c), hg))
    directed = pl.BlockSpec((None, C, heads * K), lambda z, bi, hg, c: (z, row(z, bi, hg, c), hg))
    state = pl.BlockSpec((None, None, heads, K, K), lambda z, bi, hg, c: (z, bi, hg, 0, 0))
    body = functools.partial(_scan_body, heads=heads, nchunks=nchunks)
    return pl.pallas_call(
        body,
        grid=(2, b, ngroups, nchunks),
        in_specs=[shared, shared, shared, directed, directed, directed, state],
        out_specs=[directed, state],
        out_shape=[jax.ShapeDtypeStruct((2, n, rw), F32),
                   jax.ShapeDtypeStruct((2, b, rw // K, K, K), F32)],
        scratch_shapes=[pltpu.VMEM((heads, K, K), F32)],
        compiler_params=_params("parallel", "parallel", "parallel", "arbitrary"),
    )(r, kk, v, lw, bb, kr, s0)


def _rwkv_out_body(yf_ref, yb_ref, r_ref, k_ref, v_ref, g_ref, rk_ref, lnw_ref, lnb_ref, o_ref):
    y = yf_ref[...] + yb_ref[...]
    inv = 1.0 / RWKV_HEAD
    mu = _head_sum(y) * inv
    d = y - mu
    var = _head_sum(d * d) * inv
    yn = d * lax.rsqrt(var + RWKV_GN_EPS) * lnw_ref[...] + lnb_ref[...]
    bonus = _head_sum(r_ref[...] * k_ref[...] * rk_ref[...]) * v_ref[...]
    o_ref[...] = ((yn + bonus) * jax.nn.sigmoid(g_ref[...])).astype(o_ref.dtype)


def _rwkv_output(y, r, k, v, p, gcol, rk, ln_w, ln_b):
    n, rw = r.shape
    tr = min(256, n)
    shared = pl.BlockSpec((tr, rw), lambda i: (i, 0))
    vec = pl.BlockSpec((1, rw), lambda i: (0, 0))
    return pl.pallas_call(
        _rwkv_out_body,
        grid=(n // tr,),
        in_specs=[pl.BlockSpec((None, tr, rw), lambda i: (0, i, 0)),
                  pl.BlockSpec((None, tr, rw), lambda i: (1, i, 0)),
                  shared, shared, shared,
                  pl.BlockSpec((tr, rw), lambda i: (i, gcol)),
                  vec, vec, vec],
        out_specs=shared,
        out_shape=jax.ShapeDtypeStruct((n, rw), BF16),
        compiler_params=_params("parallel"),
    )(y, y, r, k, v, p, rk, ln_w, ln_b)


def _wout_body(a_ref, c_ref, r_ref, wa_ref, wc_ref, wr_ref, x_ref, g_ref, o_ref, *, tm, rows_per_mod, fixed_row):
    row = _mod_row(pl.program_id(0), tm, rows_per_mod, fixed_row)
    acc = _dot(a_ref[...], wa_ref[...]) + _dot(c_ref[...], wc_ref[...]) + _dot(r_ref[...], wr_ref[...])
    o_ref[...] = x_ref[...] + g_ref[pl.ds(row, 1), :] * acc


def _out_proj(attn, cmlp, rwkv, w_out, x2d, mod, layer, rows_per_mod, fixed_row):
    n, d = x2d.shape
    aq, cw, rw = attn.shape[1], cmlp.shape[1], rwkv.shape[1]
    tm = min(512, n)
    tn = w_out.shape[3]
    body = functools.partial(_wout_body, tm=tm, rows_per_mod=rows_per_mod, fixed_row=fixed_row)
    return pl.pallas_call(
        body,
        grid=(n // tm, d // tn),
        in_specs=[pl.BlockSpec((tm, aq), lambda i, j: (i, 0)),
                  pl.BlockSpec((tm, cw), lambda i, j: (i, 0)),
                  pl.BlockSpec((tm, rw), lambda i, j: (i, 0)),
                  pl.BlockSpec((None, None, aq, tn), lambda i, j: (layer, j, 0, 0)),
                  pl.BlockSpec((None, None, cw, tn), lambda i, j: (layer, j, aq // cw, 0)),
                  pl.BlockSpec((None, None, rw, tn), lambda i, j: (layer, j, (aq + cw) // rw, 0)),
                  pl.BlockSpec((tm, tn), lambda i, j: (i, j)),
                  pl.BlockSpec((None, MOD_ROWS, tn), lambda i, j: (layer, 0, 2 * (d // tn) + j))],
        out_specs=pl.BlockSpec((tm, tn), lambda i, j: (i, j)),
        out_shape=jax.ShapeDtypeStruct((n, d), F32),
        compiler_params=_params("parallel", "parallel"),
    )(attn, cmlp, rwkv, w_out, w_out, w_out, x2d, mod)


def _router_body(x_ref, g_ref, sh_ref, sc_ref, rw_ref, rb_ref, zn_ref, ids_ref, wt_ref, *,
                 tm, rows_per_mod, fixed_row, experts):
    row = _mod_row(pl.program_id(0), tm, rows_per_mod, fixed_row)
    zn = _modulated_norm(x_ref[...], g_ref[...], sh_ref[pl.ds(row, 1), :], sc_ref[pl.ds(row, 1), :])
    zn_ref[...] = zn
    logits = _dot(zn, rw_ref[...], precision=HIGHEST).T
    per_group = experts // N_EXPERT_GROUPS
    scores = [jax.nn.sigmoid(logits[e:e + 1, :]) for e in range(experts)]
    sel = [scores[e] + rb_ref[e:e + 1, :] for e in range(experts)]
    best_val, best_grp = None, None
    for gi in range(N_EXPERT_GROUPS):
        mem = sel[gi * per_group:(gi + 1) * per_group]
        top2 = None
        for a in range(per_group):
            for b2 in range(a + 1, per_group):
                pair = mem[a] + mem[b2]
                top2 = pair if top2 is None else jnp.maximum(top2, pair)
        if gi == 0:
            best_val, best_grp = top2, jnp.zeros(top2.shape, jnp.int32)
        else:
            better = top2 > best_val
            best_grp = jnp.where(better, gi, best_grp)
            best_val = jnp.where(better, top2, best_val)
    chosen, picked = [], []
    for e in range(experts):
        gi = e // per_group
        rank = jnp.zeros(best_grp.shape, jnp.int32)
        for j in range(gi * per_group, (gi + 1) * per_group):
            if j != e:
                ahead = (sel[j] > sel[e]) | ((sel[j] == sel[e]) & (j < e))
                rank = rank + ahead.astype(jnp.int32)
        chosen.append((best_grp == gi) & (rank < TOP_K))
        picked.append(jnp.where(chosen[e], scores[e], 0.0))
    total = picked[0]
    for e in range(1, experts):
        total = total + picked[e]
    zero_i, zero_f = jnp.zeros(total.shape, jnp.int32), jnp.zeros(total.shape, F32)
    seen, ids, wts = zero_i, [zero_i, zero_i], [zero_f, zero_f]
    for e in range(experts):
        gate = picked[e] / total
        for slot in range(TOP_K):
            here = chosen[e] & (seen == slot)
            ids[slot] = jnp.where(here, e, ids[slot])
            wts[slot] = jnp.where(here, gate, wts[slot])
        seen = seen + chosen[e].astype(jnp.int32)
    ids_ref[...] = jnp.concatenate(ids + [jnp.zeros((8 - TOP_K, tm), jnp.int32)], axis=0)
    wt_ref[...] = jnp.concatenate(wts + [jnp.zeros((128 - TOP_K, tm), F32)], axis=0).T


def _router(x2d, g, mod, router_w_pad, router_b_col, layer, rows_per_mod, fixed_row, experts):
    n, d = x2d.shape
    tm = min(256, n)
    body = functools.partial(_router_body, tm=tm, rows_per_mod=rows_per_mod, fixed_row=fixed_row, experts=experts)
    return pl.pallas_call(
        body,
        grid=(n // tm,),
        in_specs=[pl.BlockSpec((tm, d), lambda i: (i, 0)),
                  pl.BlockSpec((None, 1, d), lambda i: (layer, 0, 0)),
                  pl.BlockSpec((None, MOD_ROWS, d), lambda i: (layer, 0, 3)),
                  pl.BlockSpec((None, MOD_ROWS, d), lambda i: (layer, 0, 4)),
                  pl.BlockSpec((d, 128), lambda i: (0, 0)),
                  pl.BlockSpec((128, 1), lambda i: (0, 0))],
        out_specs=[pl.BlockSpec((tm, d), lambda i: (i, 0)),
                   pl.BlockSpec((8, tm), lambda i: (0, i)),
                   pl.BlockSpec((tm, 128), lambda i: (i, 0))],
        out_shape=[jax.ShapeDtypeStruct((n, d), F32), jax.ShapeDtypeStruct((8, n), jnp.int32),
                   jax.ShapeDtypeStruct((n, 128), F32)],
        compiler_params=_params("parallel"),
    )(x2d, g, mod, mod, router_w_pad, router_b_col)


def _route_plan(ids, experts, tm):
    n = ids.shape[1]
    total = TOP_K * n + experts * tm
    flat = ids.reshape(-1)
    onehot = (flat[:, None] == jnp.arange(experts, dtype=jnp.int32)[None, :]).astype(jnp.int32)
    rank = jnp.cumsum(onehot, axis=0) - onehot
    padded = ((jnp.sum(onehot, axis=0) + tm - 1) // tm) * tm
    ends = jnp.cumsum(padded)
    pos = (ends - padded)[flat] + jnp.sum(rank * onehot, axis=1)
    row_token = jnp.zeros((total,), jnp.int32).at[pos].set(jnp.tile(jnp.arange(n, dtype=jnp.int32), TOP_K))
    tile_start = jnp.arange(total // tm, dtype=jnp.int32) * tm
    tile_used = (tile_start < ends[-1]).astype(jnp.int32)
    tile_expert = jnp.minimum(jnp.searchsorted(ends, tile_start, side="right"), experts - 1).astype(jnp.int32)
    last_used = tile_expert[jnp.maximum(ends[-1] // tm - 1, 0)]
    tile_expert = jnp.where(tile_used == 1, tile_expert, last_used)
    return pos.astype(jnp.int32), row_token, tile_expert, tile_used


def _row_copy(src_hbm, row, dst_ref, i, sem):
    return pltpu.make_async_copy(src_hbm.at[pl.ds(row, 1), :], dst_ref.at[pl.ds(i, 1), :], sem)


def _start_rows(src_hbm, dst_ref, sem, index_of, priorities):
    k = len(priorities)

    def start(i, carry):
        for j, prio in enumerate(priorities):
            _row_copy(src_hbm, index_of(i * k + j), dst_ref, i * k + j, sem).start(priority=prio)
        return carry

    lax.fori_loop(0, dst_ref.shape[0] // k, start, 0, unroll=8 // k)


def _wait_rows(src_hbm, dst_ref, sem):
    pltpu.make_async_copy(src_hbm.at[pl.ds(0, dst_ref.shape[0]), :], dst_ref, sem).wait()


def _expert_hidden_body(te_ref, used_ref, tok_ref, zn_hbm, w1_ref, w3_ref, h_ref, xg_ref, h1_ref, h3_ref, sem, *,
                        tm, tk, n_k, ntiles):
    p, k = pl.program_id(0), pl.program_id(1)
    used = used_ref[p] == 1
    slot = p % 2
    rows_of = lambda tile: (lambda i: tok_ref[tile * tm + i])

    @pl.when(used & (k == 0) & (p == 0))
    def _():
        _start_rows(zn_hbm, xg_ref.at[0], sem.at[0], rows_of(0), GATHER_PRIORITY)

    @pl.when(used & (k == 0))
    def _():
        _wait_rows(zn_hbm, xg_ref.at[slot], sem.at[slot])

        @pl.when((p + 1 < ntiles) & (used_ref[jnp.minimum(p + 1, ntiles - 1)] == 1))
        def _():
            _start_rows(zn_hbm, xg_ref.at[1 - slot], sem.at[1 - slot], rows_of(p + 1), GATHER_PRIORITY)

    @pl.when(used)
    def _():
        x = xg_ref[slot, :, pl.ds(pl.multiple_of(k * tk, tk), tk)].astype(BF16)
        d1 = _dot(x, w1_ref[...].astype(BF16))
        d3 = _dot(x, w3_ref[...].astype(BF16))

        @pl.when(k == 0)
        def _():
            h1_ref[...] = d1
            h3_ref[...] = d3

        @pl.when(k > 0)
        def _():
            h1_ref[...] += d1
            h3_ref[...] += d3

    @pl.when(k == n_k - 1)
    def _():
        h1 = h1_ref[...]
        hid = (h1 * jax.nn.sigmoid(h1)) * h3_ref[...]
        h_ref[...] = jnp.where(used, hid, 0.0).astype(h_ref.dtype)


def _expert_out_body(te_ref, used_ref, h_ref, w2_ref, o_ref):
    o_ref[...] = _dot(h_ref[...], w2_ref[...].astype(BF16))


def _experts(zn, row_token, tile_expert, tile_used, w1, w3, w2, layer, tm):
    n, d = zn.shape
    ff = w1.shape[3]
    total = row_token.shape[0]
    ntiles = total // tm
    tk = min(1024, d)
    n_k = d // tk
    tn = min(2048, d)
    n_j = d // tn
    hold = lambda used, p, j, last: jnp.where(used[p] == 1, j, last)
    w_spec = pl.BlockSpec((None, None, tk, ff),
                          lambda p, k, te, us, tok: (layer, te[p], hold(us, p, k, n_k - 1), 0))
    hidden = pl.pallas_call(
        functools.partial(_expert_hidden_body, tm=tm, tk=tk, n_k=n_k, ntiles=ntiles),
        grid_spec=pltpu.PrefetchScalarGridSpec(
            num_scalar_prefetch=3,
            grid=(ntiles, n_k),
            in_specs=[pl.BlockSpec(memory_space=pl.ANY), w_spec, w_spec],
            out_specs=pl.BlockSpec((tm, ff), lambda p, k, te, us, tok: (p, 0)),
            scratch_shapes=[pltpu.VMEM((2, tm, d), F32), pltpu.VMEM((tm, ff), F32), pltpu.VMEM((tm, ff), F32),
                            pltpu.SemaphoreType.DMA((2,))]),
        out_shape=jax.ShapeDtypeStruct((total, ff), BF16),
        compiler_params=_params("arbitrary", "arbitrary", disable_bounds_checks=True),
    )(tile_expert, tile_used, row_token, zn, w1, w3)
    return pl.pallas_call(
        _expert_out_body,
        grid_spec=pltpu.PrefetchScalarGridSpec(
            num_scalar_prefetch=2,
            grid=(n_j, ntiles),
            in_specs=[pl.BlockSpec((tm, ff), lambda j, p, te, us: (p, 0)),
                      pl.BlockSpec((None, None, ff, tn), lambda j, p, te, us: (layer, te[p], 0, j))],
            out_specs=pl.BlockSpec((tm, tn), lambda j, p, te, us: (p, j))),
        out_shape=jax.ShapeDtypeStruct((total, d), F32),
        compiler_params=_params("parallel", "parallel"),
    )(tile_expert, tile_used, hidden, w2)


def _combine_body(pos_ref, ys_hbm, wt_ref, x_ref, g2_ref, fg_ref, o_ref, ya_ref, yb_ref, sem_a, sem_b, *,
                  tm, n, rows_per_mod, fixed_row, final_norm):
    i = pl.program_id(0)
    slot = i % 2

    def start(tile, s):
        _start_rows(ys_hbm, ya_ref.at[s], sem_a.at[s], lambda r: pos_ref[tile * tm + r], COMBINE_PRIORITY)
        _start_rows(ys_hbm, yb_ref.at[s], sem_b.at[s], lambda r: pos_ref[n + tile * tm + r], COMBINE_PRIORITY)

    @pl.when(i == 0)
    def _():
        start(0, 0)

    @pl.when(i + 1 < n // tm)
    def _():
        start(i + 1, 1 - slot)

    _wait_rows(ys_hbm, ya_ref.at[slot], sem_a.at[slot])
    _wait_rows(ys_hbm, yb_ref.at[slot], sem_b.at[slot])
    row = _mod_row(i, tm, rows_per_mod, fixed_row)
    wt = wt_ref[...]
    mix = wt[:, 0:1] * ya_ref[slot] + wt[:, 1:2] * yb_ref[slot]
    out = x_ref[...] + g2_ref[pl.ds(row, 1), :] * mix
    if final_norm:
        out = out * lax.rsqrt(jnp.mean(out * out, axis=-1, keepdims=True) + EPS) * fg_ref[...]
    o_ref[...] = out


def _moe_combine(ys, pos, wts, x2d, mod, final_g, layer, rows_per_mod, fixed_row, final_norm):
    n, d = x2d.shape
    tm = min(256, n)
    body = functools.partial(_combine_body, tm=tm, n=n, rows_per_mod=rows_per_mod, fixed_row=fixed_row,
                             final_norm=final_norm)
    return pl.pallas_call(
        body,
        grid_spec=pltpu.PrefetchScalarGridSpec(
            num_scalar_prefetch=1,
            grid=(n // tm,),
            in_specs=[pl.BlockSpec(memory_space=pl.ANY),
                      pl.BlockSpec((tm, 128), lambda i, ps: (i, 0)),
                      pl.BlockSpec((tm, d), lambda i, ps: (i, 0)),
                      pl.BlockSpec((None, MOD_ROWS, d), lambda i, ps: (layer, 0, 5)),
                      pl.BlockSpec((1, d), lambda i, ps: (0, 0))],
            out_specs=pl.BlockSpec((tm, d), lambda i, ps: (i, 0)),
            scratch_shapes=[pltpu.VMEM((2, tm, d), F32), pltpu.VMEM((2, tm, d), F32),
                            pltpu.SemaphoreType.DMA((2,)), pltpu.SemaphoreType.DMA((2,))]),
        out_shape=jax.ShapeDtypeStruct((n, d), F32),
        compiler_params=_params("arbitrary", disable_bounds_checks=True),
    )(pos, ys, wts, x2d, mod, final_g)


def _moe(x2d, g, mod, router_w_pad, router_b_col, w1, w3, w2, final_g, layer, experts,
         rows_per_mod, fixed_row, final_norm=False):
    n = x2d.shape[0]
    tm = 512 if n >= 8 * 512 else 128
    zn, ids, wts = _router(x2d, g, mod, router_w_pad, router_b_col, layer, rows_per_mod, fixed_row, experts)
    pos, row_token, tile_expert, tile_used = _route_plan(ids[:TOP_K], experts, tm)
    ys = _experts(zn, row_token, tile_expert, tile_used, w1, w3, w2, layer, tm)
    return _moe_combine(ys, pos, wts, x2d, mod, final_g, layer, rows_per_mod, fixed_row, final_norm)


def _rope_tables(t):
    pos = jnp.arange(t)
    half = HEAD_DIM // 4
    freqs = ROPE_BASE ** (-jnp.arange(half, dtype=F32) / half)
    ang_r = (pos // GRID_W).astype(F32)[:, None] * freqs[None, :]
    ang_c = (pos % GRID_W).astype(F32)[:, None] * freqs[None, :]
    cos_t = jnp.concatenate([jnp.cos(ang_r)] * 2 + [jnp.cos(ang_c)] * 2, axis=1)
    sin_t = jnp.concatenate([-jnp.sin(ang_r), jnp.sin(ang_r), -jnp.sin(ang_c), jnp.sin(ang_c)], axis=1)
    return cos_t, sin_t


def kernel(x, c, ctx, c_ctx, ada_w, ada_b, norm1_g, w_in, rwkv_conv, attn_sink, cmlp_norm_g, cmlp_ws, cmlp_b,
           rwkv_w0, rwkv_w1, rwkv_w2, rwkv_a0, rwkv_a1, rwkv_a2, rwkv_kk, rwkv_ka, rwkv_rk, rwkv_ln_w, rwkv_ln_b,
           w_out, norm2_g, router_w, router_b, moe_w1, moe_w3, moe_w2, final_g):
    b, t, d = x.shape
    l = ctx.shape[1]
    depth = ada_w.shape[0]
    cw = cmlp_norm_g.shape[1]
    rw = rwkv_w0.shape[2]
    lora = rwkv_w1.shape[3]
    experts = router_w.shape[1]
    dp = w_in.shape[2]
    akv = KV_HEADS * HEAD_DIM
    aq = dp - 2 * akv - 2 * cw - 4 * rw
    group = aq // akv
    heads = rw // RWKV_HEAD
    ucol = (aq + 2 * akv) // cw
    rcol = (aq + 2 * akv + 2 * cw) // rw
    gcol = rcol + 3
    hcol = dp // (2 * lora)
    assert (aq + 2 * akv) % cw == 0 and (aq + 2 * akv + 2 * cw) % rw == 0 and dp % (2 * lora) == 0
    assert b + 1 <= MOD_ROWS and (b * t) % l == 0 and t % 256 == 0 and l % 128 == 0

    cpad = jnp.zeros((MOD_ROWS, d), F32).at[:b].set(c).at[b].set(c_ctx)
    mod = _ada(cpad, ada_w, ada_b)
    cos_t, sin_t = _rope_tables(t)
    router_w_pad = jnp.zeros((d, 128), F32).at[:, :experts].set(router_w)
    router_b_col = jnp.zeros((128, 1), F32).at[:experts, 0].set(router_b)
    w_lora_bf = jnp.concatenate([rwkv_w1[:, 0], rwkv_w1[:, 1], rwkv_a1[:, 0], rwkv_a1[:, 1]], axis=2).astype(BF16)
    tn_in, tn_out = w_lora_bf.shape[2], min(1024, d)
    assert dp % tn_in == 0
    w_in_bf = w_in.astype(BF16).reshape(depth, d, dp // tn_in, tn_in).transpose(0, 2, 1, 3)
    w_out_bf = w_out.astype(BF16).reshape(depth, w_out.shape[1], d // tn_out, tn_out).transpose(0, 2, 1, 3)
    s_zero = jnp.zeros((2, b, heads, RWKV_HEAD, RWKV_HEAD), F32)

    xs = x.reshape(b * t, d)
    hs = ctx.reshape(b * l, d)
    for layer in range(depth):
        lat = dict(rows_per_mod=t, fixed_row=None)
        con = dict(rows_per_mod=None, fixed_row=b)
        last = layer == depth - 1
        px = _proj(xs, norm1_g.reshape(depth, 1, d), mod, w_in_bf, w_lora_bf, layer, **lat)
        pc = _proj(hs, norm1_g.reshape(depth, 1, d), mod, w_in_bf, w_lora_bf, layer, **con)

        sink = attn_sink[layer]
        sink_col = jnp.repeat(sink.reshape(KV_HEADS, group), ATTN_BLOCK, axis=1)[..., None]
        attn_x = _latent_attention(px, pc, cos_t, sin_t, sink_col, b, t, l, aq, akv)

        bs_b = jnp.broadcast_to(cmlp_b[layer][:, :, None], cmlp_b.shape[1:] + (CMLP_CH,))
        cmlp_x = _chunk_mlp(px, cmlp_norm_g[layer][None], cmlp_ws[layer], bs_b, ucol, cw)

        prep_args = (rwkv_conv[layer], rwkv_w2[layer], rwkv_a2[layer], rwkv_w0[layer], rwkv_a0[layer],
                     rwkv_kk[layer][None], rwkv_ka[layer][None])
        rc, kc, vc, kkc, lwc, bbc, krc = _rwkv_prep(pc, *prep_args, l, rcol, hcol, rw, lora)
        rx, kx, vx, kkx, lwx, bbx, krx = _rwkv_prep(px, *prep_args, t, rcol, hcol, rw, lora)
        y_c, s_ctx = _rwkv_scan(rc, kkc, vc, lwc, bbc, krc, s_zero, b, l, rw)
        y_x, _ = _rwkv_scan(rx, kkx, vx, lwx, bbx, krx, s_ctx, b, t, rw)
        out_args = (rwkv_rk[layer][None], rwkv_ln_w[layer][None], rwkv_ln_b[layer][None])
        rwkv_x = _rwkv_output(y_x, rx, kx, vx, px, gcol, *out_args)

        xs = _out_proj(attn_x, cmlp_x, rwkv_x, w_out_bf, xs, mod, layer, **lat)
        moe_args = (norm2_g.reshape(depth, 1, d), mod, router_w_pad, router_b_col, moe_w1, moe_w3, moe_w2,
                    final_g[None], layer, experts)
        xs = _moe(xs, *moe_args, final_norm=last, **lat)

        if not last:
            sink_rows = jnp.broadcast_to(sink[:, None, None], (KV_HEADS * group, l, 1))
            attn_c = _context_attention(pc, sink_rows, b, l, aq, akv)
            cmlp_c = _chunk_mlp(pc, cmlp_norm_g[layer][None], cmlp_ws[layer], bs_b, ucol, cw)
            rwkv_c = _rwkv_output(y_c, rc, kc, vc, pc, gcol, *out_args)
            hs = _out_proj(attn_c, cmlp_c, rwkv_c, w_out_bf, hs, mod, layer, **con)
            hs = _moe(hs, *moe_args, **con)
    return xs.reshape(b, t, d)
```

```python
import functools

import jax
import jax.numpy as jnp
from jax import lax
from jax.experimental import pallas as pl
from jax.experimental.pallas import tpu as pltpu

F32, BF16 = jnp.float32, jnp.bfloat16
HIGHEST = lax.Precision.HIGHEST

HEAD_DIM = 128
KV_HEADS = 4
WINDOW = 128
ATTN_BLOCK = 128
GRID_W = 64
ROPE_BASE = 10000.0
CMLP_CH = 128
CMLP_CHUNK = 128
RWKV_HEAD = 64
RWKV_GN_EPS = 64e-5
N_EXPERT_GROUPS = 4
TOP_K = 2
N_MOD = 6
EPS = 1e-6
MASKED = -1e30

MOD_ROWS = 8
SCAN_CHUNK = 64
SCAN_HEADS = 16
SCAN_PASSES = {"pair": 1, "inv": 1, "solve": 1, "state": 1}
GATHER_PRIORITY = (1,)
COMBINE_PRIORITY = (0, 1)
VMEM_LIMIT_BYTES = 56 * 1024 * 1024


def _params(*sem, **kw):
    return pltpu.CompilerParams(dimension_semantics=sem, vmem_limit_bytes=VMEM_LIMIT_BYTES, **kw)


def _dot(a, b, **kw):
    return jnp.dot(a, b, preferred_element_type=F32, **kw)


def _dot_nt(a, b, **kw):
    return lax.dot_general(a, b, (((1,), (1,)), ((), ())), preferred_element_type=F32, **kw)


def _dot_tn(a, b, **kw):
    return lax.dot_general(a, b, (((0,), (0,)), ((), ())), preferred_element_type=F32, **kw)


def _iota(shape, dim):
    return lax.broadcasted_iota(jnp.int32, shape, dim)


def _ada_body(c_ref, w_ref, b_ref, o_ref):
    c = c_ref[...]
    a = (c * jax.nn.sigmoid(c)).astype(BF16)
    o_ref[...] = _dot(a, w_ref[...].astype(BF16)) + b_ref[...]


def _ada(cpad, ada_w, ada_b):
    depth, d, n = ada_w.shape
    tn = 512
    return pl.pallas_call(
        _ada_body,
        grid=(depth, n // tn),
        in_specs=[pl.BlockSpec((MOD_ROWS, d), lambda l, j: (0, 0)),
                  pl.BlockSpec((None, d, tn), lambda l, j: (l, 0, j)),
                  pl.BlockSpec((None, 1, tn), lambda l, j: (l, 0, j))],
        out_specs=pl.BlockSpec((None, MOD_ROWS, tn), lambda l, j: (l, 0, j)),
        out_shape=jax.ShapeDtypeStruct((depth, MOD_ROWS, n), F32),
        compiler_params=_params("parallel", "parallel"),
    )(cpad, ada_w, ada_b.reshape(depth, 1, n))


def _mod_row(i, tm, rows_per_mod, fixed_row):
    return fixed_row if rows_per_mod is None else (i * tm) // rows_per_mod


def _modulated_norm(x, g, shift, scale):
    y = x * lax.rsqrt(jnp.mean(x * x, axis=-1, keepdims=True) + EPS) * g
    return y * (1.0 + scale) + shift


def _proj_body(x_ref, g_ref, sh_ref, sc_ref, w_ref, wl_ref, o_ref, xn_ref, *, tm, n_main, rows_per_mod, fixed_row):
    j = pl.program_id(1)

    @pl.when(j == 0)
    def _():
        r = _mod_row(pl.program_id(0), tm, rows_per_mod, fixed_row)
        xn = _modulated_norm(x_ref[...], g_ref[...], sh_ref[pl.ds(r, 1), :], sc_ref[pl.ds(r, 1), :])
        xn_ref[...] = xn.astype(BF16)

    @pl.when(j < n_main)
    def _():
        o_ref[...] = _dot(xn_ref[...], w_ref[...])

    @pl.when(j >= n_main)
    def _():
        o_ref[...] = _dot(xn_ref[...], wl_ref[...])


def _proj(x2d, g, mod, w_in, w_lora, layer, rows_per_mod, fixed_row):
    n, d = x2d.shape
    dp, tn = w_in.shape[2], w_lora.shape[2]
    tm = min(512, n)
    n_main = dp // tn
    body = functools.partial(_proj_body, tm=tm, n_main=n_main, rows_per_mod=rows_per_mod, fixed_row=fixed_row)
    return pl.pallas_call(
        body,
        grid=(n // tm, n_main + 1),
        in_specs=[pl.BlockSpec((tm, d), lambda i, j: (i, 0)),
                  pl.BlockSpec((None, 1, d), lambda i, j: (layer, 0, 0)),
                  pl.BlockSpec((None, MOD_ROWS, d), lambda i, j: (layer, 0, 0)),
                  pl.BlockSpec((None, MOD_ROWS, d), lambda i, j: (layer, 0, 1)),
                  pl.BlockSpec((None, d, tn), lambda i, j: (layer, 0, jnp.minimum(j, n_main - 1))),
                  pl.BlockSpec((None, d, tn), lambda i, j: (layer, 0, 0))],
        out_specs=pl.BlockSpec((tm, tn), lambda i, j: (i, j)),
        out_shape=jax.ShapeDtypeStruct((n, dp + tn), F32),
        scratch_shapes=[pltpu.VMEM((tm, d), BF16)],
        compiler_params=_params("parallel", "arbitrary"),
    )(x2d, g, mod, mod, w_in, w_lora)


def _rope(x, cos, sin_signed):
    lane = _iota(x.shape, 1)
    swapped = jnp.where((lane % 64) < 32, pltpu.roll(x, 96, axis=1), pltpu.roll(x, 32, axis=1))
    return x * cos + swapped * sin_signed


def _softmax_pv(parts, sink_col, vall):
    m = sink_col
    for s in parts:
        m = jnp.maximum(m, jnp.max(s, axis=-1, keepdims=True))
    ps = [jnp.exp(s - m) for s in parts]
    denom = jnp.exp(sink_col - m)
    for p in ps:
        denom = denom + jnp.sum(p, axis=-1, keepdims=True)
    p = ps[0] if len(ps) == 1 else jnp.concatenate(ps, axis=1)
    return _dot(p.astype(BF16), vall) / denom


def _attn_body(q_ref, kp_ref, kc_ref, kn_ref, vp_ref, vc_ref, vn_ref, kx_ref, vx_ref,
               cp_ref, cc_ref, cn_ref, sp_ref, sc_ref, sn_ref, sink_ref, o_ref, *, nb, group):
    n = pl.program_id(1)
    blk = ATTN_BLOCK
    cos = (cp_ref[...], cc_ref[...], cn_ref[...])
    sin = (sp_ref[...], sc_ref[...], sn_ref[...])
    qi = _iota((group * blk, 3 * blk), 0) % blk
    kj = _iota((group * blk, 3 * blk), 1)
    in_seq = ((kj >= blk) | (n > 0)) & ((kj < 2 * blk) | (n < nb - 1))
    band_ok = (jnp.abs(kj - blk - qi) <= WINDOW) & in_seq
    scale = HEAD_DIM ** -0.5
    for h in range(KV_HEADS):
        hs = slice(h * HEAD_DIM, (h + 1) * HEAD_DIM)
        kb = [_rope(r[:, hs], c, s) for r, c, s in zip((kp_ref, kc_ref, kn_ref), cos, sin)]
        kall = jnp.concatenate(kb + [kx_ref[:, hs]], axis=0).astype(BF16)
        vall = jnp.concatenate([vp_ref[:, hs], vc_ref[:, hs], vn_ref[:, hs], vx_ref[:, hs]], axis=0).astype(BF16)
        qs = []
        for g in range(group):
            c0 = (h * group + g) * HEAD_DIM
            qs.append(_rope(q_ref[:, c0:c0 + HEAD_DIM], cos[1], sin[1]))
        qh = jnp.concatenate(qs, axis=0).astype(BF16)
        s = _dot_nt(qh, kall) * scale
        s_loc = jnp.where(band_ok, s[:, :3 * blk], MASKED)
        o = _softmax_pv([s_loc, s[:, 3 * blk:]], sink_ref[h], vall)
        for g in range(group):
            c0 = (h * group + g) * HEAD_DIM
            o_ref[:, c0:c0 + HEAD_DIM] = o[g * blk:(g + 1) * blk].astype(o_ref.dtype)


def _latent_attention(px, pc, cos_t, sin_t, sink_col, b, t, l, aq, akv):
    blk = ATTN_BLOCK
    nb = t // blk
    group = aq // akv
    kcol, vcol = aq // akv, aq // akv + 1
    prev = lambda n: jnp.maximum(n - 1, 0)
    nxt = lambda n: jnp.minimum(n + 1, nb - 1)
    kv_spec = lambda col, f: pl.BlockSpec((blk, akv), lambda bi, n: (bi * nb + f(n), col))
    tab_spec = lambda f: pl.BlockSpec((blk, HEAD_DIM), lambda bi, n: (f(n), 0))
    ident = lambda n: n
    body = functools.partial(_attn_body, nb=nb, group=group)
    return pl.pallas_call(
        body,
        grid=(b, nb),
        in_specs=[pl.BlockSpec((blk, aq), lambda bi, n: (bi * nb + n, 0)),
                  kv_spec(kcol, prev), kv_spec(kcol, ident), kv_spec(kcol, nxt),
                  kv_spec(vcol, prev), kv_spec(vcol, ident), kv_spec(vcol, nxt),
                  pl.BlockSpec((l, akv), lambda bi, n: (bi, kcol)),
                  pl.BlockSpec((l, akv), lambda bi, n: (bi, vcol)),
                  tab_spec(prev), tab_spec(ident), tab_spec(nxt),
                  tab_spec(prev), tab_spec(ident), tab_spec(nxt),
                  pl.BlockSpec((KV_HEADS, group * blk, 1), lambda bi, n: (0, 0, 0))],
        out_specs=pl.BlockSpec((blk, aq), lambda bi, n: (bi * nb + n, 0)),
        out_shape=jax.ShapeDtypeStruct((b * t, aq), BF16),
        compiler_params=_params("parallel", "parallel"),
    )(px, px, px, px, px, px, px, pc, pc, cos_t, cos_t, cos_t, sin_t, sin_t, sin_t, sink_col)


def _ctx_attn_body(q_ref, k_ref, v_ref, sink_ref, o_ref, *, group):
    scale = HEAD_DIM ** -0.5
    for h in range(KV_HEADS):
        hs = slice(h * HEAD_DIM, (h + 1) * HEAD_DIM)
        kall = k_ref[:, hs].astype(BF16)
        vall = v_ref[:, hs].astype(BF16)
        for g in range(group):
            c0 = (h * group + g) * HEAD_DIM
            s = _dot_nt(q_ref[:, c0:c0 + HEAD_DIM].astype(BF16), kall) * scale
            o = _softmax_pv([s], sink_ref[h * group + g], vall)
            o_ref[:, c0:c0 + HEAD_DIM] = o.astype(o_ref.dtype)


def _context_attention(pc, sink_rows, b, l, aq, akv):
    group = aq // akv
    kcol, vcol = aq // akv, aq // akv + 1
    return pl.pallas_call(
        functools.partial(_ctx_attn_body, group=group),
        grid=(b,),
        in_specs=[pl.BlockSpec((l, aq), lambda bi: (bi, 0)),
                  pl.BlockSpec((l, akv), lambda bi: (bi, kcol)),
                  pl.BlockSpec((l, akv), lambda bi: (bi, vcol)),
                  pl.BlockSpec((KV_HEADS * group, l, 1), lambda bi: (0, 0, 0))],
        out_specs=pl.BlockSpec((l, aq), lambda bi: (bi, 0)),
        out_shape=jax.ShapeDtypeStruct((b * l, aq), BF16),
        compiler_params=_params("parallel"),
    )(pc, pc, pc, sink_rows)


def _cmlp_body(u_ref, gv_ref, g_ref, ws_ref, bs_ref, o_ref, *, groups):
    u = jax.nn.gelu(u_ref[...])
    gv = jax.nn.gelu(gv_ref[...])
    gvn = gv * lax.rsqrt(jnp.mean(gv * gv, axis=-1, keepdims=True) + EPS) * g_ref[...]
    for gi in range(groups):
        cs = slice(gi * CMLP_CH, (gi + 1) * CMLP_CH)
        mixed = _dot(ws_ref[gi].astype(BF16), gvn[:, cs].astype(BF16)) + bs_ref[gi]
        o_ref[:, cs] = (u[:, cs] * mixed).astype(o_ref.dtype)


def _chunk_mlp(p, norm_g, ws, bs_b, ucol, cw):
    n = p.shape[0]
    groups = cw // CMLP_CH
    ch = CMLP_CHUNK
    return pl.pallas_call(
        functools.partial(_cmlp_body, groups=groups),
        grid=(n // ch,),
        in_specs=[pl.BlockSpec((ch, cw), lambda i: (i, ucol)),
                  pl.BlockSpec((ch, cw), lambda i: (i, ucol + 1)),
                  pl.BlockSpec((1, cw), lambda i: (0, 0)),
                  pl.BlockSpec((groups, ch, ch), lambda i: (0, 0, 0)),
                  pl.BlockSpec((groups, ch, CMLP_CH), lambda i: (0, 0, 0))],
        out_specs=pl.BlockSpec((ch, cw), lambda i: (i, 0)),
        out_shape=jax.ShapeDtypeStruct((n, cw), BF16),
        compiler_params=_params("parallel"),
    )(p, p, norm_g, ws, bs_b)


def _head_sum(x):
    ones = (_iota((128, 128), 0) // RWKV_HEAD == _iota((128, 128), 1) // RWKV_HEAD).astype(F32)
    cols = [_dot(x[:, s * 128:(s + 1) * 128], ones, precision=HIGHEST) for s in range(x.shape[1] // 128)]
    return cols[0] if len(cols) == 1 else jnp.concatenate(cols, axis=1)


def _prep_body(r_ref, k_ref, v_ref, rp_ref, kp_ref, vp_ref, rn_ref, kn_ref, vn_ref,
               cr_ref, ck_ref, cv_ref, hw_ref, ha_ref, w2_ref, a2_ref, w0_ref, a0_ref, kkp_ref, kap_ref,
               ro_ref, ko_ref, vo_ref, kko_ref, lw_ref, bo_ref, kr_ref, *, tr, seq, lora):
    i = pl.program_id(0)
    first = (i * tr) % seq == 0
    last = ((i + 1) * tr) % seq == 0
    row = _iota(r_ref.shape, 0)

    def conv(x_ref, xp_ref, xn_ref, w_ref):
        x = x_ref[...]
        before = jnp.where(first, 0.0, xp_ref[7:8, :])
        after = jnp.where(last, 0.0, xn_ref[0:1, :])
        xm = jnp.where(row == 0, before, pltpu.roll(x, 1, axis=0))
        xp = jnp.where(row == tr - 1, after, pltpu.roll(x, tr - 1, axis=0))
        return xm * w_ref[0:1, :] + x * w_ref[1:2, :] + xp * w_ref[2:3, :]

    r = conv(r_ref, rp_ref, rn_ref, cr_ref)
    k = conv(k_ref, kp_ref, kn_ref, ck_ref)
    v = conv(v_ref, vp_ref, vn_ref, cv_ref)
    kk = k * kkp_ref[...]
    kk = kk * lax.rsqrt(_head_sum(kk * kk) + 1e-12)
    ro_ref[...] = r
    ko_ref[...] = k
    vo_ref[...] = v
    kko_ref[...] = kk
    for z in range(2):
        zs = slice(z * lora, (z + 1) * lora)
        w_raw = w0_ref[z:z + 1, :] + _dot(jnp.tanh(hw_ref[:, zs]).astype(BF16), w2_ref[z].astype(BF16))
        softplus_neg = jnp.maximum(-w_raw, 0.0) + jnp.log1p(jnp.exp(-jnp.abs(w_raw)))
        lw_ref[z] = -jnp.exp(-softplus_neg - 0.5)
        a = jax.nn.sigmoid(a0_ref[z:z + 1, :] + _dot(ha_ref[:, zs].astype(BF16), a2_ref[z].astype(BF16)))
        kr_ref[z] = k * (1.0 + (a - 1.0) * kap_ref[...])
        bo_ref[z] = kk * a


def _rwkv_prep(p, conv_w, w2, a2, w0, a0, kk_p, ka_p, seq, rcol, hcol, rw, lora):
    n = p.shape[0]
    tr = min(256, seq)
    nh = n // 8
    body = functools.partial(_prep_body, tr=tr, seq=seq, lora=lora)
    cur = lambda c: pl.BlockSpec((tr, rw), lambda i: (i, rcol + c))
    prv = lambda c: pl.BlockSpec((8, rw), lambda i: (jnp.maximum(i * (tr // 8) - 1, 0), rcol + c))
    nxt = lambda c: pl.BlockSpec((8, rw), lambda i: (jnp.minimum((i + 1) * (tr // 8), nh - 1), rcol + c))
    cw = lambda c: pl.BlockSpec((3, rw), lambda i: (0, c))
    full2 = lambda shape: pl.BlockSpec(shape, lambda i: (0,) * len(shape))
    shared = pl.BlockSpec((tr, rw), lambda i: (i, 0))
    directed = pl.BlockSpec((2, tr, rw), lambda i: (0, i, 0))
    return pl.pallas_call(
        body,
        grid=(n // tr,),
        in_specs=[cur(0), cur(1), cur(2), prv(0), prv(1), prv(2), nxt(0), nxt(1), nxt(2),
                  cw(0), cw(1), cw(2),
                  pl.BlockSpec((tr, 2 * lora), lambda i: (i, hcol)),
                  pl.BlockSpec((tr, 2 * lora), lambda i: (i, hcol + 1)),
                  full2((2, lora, rw)), full2((2, lora, rw)), full2((2, rw)), full2((2, rw)),
                  full2((1, rw)), full2((1, rw))],
        out_specs=[shared, shared, shared, shared, directed, directed, directed],
        out_shape=[jax.ShapeDtypeStruct((n, rw), F32)] * 4 + [jax.ShapeDtypeStruct((2, n, rw), F32)] * 3,
        compiler_params=_params("parallel"),
    )(p, p, p, p, p, p, p, p, p, conv_w, conv_w, conv_w, p, p, w2, a2, w0, a0, kk_p, ka_p)


def _mm(a, b, passes, kind="nn"):
    fn = {"nn": _dot, "nt": _dot_nt, "tn": _dot_tn}[kind]
    if passes == 6:
        return fn(a, b, precision=HIGHEST)
    a_hi, b_hi = a.astype(BF16), b.astype(BF16)
    if passes == 1:
        return fn(a_hi, b_hi)
    a_lo = (a - a_hi.astype(F32)).astype(BF16)
    b_lo = (b - b_hi.astype(F32)).astype(BF16)
    return fn(a_hi, b_hi) + (fn(a_lo, b_hi) + fn(a_hi, b_lo))


def _unit_tri_inverse(nmats, eye, same16, same32, passes):
    n16 = [jnp.where(same16, n, 0.0) for n in nmats]
    xs = [eye - n for n in n16]
    pw = n16
    for _ in range(3):
        pw = [_mm(p, p, passes) for p in pw]
        xs = [x + _mm(x, p, passes) for x, p in zip(xs, pw)]
    for mask in (same32 & ~same16, ~same32):
        offs = [jnp.where(mask, n, 0.0) for n in nmats]
        xo = [_mm(x, o, passes) for x, o in zip(xs, offs)]
        xs = [x - _mm(t, x, passes) for x, t in zip(xs, xo)]
    return xs


def _scan_body(r_ref, kk_ref, v_ref, lw_ref, b_ref, k_ref, s0_ref, y_ref, sf_ref, st_ref, *, heads, nchunks):
    z = pl.program_id(0)
    c = pl.program_id(3)
    C, K = SCAN_CHUNK, RWKV_HEAD
    pp = SCAN_PASSES

    @pl.when(c == 0)
    def _():
        st_ref[...] = s0_ref[...]

    ti, si = _iota((C, C), 0), _iota((C, C), 1)
    before = (si - ti) * (1 - 2 * z) < 0
    upto = before | (si == ti)
    eye = (si == ti).astype(F32)
    same16 = (ti // 16) == (si // 16)
    same32 = (ti // 32) == (si // 32)

    lw = lw_ref[...]
    lc = _dot(upto.astype(F32), lw, precision=HIGHEST)
    ltot = jnp.sum(lw, axis=0, keepdims=True)
    e_neg = jnp.exp(-lc)
    e_out = jnp.exp(ltot - lc)
    kkt = kk_ref[...] * jnp.exp(lc - lw)
    rt = r_ref[...] * jnp.exp(lc)
    bt = b_ref[...] * e_neg
    kt = k_ref[...] * e_neg
    bh = b_ref[...] * e_out
    kh = k_ref[...] * e_out
    etot = jnp.exp(ltot)
    v = v_ref[...]

    hr = range(heads)
    ls = [slice(h * K, (h + 1) * K) for h in hr]
    ps = [_mm(jnp.concatenate([kkt[:, s], rt[:, s]], axis=0),
              jnp.concatenate([bt[:, s], kt[:, s]], axis=0), pp["pair"], "nt") for s in ls]
    nmats = [jnp.where(before, p[:C, :C], 0.0) for p in ps]
    pkk = [jnp.where(before, p[:C, C:], 0.0) for p in ps]
    prb = [jnp.where(upto, p[C:, :C], 0.0) for p in ps]
    prk = [jnp.where(upto, p[C:, C:], 0.0) for p in ps]
    tinv = _unit_tri_inverse(nmats, eye, same16, same32, pp["inv"])
    tg = [_mm(tinv[h], jnp.concatenate([kkt[:, ls[h]], pkk[h]], axis=1), pp["solve"]) for h in hr]
    qa = [jnp.concatenate([rt[:, ls[h]], prk[h]], axis=1) - _mm(prb[h], tg[h], pp["solve"]) for h in hr]
    m3 = [_mm(bh[:, ls[h]], tg[h], pp["solve"], "tn") for h in hr]
    sv = [jnp.concatenate([st_ref[h], v[:, ls[h]]], axis=0) for h in hr]
    ys = [_mm(qa[h], sv[h], pp["state"]) for h in hr]
    for h in hr:
        decay_diag = eye * jnp.broadcast_to(etot[:, ls[h]], (K, K))
        trans = jnp.concatenate([decay_diag, jnp.zeros((K, C), F32)], axis=1) - m3[h]
        st_ref[h] = _mm(trans, sv[h], pp["state"]) + _mm(kh[:, ls[h]], v[:, ls[h]], pp["state"], "tn")
    y_ref[...] = ys[0] if heads == 1 else jnp.concatenate(ys, axis=1)

    @pl.when(c == nchunks - 1)
    def _():
        sf_ref[...] = st_ref[...]


def _rwkv_scan(r, kk, v, lw, bb, kr, s0, b, seq, rw):
    C, K = SCAN_CHUNK, RWKV_HEAD
    nchunks = seq // C
    heads = min(SCAN_HEADS, rw // K)
    ngroups = rw // (heads * K)
    n = b * seq
    row = lambda z, bi, hg, c: bi * nchunks + c + z * (nchunks - 1 - 2 * c)
    shared = pl.BlockSpec((C, heads * K), lambda z, bi, hg, c: (row(z, bi, hg, c), hg))
    directed = pl.BlockSpec((None, C, heads * K), lambda z, bi, hg, c: (z, row(z, bi, hg, c), hg))
    state = pl.BlockSpec((None, None, heads, K, K), lambda z, bi, hg, c: (z, bi, hg, 0, 0))
    body = functools.partial(_scan_body, heads=heads, nchunks=nchunks)
    return pl.pallas_call(
        body,
        grid=(2, b, ngroups, nchunks),
        in_specs=[shared, shared, shared, directed, directed, directed, state],
        out_specs=[directed, state],
        out_shape=[jax.ShapeDtypeStruct((2, n, rw), F32),
                   jax.ShapeDtypeStruct((2, b, rw // K, K, K), F32)],
        scratch_shapes=[pltpu.VMEM((heads, K, K), F32)],
        compiler_params=_params("parallel", "parallel", "parallel", "arbitrary"),
    )(r, kk, v, lw, bb, kr, s0)


def _rwkv_out_body(yf_ref, yb_ref, r_ref, k_ref, v_ref, g_ref, rk_ref, lnw_ref, lnb_ref, o_ref):
    y = yf_ref[...] + yb_ref[...]
    inv = 1.0 / RWKV_HEAD
    mu = _head_sum(y) * inv
    d = y - mu
    var = _head_sum(d * d) * inv
    yn = d * lax.rsqrt(var + RWKV_GN_EPS) * lnw_ref[...] + lnb_ref[...]
    bonus = _head_sum(r_ref[...] * k_ref[...] * rk_ref[...]) * v_ref[...]
    o_ref[...] = ((yn + bonus) * jax.nn.sigmoid(g_ref[...])).astype(o_ref.dtype)


def _rwkv_output(y, r, k, v, p, gcol, rk, ln_w, ln_b):
    n, rw = r.shape
    tr = min(256, n)
    shared = pl.BlockSpec((tr, rw), lambda i: (i, 0))
    vec = pl.BlockSpec((1, rw), lambda i: (0, 0))
    return pl.pallas_call(
        _rwkv_out_body,
        grid=(n // tr,),
        in_specs=[pl.BlockSpec((None, tr, rw), lambda i: (0, i, 0)),
                  pl.BlockSpec((None, tr, rw), lambda i: (1, i, 0)),
                  shared, shared, shared,
                  pl.BlockSpec((tr, rw), lambda i: (i, gcol)),
                  vec, vec, vec],
        out_specs=shared,
        out_shape=jax.ShapeDtypeStruct((n, rw), BF16),
        compiler_params=_params("parallel"),
    )(y, y, r, k, v, p, rk, ln_w, ln_b)


def _wout_body(a_ref, c_ref, r_ref, wa_ref, wc_ref, wr_ref, x_ref, g_ref, o_ref, *, tm, rows_per_mod, fixed_row):
    row = _mod_row(pl.program_id(0), tm, rows_per_mod, fixed_row)
    acc = _dot(a_ref[...], wa_ref[...]) + _dot(c_ref[...], wc_ref[...]) + _dot(r_ref[...], wr_ref[...])
    o_ref[...] = x_ref[...] + g_ref[pl.ds(row, 1), :] * acc


def _out_proj(attn, cmlp, rwkv, w_out, x2d, mod, layer, rows_per_mod, fixed_row):
    n, d = x2d.shape
    aq, cw, rw = attn.shape[1], cmlp.shape[1], rwkv.shape[1]
    tm = min(512, n)
    tn = min(1024, d)
    body = functools.partial(_wout_body, tm=tm, rows_per_mod=rows_per_mod, fixed_row=fixed_row)
    return pl.pallas_call(
        body,
        grid=(n // tm, d // tn),
        in_specs=[pl.BlockSpec((tm, aq), lambda i, j: (i, 0)),
                  pl.BlockSpec((tm, cw), lambda i, j: (i, 0)),
                  pl.BlockSpec((tm, rw), lambda i, j: (i, 0)),
                  pl.BlockSpec((None, aq, tn), lambda i, j: (layer, 0, j)),
                  pl.BlockSpec((None, cw, tn), lambda i, j: (layer, aq // cw, j)),
                  pl.BlockSpec((None, rw, tn), lambda i, j: (layer, (aq + cw) // rw, j)),
                  pl.BlockSpec((tm, tn), lambda i, j: (i, j)),
                  pl.BlockSpec((None, MOD_ROWS, tn), lambda i, j: (layer, 0, 2 * (d // tn) + j))],
        out_specs=pl.BlockSpec((tm, tn), lambda i, j: (i, j)),
        out_shape=jax.ShapeDtypeStruct((n, d), F32),
        compiler_params=_params("parallel", "parallel"),
    )(attn, cmlp, rwkv, w_out, w_out, w_out, x2d, mod)


def _router_body(x_ref, g_ref, sh_ref, sc_ref, rw_ref, rb_ref, zn_ref, ids_ref, wt_ref, *,
                 tm, rows_per_mod, fixed_row, experts):
    row = _mod_row(pl.program_id(0), tm, rows_per_mod, fixed_row)
    zn = _modulated_norm(x_ref[...], g_ref[...], sh_ref[pl.ds(row, 1), :], sc_ref[pl.ds(row, 1), :])
    zn_ref[...] = zn
    logits = _dot(zn, rw_ref[...], precision=HIGHEST).T
    per_group = experts // N_EXPERT_GROUPS
    scores = [jax.nn.sigmoid(logits[e:e + 1, :]) for e in range(experts)]
    sel = [scores[e] + rb_ref[e:e + 1, :] for e in range(experts)]
    best_val, best_grp = None, None
    for gi in range(N_EXPERT_GROUPS):
        mem = sel[gi * per_group:(gi + 1) * per_group]
        top2 = None
        for a in range(per_group):
            for b2 in range(a + 1, per_group):
                pair = mem[a] + mem[b2]
                top2 = pair if top2 is None else jnp.maximum(top2, pair)
        if gi == 0:
            best_val, best_grp = top2, jnp.zeros(top2.shape, jnp.int32)
        else:
            better = top2 > best_val
            best_grp = jnp.where(better, gi, best_grp)
            best_val = jnp.where(better, top2, best_val)
    chosen, picked = [], []
    for e in range(experts):
        gi = e // per_group
        rank = jnp.zeros(best_grp.shape, jnp.int32)
        for j in range(gi * per_group, (gi + 1) * per_group):
            if j != e:
                ahead = (sel[j] > sel[e]) | ((sel[j] == sel[e]) & (j < e))
                rank = rank + ahead.astype(jnp.int32)
        chosen.append((best_grp == gi) & (rank < TOP_K))
        picked.append(jnp.where(chosen[e], scores[e], 0.0))
    total = picked[0]
    for e in range(1, experts):
        total = total + picked[e]
    zero_i, zero_f = jnp.zeros(total.shape, jnp.int32), jnp.zeros(total.shape, F32)
    seen, ids, wts = zero_i, [zero_i, zero_i], [zero_f, zero_f]
    for e in range(experts):
        gate = picked[e] / total
        for slot in range(TOP_K):
            here = chosen[e] & (seen == slot)
            ids[slot] = jnp.where(here, e, ids[slot])
            wts[slot] = jnp.where(here, gate, wts[slot])
        seen = seen + chosen[e].astype(jnp.int32)
    ids_ref[...] = jnp.concatenate(ids + [jnp.zeros((8 - TOP_K, tm), jnp.int32)], axis=0)
    wt_ref[...] = jnp.concatenate(wts + [jnp.zeros((128 - TOP_K, tm), F32)], axis=0).T


def _router(x2d, g, mod, router_w_pad, router_b_col, layer, rows_per_mod, fixed_row, experts):
    n, d = x2d.shape
    tm = min(256, n)
    body = functools.partial(_router_body, tm=tm, rows_per_mod=rows_per_mod, fixed_row=fixed_row, experts=experts)
    return pl.pallas_call(
        body,
        grid=(n // tm,),
        in_specs=[pl.BlockSpec((tm, d), lambda i: (i, 0)),
                  pl.BlockSpec((None, 1, d), lambda i: (layer, 0, 0)),
                  pl.BlockSpec((None, MOD_ROWS, d), lambda i: (layer, 0, 3)),
                  pl.BlockSpec((None, MOD_ROWS, d), lambda i: (layer, 0, 4)),
                  pl.BlockSpec((d, 128), lambda i: (0, 0)),
                  pl.BlockSpec((128, 1), lambda i: (0, 0))],
        out_specs=[pl.BlockSpec((tm, d), lambda i: (i, 0)),
                   pl.BlockSpec((8, tm), lambda i: (0, i)),
                   pl.BlockSpec((tm, 128), lambda i: (i, 0))],
        out_shape=[jax.ShapeDtypeStruct((n, d), F32), jax.ShapeDtypeStruct((8, n), jnp.int32),
                   jax.ShapeDtypeStruct((n, 128), F32)],
        compiler_params=_params("parallel"),
    )(x2d, g, mod, mod, router_w_pad, router_b_col)


def _route_plan(ids, experts, tm):
    n = ids.shape[1]
    total = TOP_K * n + experts * tm
    flat = ids.reshape(-1)
    onehot = (flat[:, None] == jnp.arange(experts, dtype=jnp.int32)[None, :]).astype(jnp.int32)
    rank = jnp.cumsum(onehot, axis=0) - onehot
    padded = ((jnp.sum(onehot, axis=0) + tm - 1) // tm) * tm
    ends = jnp.cumsum(padded)
    pos = (ends - padded)[flat] + jnp.sum(rank * onehot, axis=1)
    row_token = jnp.zeros((total,), jnp.int32).at[pos].set(jnp.tile(jnp.arange(n, dtype=jnp.int32), TOP_K))
    tile_start = jnp.arange(total // tm, dtype=jnp.int32) * tm
    tile_used = (tile_start < ends[-1]).astype(jnp.int32)
    tile_expert = jnp.minimum(jnp.searchsorted(ends, tile_start, side="right"), experts - 1).astype(jnp.int32)
    last_used = tile_expert[jnp.maximum(ends[-1] // tm - 1, 0)]
    tile_expert = jnp.where(tile_used == 1, tile_expert, last_used)
    return pos.astype(jnp.int32), row_token, tile_expert, tile_used


def _row_copy(src_hbm, row, dst_ref, i, sem):
    return pltpu.make_async_copy(src_hbm.at[pl.ds(row, 1), :], dst_ref.at[pl.ds(i, 1), :], sem)


def _start_rows(src_hbm, dst_ref, sem, index_of, priorities):
    k = len(priorities)

    def start(i, carry):
        for j, prio in enumerate(priorities):
            _row_copy(src_hbm, index_of(i * k + j), dst_ref, i * k + j, sem).start(priority=prio)
        return carry

    lax.fori_loop(0, dst_ref.shape[0] // k, start, 0, unroll=8 // k)


def _wait_rows(src_hbm, dst_ref, sem):
    pltpu.make_async_copy(src_hbm.at[pl.ds(0, dst_ref.shape[0]), :], dst_ref, sem).wait()


def _expert_hidden_body(te_ref, used_ref, tok_ref, zn_hbm, w1_ref, w3_ref, h_ref, xg_ref, xb_ref, sem, *,
                        tm, ntiles):
    p, f = pl.program_id(0), pl.program_id(1)
    used = used_ref[p] == 1
    slot = p % 2
    rows_of = lambda tile: (lambda i: tok_ref[tile * tm + i])

    @pl.when(used & (f == 0) & (p == 0))
    def _():
        _start_rows(zn_hbm, xg_ref.at[0], sem.at[0], rows_of(0), GATHER_PRIORITY)

    @pl.when(used & (f == 0))
    def _():
        _wait_rows(zn_hbm, xg_ref.at[slot], sem.at[slot])
        xb_ref[...] = xg_ref[slot].astype(BF16)

        @pl.when((p + 1 < ntiles) & (used_ref[jnp.minimum(p + 1, ntiles - 1)] == 1))
        def _():
            _start_rows(zn_hbm, xg_ref.at[1 - slot], sem.at[1 - slot], rows_of(p + 1), GATHER_PRIORITY)

    @pl.when(used)
    def _():
        x = xb_ref[...]
        h1 = _dot(x, w1_ref[...].astype(BF16))
        h3 = _dot(x, w3_ref[...].astype(BF16))
        h_ref[...] = ((h1 * jax.nn.sigmoid(h1)) * h3).astype(h_ref.dtype)

    @pl.when(jnp.logical_not(used))
    def _():
        h_ref[...] = jnp.zeros_like(h_ref)


def _expert_out_body(te_ref, used_ref, h_ref, w2_ref, o_ref):
    o_ref[...] = _dot(h_ref[...], w2_ref[...].astype(BF16))


def _experts(zn, row_token, tile_expert, tile_used, w1, w3, w2, layer, tm):
    n, d = zn.shape
    ff = w1.shape[3]
    total = row_token.shape[0]
    ntiles = total // tm
    tf = min(256, ff)
    n_f = ff // tf
    tn = min(2048, d)
    n_j = d // tn
    hold = lambda used, p, j, last: jnp.where(used[p] == 1, j, last)
    w_spec = pl.BlockSpec((None, None, d, tf),
                          lambda p, f, te, us, tok: (layer, te[p], 0, hold(us, p, f, n_f - 1)))
    hidden = pl.pallas_call(
        functools.partial(_expert_hidden_body, tm=tm, ntiles=ntiles),
        grid_spec=pltpu.PrefetchScalarGridSpec(
            num_scalar_prefetch=3,
            grid=(ntiles, n_f),
            in_specs=[pl.BlockSpec(memory_space=pl.ANY), w_spec, w_spec],
            out_specs=pl.BlockSpec((tm, tf), lambda p, f, te, us, tok: (p, f)),
            scratch_shapes=[pltpu.VMEM((2, tm, d), F32), pltpu.VMEM((tm, d), BF16),
                            pltpu.SemaphoreType.DMA((2,))]),
        out_shape=jax.ShapeDtypeStruct((total, ff), BF16),
        compiler_params=_params("arbitrary", "arbitrary", disable_bounds_checks=True),
    )(tile_expert, tile_used, row_token, zn, w1, w3)
    return pl.pallas_call(
        _expert_out_body,
        grid_spec=pltpu.PrefetchScalarGridSpec(
            num_scalar_prefetch=2,
            grid=(n_j, ntiles),
            in_specs=[pl.BlockSpec((tm, ff), lambda j, p, te, us: (p, 0)),
                      pl.BlockSpec((None, None, ff, tn), lambda j, p, te, us: (layer, te[p], 0, j))],
            out_specs=pl.BlockSpec((tm, tn), lambda j, p, te, us: (p, j))),
        out_shape=jax.ShapeDtypeStruct((total, d), F32),
        compiler_params=_params("parallel", "parallel"),
    )(tile_expert, tile_used, hidden, w2)


def _combine_body(pos_ref, ys_hbm, wt_ref, x_ref, g2_ref, fg_ref, o_ref, ya_ref, yb_ref, sem_a, sem_b, *,
                  tm, n, rows_per_mod, fixed_row, final_norm):
    i = pl.program_id(0)
    slot = i % 2

    def start(tile, s):
        _start_rows(ys_hbm, ya_ref.at[s], sem_a.at[s], lambda r: pos_ref[tile * tm + r], COMBINE_PRIORITY)
        _start_rows(ys_hbm, yb_ref.at[s], sem_b.at[s], lambda r: pos_ref[n + tile * tm + r], COMBINE_PRIORITY)

    @pl.when(i == 0)
    def _():
        start(0, 0)

    @pl.when(i + 1 < n // tm)
    def _():
        start(i + 1, 1 - slot)

    _wait_rows(ys_hbm, ya_ref.at[slot], sem_a.at[slot])
    _wait_rows(ys_hbm, yb_ref.at[slot], sem_b.at[slot])
    row = _mod_row(i, tm, rows_per_mod, fixed_row)
    wt = wt_ref[...]
    mix = wt[:, 0:1] * ya_ref[slot] + wt[:, 1:2] * yb_ref[slot]
    out = x_ref[...] + g2_ref[pl.ds(row, 1), :] * mix
    if final_norm:
        out = out * lax.rsqrt(jnp.mean(out * out, axis=-1, keepdims=True) + EPS) * fg_ref[...]
    o_ref[...] = out


def _moe_combine(ys, pos, wts, x2d, mod, final_g, layer, rows_per_mod, fixed_row, final_norm):
    n, d = x2d.shape
    tm = min(256, n)
    body = functools.partial(_combine_body, tm=tm, n=n, rows_per_mod=rows_per_mod, fixed_row=fixed_row,
                             final_norm=final_norm)
    return pl.pallas_call(
        body,
        grid_spec=pltpu.PrefetchScalarGridSpec(
            num_scalar_prefetch=1,
            grid=(n // tm,),
            in_specs=[pl.BlockSpec(memory_space=pl.ANY),
                      pl.BlockSpec((tm, 128), lambda i, ps: (i, 0)),
                      pl.BlockSpec((tm, d), lambda i, ps: (i, 0)),
                      pl.BlockSpec((None, MOD_ROWS, d), lambda i, ps: (layer, 0, 5)),
                      pl.BlockSpec((1, d), lambda i, ps: (0, 0))],
            out_specs=pl.BlockSpec((tm, d), lambda i, ps: (i, 0)),
            scratch_shapes=[pltpu.VMEM((2, tm, d), F32), pltpu.VMEM((2, tm, d), F32),
                            pltpu.SemaphoreType.DMA((2,)), pltpu.SemaphoreType.DMA((2,))]),
        out_shape=jax.ShapeDtypeStruct((n, d), F32),
        compiler_params=_params("arbitrary", disable_bounds_checks=True),
    )(pos, ys, wts, x2d, mod, final_g)


def _moe(x2d, g, mod, router_w_pad, router_b_col, w1, w3, w2, final_g, layer, experts,
         rows_per_mod, fixed_row, final_norm=False):
    n = x2d.shape[0]
    tm = 512 if n >= 8 * 512 else 128
    zn, ids, wts = _router(x2d, g, mod, router_w_pad, router_b_col, layer, rows_per_mod, fixed_row, experts)
    pos, row_token, tile_expert, tile_used = _route_plan(ids[:TOP_K], experts, tm)
    ys = _experts(zn, row_token, tile_expert, tile_used, w1, w3, w2, layer, tm)
    return _moe_combine(ys, pos, wts, x2d, mod, final_g, layer, rows_per_mod, fixed_row, final_norm)


def _rope_tables(t):
    pos = jnp.arange(t)
    half = HEAD_DIM // 4
    freqs = ROPE_BASE ** (-jnp.arange(half, dtype=F32) / half)
    ang_r = (pos // GRID_W).astype(F32)[:, None] * freqs[None, :]
    ang_c = (pos % GRID_W).astype(F32)[:, None] * freqs[None, :]
    cos_t = jnp.concatenate([jnp.cos(ang_r)] * 2 + [jnp.cos(ang_c)] * 2, axis=1)
    sin_t = jnp.concatenate([-jnp.sin(ang_r), jnp.sin(ang_r), -jnp.sin(ang_c), jnp.sin(ang_c)], axis=1)
    return cos_t, sin_t


def kernel(x, c, ctx, c_ctx, ada_w, ada_b, norm1_g, w_in, rwkv_conv, attn_sink, cmlp_norm_g, cmlp_ws, cmlp_b,
           rwkv_w0, rwkv_w1, rwkv_w2, rwkv_a0, rwkv_a1, rwkv_a2, rwkv_kk, rwkv_ka, rwkv_rk, rwkv_ln_w, rwkv_ln_b,
           w_out, norm2_g, router_w, router_b, moe_w1, moe_w3, moe_w2, final_g):
    b, t, d = x.shape
    l = ctx.shape[1]
    depth = ada_w.shape[0]
    cw = cmlp_norm_g.shape[1]
    rw = rwkv_w0.shape[2]
    lora = rwkv_w1.shape[3]
    experts = router_w.shape[1]
    dp = w_in.shape[2]
    akv = KV_HEADS * HEAD_DIM
    aq = dp - 2 * akv - 2 * cw - 4 * rw
    group = aq // akv
    heads = rw // RWKV_HEAD
    ucol = (aq + 2 * akv) // cw
    rcol = (aq + 2 * akv + 2 * cw) // rw
    gcol = rcol + 3
    hcol = dp // (2 * lora)
    assert (aq + 2 * akv) % cw == 0 and (aq + 2 * akv + 2 * cw) % rw == 0 and dp % (2 * lora) == 0
    assert b + 1 <= MOD_ROWS and (b * t) % l == 0 and t % 256 == 0 and l % 128 == 0

    cpad = jnp.zeros((MOD_ROWS, d), F32).at[:b].set(c).at[b].set(c_ctx)
    mod = _ada(cpad, ada_w, ada_b)
    cos_t, sin_t = _rope_tables(t)
    router_w_pad = jnp.zeros((d, 128), F32).at[:, :experts].set(router_w)
    router_b_col = jnp.zeros((128, 1), F32).at[:experts, 0].set(router_b)
    w_out_bf = w_out.astype(BF16)
    w_in_bf = w_in.astype(BF16)
    w_lora_bf = jnp.concatenate([rwkv_w1[:, 0], rwkv_w1[:, 1], rwkv_a1[:, 0], rwkv_a1[:, 1]], axis=2).astype(BF16)
    assert dp % w_lora_bf.shape[2] == 0
    s_zero = jnp.zeros((2, b, heads, RWKV_HEAD, RWKV_HEAD), F32)

    xs = x.reshape(b * t, d)
    hs = ctx.reshape(b * l, d)
    for layer in range(depth):
        lat = dict(rows_per_mod=t, fixed_row=None)
        con = dict(rows_per_mod=None, fixed_row=b)
        last = layer == depth - 1
        px = _proj(xs, norm1_g.reshape(depth, 1, d), mod, w_in_bf, w_lora_bf, layer, **lat)
        pc = _proj(hs, norm1_g.reshape(depth, 1, d), mod, w_in_bf, w_lora_bf, layer, **con)

        sink = attn_sink[layer]
        sink_col = jnp.repeat(sink.reshape(KV_HEADS, group), ATTN_BLOCK, axis=1)[..., None]
        attn_x = _latent_attention(px, pc, cos_t, sin_t, sink_col, b, t, l, aq, akv)

        bs_b = jnp.broadcast_to(cmlp_b[layer][:, :, None], cmlp_b.shape[1:] + (CMLP_CH,))
        cmlp_x = _chunk_mlp(px, cmlp_norm_g[layer][None], cmlp_ws[layer], bs_b, ucol, cw)

        prep_args = (rwkv_conv[layer], rwkv_w2[layer], rwkv_a2[layer], rwkv_w0[layer], rwkv_a0[layer],
                     rwkv_kk[layer][None], rwkv_ka[layer][None])
        rc, kc, vc, kkc, lwc, bbc, krc = _rwkv_prep(pc, *prep_args, l, rcol, hcol, rw, lora)
        rx, kx, vx, kkx, lwx, bbx, krx = _rwkv_prep(px, *prep_args, t, rcol, hcol, rw, lora)
        y_c, s_ctx = _rwkv_scan(rc, kkc, vc, lwc, bbc, krc, s_zero, b, l, rw)
        y_x, _ = _rwkv_scan(rx, kkx, vx, lwx, bbx, krx, s_ctx, b, t, rw)
        out_args = (rwkv_rk[layer][None], rwkv_ln_w[layer][None], rwkv_ln_b[layer][None])
        rwkv_x = _rwkv_output(y_x, rx, kx, vx, px, gcol, *out_args)

        xs = _out_proj(attn_x, cmlp_x, rwkv_x, w_out_bf, xs, mod, layer, **lat)
        moe_args = (norm2_g.reshape(depth, 1, d), mod, router_w_pad, router_b_col, moe_w1, moe_w3, moe_w2,
                    final_g[None], layer, experts)
        xs = _moe(xs, *moe_args, final_norm=last, **lat)

        if not last:
            sink_rows = jnp.broadcast_to(sink[:, None, None], (KV_HEADS * group, l, 1))
            attn_c = _context_attention(pc, sink_rows, b, l, aq, akv)
            cmlp_c = _chunk_mlp(pc, cmlp_norm_g[layer][None], cmlp_ws[layer], bs_b, ucol, cw)
            rwkv_c = _rwkv_output(y_c, rc, kc, vc, pc, gcol, *out_args)
            hs = _out_proj(attn_c, cmlp_c, rwkv_c, w_out_bf, hs, mod, layer, **con)
            hs = _moe(hs, *moe_args, **con)
    return xs.reshape(b, t, d)
```

```python
import functools

import jax
import jax.numpy as jnp
from jax import lax
from jax.experimental import pallas as pl
from jax.experimental.pallas import tpu as pltpu

F32, BF16 = jnp.float32, jnp.bfloat16
HIGHEST = lax.Precision.HIGHEST

HEAD_DIM = 128
KV_HEADS = 4
WINDOW = 128
ATTN_BLOCK = 128
GRID_W = 64
ROPE_BASE = 10000.0
CMLP_CH = 128
CMLP_CHUNK = 128
RWKV_HEAD = 64
RWKV_GN_EPS = 64e-5
N_EXPERT_GROUPS = 4
TOP_K = 2
N_MOD = 6
EPS = 1e-6
MASKED = -1e30

MOD_ROWS = 8
SCAN_CHUNK = 64
SCAN_HEADS = 16
SCAN_PASSES = {"pair": 1, "inv": 1, "solve": 1, "state": 1}
GATHER_PRIORITY = (1,)
COMBINE_PRIORITY = (0, 1)
VMEM_LIMIT_BYTES = 56 * 1024 * 1024


def _params(*sem, **kw):
    return pltpu.CompilerParams(dimension_semantics=sem, vmem_limit_bytes=VMEM_LIMIT_BYTES, **kw)


def _dot(a, b, **kw):
    return jnp.dot(a, b, preferred_element_type=F32, **kw)


def _dot_nt(a, b, **kw):
    return lax.dot_general(a, b, (((1,), (1,)), ((), ())), preferred_element_type=F32, **kw)


def _dot_tn(a, b, **kw):
    return lax.dot_general(a, b, (((0,), (0,)), ((), ())), preferred_element_type=F32, **kw)


def _iota(shape, dim):
    return lax.broadcasted_iota(jnp.int32, shape, dim)


def _ada_body(c_ref, w_ref, b_ref, o_ref):
    c = c_ref[...]
    a = (c * jax.nn.sigmoid(c)).astype(BF16)
    o_ref[...] = _dot(a, w_ref[...].astype(BF16)) + b_ref[...]


def _ada(cpad, ada_w, ada_b):
    depth, d, n = ada_w.shape
    tn = 512
    return pl.pallas_call(
        _ada_body,
        grid=(depth, n // tn),
        in_specs=[pl.BlockSpec((MOD_ROWS, d), lambda l, j: (0, 0)),
                  pl.BlockSpec((None, d, tn), lambda l, j: (l, 0, j)),
                  pl.BlockSpec((None, 1, tn), lambda l, j: (l, 0, j))],
        out_specs=pl.BlockSpec((None, MOD_ROWS, tn), lambda l, j: (l, 0, j)),
        out_shape=jax.ShapeDtypeStruct((depth, MOD_ROWS, n), F32),
        compiler_params=_params("parallel", "parallel"),
    )(cpad, ada_w, ada_b.reshape(depth, 1, n))


def _mod_row(i, tm, rows_per_mod, fixed_row):
    return fixed_row if rows_per_mod is None else (i * tm) // rows_per_mod


def _modulated_norm(x, g, shift, scale):
    y = x * lax.rsqrt(jnp.mean(x * x, axis=-1, keepdims=True) + EPS) * g
    return y * (1.0 + scale) + shift


def _proj_body(x_ref, g_ref, sh_ref, sc_ref, w_ref, wl_ref, o_ref, xn_ref, *, tm, n_main, rows_per_mod, fixed_row):
    j = pl.program_id(1)

    @pl.when(j == 0)
    def _():
        r = _mod_row(pl.program_id(0), tm, rows_per_mod, fixed_row)
        xn = _modulated_norm(x_ref[...], g_ref[...], sh_ref[pl.ds(r, 1), :], sc_ref[pl.ds(r, 1), :])
        xn_ref[...] = xn.astype(BF16)

    @pl.when(j < n_main)
    def _():
        o_ref[...] = _dot(xn_ref[...], w_ref[...])

    @pl.when(j >= n_main)
    def _():
        o_ref[...] = _dot(xn_ref[...], wl_ref[...])


def _proj(x2d, g, mod, w_in, w_lora, layer, rows_per_mod, fixed_row):
    n, d = x2d.shape
    dp, tn = w_in.shape[2], w_lora.shape[2]
    tm = min(512, n)
    n_main = dp // tn
    body = functools.partial(_proj_body, tm=tm, n_main=n_main, rows_per_mod=rows_per_mod, fixed_row=fixed_row)
    return pl.pallas_call(
        body,
        grid=(n // tm, n_main + 1),
        in_specs=[pl.BlockSpec((tm, d), lambda i, j: (i, 0)),
                  pl.BlockSpec((None, 1, d), lambda i, j: (layer, 0, 0)),
                  pl.BlockSpec((None, MOD_ROWS, d), lambda i, j: (layer, 0, 0)),
                  pl.BlockSpec((None, MOD_ROWS, d), lambda i, j: (layer, 0, 1)),
                  pl.BlockSpec((None, d, tn), lambda i, j: (layer, 0, jnp.minimum(j, n_main - 1))),
                  pl.BlockSpec((None, d, tn), lambda i, j: (layer, 0, 0))],
        out_specs=pl.BlockSpec((tm, tn), lambda i, j: (i, j)),
        out_shape=jax.ShapeDtypeStruct((n, dp + tn), F32),
        scratch_shapes=[pltpu.VMEM((tm, d), BF16)],
        compiler_params=_params("parallel", "arbitrary"),
    )(x2d, g, mod, mod, w_in, w_lora)


def _rope(x, cos, sin_signed):
    lane = _iota(x.shape, 1)
    swapped = jnp.where((lane % 64) < 32, pltpu.roll(x, 96, axis=1), pltpu.roll(x, 32, axis=1))
    return x * cos + swapped * sin_signed


def _softmax_pv(parts, sink_col, vall):
    m = sink_col
    for s in parts:
        m = jnp.maximum(m, jnp.max(s, axis=-1, keepdims=True))
    ps = [jnp.exp(s - m) for s in parts]
    denom = jnp.exp(sink_col - m)
    for p in ps:
        denom = denom + jnp.sum(p, axis=-1, keepdims=True)
    p = ps[0] if len(ps) == 1 else jnp.concatenate(ps, axis=1)
    return _dot(p.astype(BF16), vall) / denom


def _attn_body(q_ref, kp_ref, kc_ref, kn_ref, vp_ref, vc_ref, vn_ref, kx_ref, vx_ref,
               cp_ref, cc_ref, cn_ref, sp_ref, sc_ref, sn_ref, sink_ref, o_ref, *, nb, group):
    n = pl.program_id(1)
    blk = ATTN_BLOCK
    cos = (cp_ref[...], cc_ref[...], cn_ref[...])
    sin = (sp_ref[...], sc_ref[...], sn_ref[...])
    qi = _iota((group * blk, 3 * blk), 0) % blk
    kj = _iota((group * blk, 3 * blk), 1)
    in_seq = ((kj >= blk) | (n > 0)) & ((kj < 2 * blk) | (n < nb - 1))
    band_ok = (jnp.abs(kj - blk - qi) <= WINDOW) & in_seq
    scale = HEAD_DIM ** -0.5
    for h in range(KV_HEADS):
        hs = slice(h * HEAD_DIM, (h + 1) * HEAD_DIM)
        kb = [_rope(r[:, hs], c, s) for r, c, s in zip((kp_ref, kc_ref, kn_ref), cos, sin)]
        kall = jnp.concatenate(kb + [kx_ref[:, hs]], axis=0).astype(BF16)
        vall = jnp.concatenate([vp_ref[:, hs], vc_ref[:, hs], vn_ref[:, hs], vx_ref[:, hs]], axis=0).astype(BF16)
        qs = []
        for g in range(group):
            c0 = (h * group + g) * HEAD_DIM
            qs.append(_rope(q_ref[:, c0:c0 + HEAD_DIM], cos[1], sin[1]))
        qh = jnp.concatenate(qs, axis=0).astype(BF16)
        s = _dot_nt(qh, kall) * scale
        s_loc = jnp.where(band_ok, s[:, :3 * blk], MASKED)
        o = _softmax_pv([s_loc, s[:, 3 * blk:]], sink_ref[h], vall)
        for g in range(group):
            c0 = (h * group + g) * HEAD_DIM
            o_ref[:, c0:c0 + HEAD_DIM] = o[g * blk:(g + 1) * blk].astype(o_ref.dtype)


def _latent_attention(px, pc, cos_t, sin_t, sink_col, b, t, l, aq, akv):
    blk = ATTN_BLOCK
    nb = t // blk
    group = aq // akv
    kcol, vcol = aq // akv, aq // akv + 1
    prev = lambda n: jnp.maximum(n - 1, 0)
    nxt = lambda n: jnp.minimum(n + 1, nb - 1)
    kv_spec = lambda col, f: pl.BlockSpec((blk, akv), lambda bi, n: (bi * nb + f(n), col))
    tab_spec = lambda f: pl.BlockSpec((blk, HEAD_DIM), lambda bi, n: (f(n), 0))
    ident = lambda n: n
    body = functools.partial(_attn_body, nb=nb, group=group)
    return pl.pallas_call(
        body,
        grid=(b, nb),
        in_specs=[pl.BlockSpec((blk, aq), lambda bi, n: (bi * nb + n, 0)),
                  kv_spec(kcol, prev), kv_spec(kcol, ident), kv_spec(kcol, nxt),
                  kv_spec(vcol, prev), kv_spec(vcol, ident), kv_spec(vcol, nxt),
                  pl.BlockSpec((l, akv), lambda bi, n: (bi, kcol)),
                  pl.BlockSpec((l, akv), lambda bi, n: (bi, vcol)),
                  tab_spec(prev), tab_spec(ident), tab_spec(nxt),
                  tab_spec(prev), tab_spec(ident), tab_spec(nxt),
                  pl.BlockSpec((KV_HEADS, group * blk, 1), lambda bi, n: (0, 0, 0))],
        out_specs=pl.BlockSpec((blk, aq), lambda bi, n: (bi * nb + n, 0)),
        out_shape=jax.ShapeDtypeStruct((b * t, aq), BF16),
        compiler_params=_params("parallel", "parallel"),
    )(px, px, px, px, px, px, px, pc, pc, cos_t, cos_t, cos_t, sin_t, sin_t, sin_t, sink_col)


def _ctx_attn_body(q_ref, k_ref, v_ref, sink_ref, o_ref, *, group):
    scale = HEAD_DIM ** -0.5
    for h in range(KV_HEADS):
        hs = slice(h * HEAD_DIM, (h + 1) * HEAD_DIM)
        kall = k_ref[:, hs].astype(BF16)
        vall = v_ref[:, hs].astype(BF16)
        for g in range(group):
            c0 = (h * group + g) * HEAD_DIM
            s = _dot_nt(q_ref[:, c0:c0 + HEAD_DIM].astype(BF16), kall) * scale
            o = _softmax_pv([s], sink_ref[h * group + g], vall)
            o_ref[:, c0:c0 + HEAD_DIM] = o.astype(o_ref.dtype)


def _context_attention(pc, sink_rows, b, l, aq, akv):
    group = aq // akv
    kcol, vcol = aq // akv, aq // akv + 1
    return pl.pallas_call(
        functools.partial(_ctx_attn_body, group=group),
        grid=(b,),
        in_specs=[pl.BlockSpec((l, aq), lambda bi: (bi, 0)),
                  pl.BlockSpec((l, akv), lambda bi: (bi, kcol)),
                  pl.BlockSpec((l, akv), lambda bi: (bi, vcol)),
                  pl.BlockSpec((KV_HEADS * group, l, 1), lambda bi: (0, 0, 0))],
        out_specs=pl.BlockSpec((l, aq), lambda bi: (bi, 0)),
        out_shape=jax.ShapeDtypeStruct((b * l, aq), BF16),
        compiler_params=_params("parallel"),
    )(pc, pc, pc, sink_rows)


def _cmlp_body(u_ref, gv_ref, g_ref, ws_ref, bs_ref, o_ref, *, groups):
    u = jax.nn.gelu(u_ref[...])
    gv = jax.nn.gelu(gv_ref[...])
    gvn = gv * lax.rsqrt(jnp.mean(gv * gv, axis=-1, keepdims=True) + EPS) * g_ref[...]
    for gi in range(groups):
        cs = slice(gi * CMLP_CH, (gi + 1) * CMLP_CH)
        mixed = _dot(ws_ref[gi].astype(BF16), gvn[:, cs].astype(BF16)) + bs_ref[gi]
        o_ref[:, cs] = (u[:, cs] * mixed).astype(o_ref.dtype)


def _chunk_mlp(p, norm_g, ws, bs_b, ucol, cw):
    n = p.shape[0]
    groups = cw // CMLP_CH
    ch = CMLP_CHUNK
    return pl.pallas_call(
        functools.partial(_cmlp_body, groups=groups),
        grid=(n // ch,),
        in_specs=[pl.BlockSpec((ch, cw), lambda i: (i, ucol)),
                  pl.BlockSpec((ch, cw), lambda i: (i, ucol + 1)),
                  pl.BlockSpec((1, cw), lambda i: (0, 0)),
                  pl.BlockSpec((groups, ch, ch), lambda i: (0, 0, 0)),
                  pl.BlockSpec((groups, ch, CMLP_CH), lambda i: (0, 0, 0))],
        out_specs=pl.BlockSpec((ch, cw), lambda i: (i, 0)),
        out_shape=jax.ShapeDtypeStruct((n, cw), BF16),
        compiler_params=_params("parallel"),
    )(p, p, norm_g, ws, bs_b)


def _head_sum(x):
    ones = (_iota((128, 128), 0) // RWKV_HEAD == _iota((128, 128), 1) // RWKV_HEAD).astype(F32)
    cols = [_dot(x[:, s * 128:(s + 1) * 128], ones, precision=HIGHEST) for s in range(x.shape[1] // 128)]
    return cols[0] if len(cols) == 1 else jnp.concatenate(cols, axis=1)


def _prep_body(r_ref, k_ref, v_ref, rp_ref, kp_ref, vp_ref, rn_ref, kn_ref, vn_ref,
               cr_ref, ck_ref, cv_ref, hw_ref, ha_ref, w2_ref, a2_ref, w0_ref, a0_ref, kkp_ref, kap_ref,
               ro_ref, ko_ref, vo_ref, kko_ref, lw_ref, bo_ref, kr_ref, *, tr, seq, lora):
    i = pl.program_id(0)
    first = (i * tr) % seq == 0
    last = ((i + 1) * tr) % seq == 0
    row = _iota(r_ref.shape, 0)

    def conv(x_ref, xp_ref, xn_ref, w_ref):
        x = x_ref[...]
        before = jnp.where(first, 0.0, xp_ref[7:8, :])
        after = jnp.where(last, 0.0, xn_ref[0:1, :])
        xm = jnp.where(row == 0, before, pltpu.roll(x, 1, axis=0))
        xp = jnp.where(row == tr - 1, after, pltpu.roll(x, tr - 1, axis=0))
        return xm * w_ref[0:1, :] + x * w_ref[1:2, :] + xp * w_ref[2:3, :]

    r = conv(r_ref, rp_ref, rn_ref, cr_ref)
    k = conv(k_ref, kp_ref, kn_ref, ck_ref)
    v = conv(v_ref, vp_ref, vn_ref, cv_ref)
    kk = k * kkp_ref[...]
    kk = kk * lax.rsqrt(_head_sum(kk * kk) + 1e-12)
    ro_ref[...] = r
    ko_ref[...] = k
    vo_ref[...] = v
    kko_ref[...] = kk
    for z in range(2):
        zs = slice(z * lora, (z + 1) * lora)
        w_raw = w0_ref[z:z + 1, :] + _dot(jnp.tanh(hw_ref[:, zs]).astype(BF16), w2_ref[z].astype(BF16))
        softplus_neg = jnp.maximum(-w_raw, 0.0) + jnp.log1p(jnp.exp(-jnp.abs(w_raw)))
        lw_ref[z] = -jnp.exp(-softplus_neg - 0.5)
        a = jax.nn.sigmoid(a0_ref[z:z + 1, :] + _dot(ha_ref[:, zs].astype(BF16), a2_ref[z].astype(BF16)))
        kr_ref[z] = k * (1.0 + (a - 1.0) * kap_ref[...])
        bo_ref[z] = kk * a


def _rwkv_prep(p, conv_w, w2, a2, w0, a0, kk_p, ka_p, seq, rcol, hcol, rw, lora):
    n = p.shape[0]
    tr = min(256, seq)
    nh = n // 8
    body = functools.partial(_prep_body, tr=tr, seq=seq, lora=lora)
    cur = lambda c: pl.BlockSpec((tr, rw), lambda i: (i, rcol + c))
    prv = lambda c: pl.BlockSpec((8, rw), lambda i: (jnp.maximum(i * (tr // 8) - 1, 0), rcol + c))
    nxt = lambda c: pl.BlockSpec((8, rw), lambda i: (jnp.minimum((i + 1) * (tr // 8), nh - 1), rcol + c))
    cw = lambda c: pl.BlockSpec((3, rw), lambda i: (0, c))
    full2 = lambda shape: pl.BlockSpec(shape, lambda i: (0,) * len(shape))
    shared = pl.BlockSpec((tr, rw), lambda i: (i, 0))
    directed = pl.BlockSpec((2, tr, rw), lambda i: (0, i, 0))
    return pl.pallas_call(
        body,
        grid=(n // tr,),
        in_specs=[cur(0), cur(1), cur(2), prv(0), prv(1), prv(2), nxt(0), nxt(1), nxt(2),
                  cw(0), cw(1), cw(2),
                  pl.BlockSpec((tr, 2 * lora), lambda i: (i, hcol)),
                  pl.BlockSpec((tr, 2 * lora), lambda i: (i, hcol + 1)),
                  full2((2, lora, rw)), full2((2, lora, rw)), full2((2, rw)), full2((2, rw)),
                  full2((1, rw)), full2((1, rw))],
        out_specs=[shared, shared, shared, shared, directed, directed, directed],
        out_shape=[jax.ShapeDtypeStruct((n, rw), F32)] * 4 + [jax.ShapeDtypeStruct((2, n, rw), F32)] * 3,
        compiler_params=_params("parallel"),
    )(p, p, p, p, p, p, p, p, p, conv_w, conv_w, conv_w, p, p, w2, a2, w0, a0, kk_p, ka_p)


def _mm(a, b, passes, kind="nn"):
    fn = {"nn": _dot, "nt": _dot_nt, "tn": _dot_tn}[kind]
    if passes == 6:
        return fn(a, b, precision=HIGHEST)
    a_hi, b_hi = a.astype(BF16), b.astype(BF16)
    if passes == 1:
        return fn(a_hi, b_hi)
    a_lo = (a - a_hi.astype(F32)).astype(BF16)
    b_lo = (b - b_hi.astype(F32)).astype(BF16)
    return fn(a_hi, b_hi) + (fn(a_lo, b_hi) + fn(a_hi, b_lo))


def _unit_tri_inverse(nmats, eye, same16, same32, passes):
    n16 = [jnp.where(same16, n, 0.0) for n in nmats]
    xs = [eye - n for n in n16]
    pw = n16
    for _ in range(3):
        pw = [_mm(p, p, passes) for p in pw]
        xs = [x + _mm(x, p, passes) for x, p in zip(xs, pw)]
    for mask in (same32 & ~same16, ~same32):
        offs = [jnp.where(mask, n, 0.0) for n in nmats]
        xo = [_mm(x, o, passes) for x, o in zip(xs, offs)]
        xs = [x - _mm(t, x, passes) for x, t in zip(xs, xo)]
    return xs


def _scan_body(r_ref, kk_ref, v_ref, lw_ref, b_ref, k_ref, s0_ref, y_ref, sf_ref, st_ref, *, heads, nchunks):
    z = pl.program_id(0)
    c = pl.program_id(3)
    C, K = SCAN_CHUNK, RWKV_HEAD
    pp = SCAN_PASSES

    @pl.when(c == 0)
    def _():
        st_ref[...] = s0_ref[...]

    ti, si = _iota((C, C), 0), _iota((C, C), 1)
    before = (si - ti) * (1 - 2 * z) < 0
    upto = before | (si == ti)
    eye = (si == ti).astype(F32)
    same16 = (ti // 16) == (si // 16)
    same32 = (ti // 32) == (si // 32)

    lw = lw_ref[...]
    lc = _dot(upto.astype(F32), lw, precision=HIGHEST)
    ltot = jnp.sum(lw, axis=0, keepdims=True)
    e_neg = jnp.exp(-lc)
    e_out = jnp.exp(ltot - lc)
    kkt = kk_ref[...] * jnp.exp(lc - lw)
    rt = r_ref[...] * jnp.exp(lc)
    bt = b_ref[...] * e_neg
    kt = k_ref[...] * e_neg
    bh = b_ref[...] * e_out
    kh = k_ref[...] * e_out
    etot = jnp.exp(ltot)
    v = v_ref[...]

    hr = range(heads)
    ls = [slice(h * K, (h + 1) * K) for h in hr]
    ps = [_mm(jnp.concatenate([kkt[:, s], rt[:, s]], axis=0),
              jnp.concatenate([bt[:, s], kt[:, s]], axis=0), pp["pair"], "nt") for s in ls]
    nmats = [jnp.where(before, p[:C, :C], 0.0) for p in ps]
    pkk = [jnp.where(before, p[:C, C:], 0.0) for p in ps]
    prb = [jnp.where(upto, p[C:, :C], 0.0) for p in ps]
    prk = [jnp.where(upto, p[C:, C:], 0.0) for p in ps]
    tinv = _unit_tri_inverse(nmats, eye, same16, same32, pp["inv"])
    tg = [_mm(tinv[h], jnp.concatenate([kkt[:, ls[h]], pkk[h]], axis=1), pp["solve"]) for h in hr]
    qa = [jnp.concatenate([rt[:, ls[h]], prk[h]], axis=1) - _mm(prb[h], tg[h], pp["solve"]) for h in hr]
    m3 = [_mm(bh[:, ls[h]], tg[h], pp["solve"], "tn") for h in hr]
    sv = [jnp.concatenate([st_ref[h], v[:, ls[h]]], axis=0) for h in hr]
    ys = [_mm(qa[h], sv[h], pp["state"]) for h in hr]
    for h in hr:
        decay_diag = eye * jnp.broadcast_to(etot[:, ls[h]], (K, K))
        trans = jnp.concatenate([decay_diag, jnp.zeros((K, C), F32)], axis=1) - m3[h]
        st_ref[h] = _mm(trans, sv[h], pp["state"]) + _mm(kh[:, ls[h]], v[:, ls[h]], pp["state"], "tn")
    y_ref[...] = ys[0] if heads == 1 else jnp.concatenate(ys, axis=1)

    @pl.when(c == nchunks - 1)
    def _():
        sf_ref[...] = st_ref[...]


def _rwkv_scan(r, kk, v, lw, bb, kr, s0, b, seq, rw):
    C, K = SCAN_CHUNK, RWKV_HEAD
    nchunks = seq // C
    heads = min(SCAN_HEADS, rw // K)
    ngroups = rw // (heads * K)
    n = b * seq
    row = lambda z, bi, hg, c: bi * nchunks + c + z * (nchunks - 1 - 2 * c)
    shared = pl.BlockSpec((C, heads * K), lambda z, bi, hg, c: (row(z, bi, hg, c), hg))
    directed = pl.BlockSpec((None, C, heads * K), lambda z, bi, hg, c: (z, row(z, bi, hg, c), hg))
    state = pl.BlockSpec((None, None, heads, K, K), lambda z, bi, hg, c: (z, bi, hg, 0, 0))
    body = functools.partial(_scan_body, heads=heads, nchunks=nchunks)
    return pl.pallas_call(
        body,
        grid=(2, b, ngroups, nchunks),
        in_specs=[shared, shared, shared, directed, directed, directed, state],
        out_specs=[directed, state],
        out_shape=[jax.ShapeDtypeStruct((2, n, rw), F32),
                   jax.ShapeDtypeStruct((2, b, rw // K, K, K), F32)],
        scratch_shapes=[pltpu.VMEM((heads, K, K), F32)],
        compiler_params=_params("parallel", "parallel", "parallel", "arbitrary"),
    )(r, kk, v, lw, bb, kr, s0)


def _rwkv_out_body(yf_ref, yb_ref, r_ref, k_ref, v_ref, g_ref, rk_ref, lnw_ref, lnb_ref, o_ref):
    y = yf_ref[...] + yb_ref[...]
    inv = 1.0 / RWKV_HEAD
    mu = _head_sum(y) * inv
    d = y - mu
    var = _head_sum(d * d) * inv
    yn = d * lax.rsqrt(var + RWKV_GN_EPS) * lnw_ref[...] + lnb_ref[...]
    bonus = _head_sum(r_ref[...] * k_ref[...] * rk_ref[...]) * v_ref[...]
    o_ref[...] = ((yn + bonus) * jax.nn.sigmoid(g_ref[...])).astype(o_ref.dtype)


def _rwkv_output(y, r, k, v, p, gcol, rk, ln_w, ln_b):
    n, rw = r.shape
    tr = min(256, n)
    shared = pl.BlockSpec((tr, rw), lambda i: (i, 0))
    vec = pl.BlockSpec((1, rw), lambda i: (0, 0))
    return pl.pallas_call(
        _rwkv_out_body,
        grid=(n // tr,),
        in_specs=[pl.BlockSpec((None, tr, rw), lambda i: (0, i, 0)),
                  pl.BlockSpec((None, tr, rw), lambda i: (1, i, 0)),
                  shared, shared, shared,
                  pl.BlockSpec((tr, rw), lambda i: (i, gcol)),
                  vec, vec, vec],
        out_specs=shared,
        out_shape=jax.ShapeDtypeStruct((n, rw), BF16),
        compiler_params=_params("parallel"),
    )(y, y, r, k, v, p, rk, ln_w, ln_b)


def _wout_body(a_ref, c_ref, r_ref, wa_ref, wc_ref, wr_ref, x_ref, g_ref, o_ref, *, tm, rows_per_mod, fixed_row):
    row = _mod_row(pl.program_id(0), tm, rows_per_mod, fixed_row)
    acc = _dot(a_ref[...], wa_ref[...]) + _dot(c_ref[...], wc_ref[...]) + _dot(r_ref[...], wr_ref[...])
    o_ref[...] = x_ref[...] + g_ref[pl.ds(row, 1), :] * acc


def _out_proj(attn, cmlp, rwkv, w_out, x2d, mod, layer, rows_per_mod, fixed_row):
    n, d = x2d.shape
    aq, cw, rw = attn.shape[1], cmlp.shape[1], rwkv.shape[1]
    tm = min(512, n)
    tn = min(1024, d)
    body = functools.partial(_wout_body, tm=tm, rows_per_mod=rows_per_mod, fixed_row=fixed_row)
    return pl.pallas_call(
        body,
        grid=(n // tm, d // tn),
        in_specs=[pl.BlockSpec((tm, aq), lambda i, j: (i, 0)),
                  pl.BlockSpec((tm, cw), lambda i, j: (i, 0)),
                  pl.BlockSpec((tm, rw), lambda i, j: (i, 0)),
                  pl.BlockSpec((None, aq, tn), lambda i, j: (layer, 0, j)),
                  pl.BlockSpec((None, cw, tn), lambda i, j: (layer, aq // cw, j)),
                  pl.BlockSpec((None, rw, tn), lambda i, j: (layer, (aq + cw) // rw, j)),
                  pl.BlockSpec((tm, tn), lambda i, j: (i, j)),
                  pl.BlockSpec((None, MOD_ROWS, tn), lambda i, j: (layer, 0, 2 * (d // tn) + j))],
        out_specs=pl.BlockSpec((tm, tn), lambda i, j: (i, j)),
        out_shape=jax.ShapeDtypeStruct((n, d), F32),
        compiler_params=_params("parallel", "parallel"),
    )(attn, cmlp, rwkv, w_out, w_out, w_out, x2d, mod)


def _router_body(x_ref, g_ref, sh_ref, sc_ref, rw_ref, rb_ref, zn_ref, ids_ref, wt_ref, *,
                 tm, rows_per_mod, fixed_row, experts):
    row = _mod_row(pl.program_id(0), tm, rows_per_mod, fixed_row)
    zn = _modulated_norm(x_ref[...], g_ref[...], sh_ref[pl.ds(row, 1), :], sc_ref[pl.ds(row, 1), :])
    bits = lax.bitcast_convert_type(zn.astype(BF16).astype(F32), jnp.uint32)
    half = zn.shape[1] // 2
    zn_ref[...] = (bits[:, :half] >> 16) | bits[:, half:]
    logits = _dot(zn, rw_ref[...], precision=HIGHEST).T
    per_group = experts // N_EXPERT_GROUPS
    scores = [jax.nn.sigmoid(logits[e:e + 1, :]) for e in range(experts)]
    sel = [scores[e] + rb_ref[e:e + 1, :] for e in range(experts)]
    best_val, best_grp = None, None
    for gi in range(N_EXPERT_GROUPS):
        mem = sel[gi * per_group:(gi + 1) * per_group]
        top2 = None
        for a in range(per_group):
            for b2 in range(a + 1, per_group):
                pair = mem[a] + mem[b2]
                top2 = pair if top2 is None else jnp.maximum(top2, pair)
        if gi == 0:
            best_val, best_grp = top2, jnp.zeros(top2.shape, jnp.int32)
        else:
            better = top2 > best_val
            best_grp = jnp.where(better, gi, best_grp)
            best_val = jnp.where(better, top2, best_val)
    chosen, picked = [], []
    for e in range(experts):
        gi = e // per_group
        rank = jnp.zeros(best_grp.shape, jnp.int32)
        for j in range(gi * per_group, (gi + 1) * per_group):
            if j != e:
                ahead = (sel[j] > sel[e]) | ((sel[j] == sel[e]) & (j < e))
                rank = rank + ahead.astype(jnp.int32)
        chosen.append((best_grp == gi) & (rank < TOP_K))
        picked.append(jnp.where(chosen[e], scores[e], 0.0))
    total = picked[0]
    for e in range(1, experts):
        total = total + picked[e]
    zero_i, zero_f = jnp.zeros(total.shape, jnp.int32), jnp.zeros(total.shape, F32)
    seen, ids, wts = zero_i, [zero_i, zero_i], [zero_f, zero_f]
    for e in range(experts):
        gate = picked[e] / total
        for slot in range(TOP_K):
            here = chosen[e] & (seen == slot)
            ids[slot] = jnp.where(here, e, ids[slot])
            wts[slot] = jnp.where(here, gate, wts[slot])
        seen = seen + chosen[e].astype(jnp.int32)
    ids_ref[...] = jnp.concatenate(ids + [jnp.zeros((8 - TOP_K, tm), jnp.int32)], axis=0)
    wt_ref[...] = jnp.concatenate(wts + [jnp.zeros((128 - TOP_K, tm), F32)], axis=0).T


def _router(x2d, g, mod, router_w_pad, router_b_col, layer, rows_per_mod, fixed_row, experts):
    n, d = x2d.shape
    tm = min(256, n)
    body = functools.partial(_router_body, tm=tm, rows_per_mod=rows_per_mod, fixed_row=fixed_row, experts=experts)
    return pl.pallas_call(
        body,
        grid=(n // tm,),
        in_specs=[pl.BlockSpec((tm, d), lambda i: (i, 0)),
                  pl.BlockSpec((None, 1, d), lambda i: (layer, 0, 0)),
                  pl.BlockSpec((None, MOD_ROWS, d), lambda i: (layer, 0, 3)),
                  pl.BlockSpec((None, MOD_ROWS, d), lambda i: (layer, 0, 4)),
                  pl.BlockSpec((d, 128), lambda i: (0, 0)),
                  pl.BlockSpec((128, 1), lambda i: (0, 0))],
        out_specs=[pl.BlockSpec((tm, d // 2), lambda i: (i, 0)),
                   pl.BlockSpec((8, tm), lambda i: (0, i)),
                   pl.BlockSpec((tm, 128), lambda i: (i, 0))],
        out_shape=[jax.ShapeDtypeStruct((n, d // 2), jnp.uint32), jax.ShapeDtypeStruct((8, n), jnp.int32),
                   jax.ShapeDtypeStruct((n, 128), F32)],
        compiler_params=_params("parallel"),
    )(x2d, g, mod, mod, router_w_pad, router_b_col)


def _route_plan(ids, experts, tm):
    n = ids.shape[1]
    total = TOP_K * n + experts * tm
    flat = ids.reshape(-1)
    onehot = (flat[:, None] == jnp.arange(experts, dtype=jnp.int32)[None, :]).astype(jnp.int32)
    rank = jnp.cumsum(onehot, axis=0) - onehot
    padded = ((jnp.sum(onehot, axis=0) + tm - 1) // tm) * tm
    ends = jnp.cumsum(padded)
    pos = (ends - padded)[flat] + jnp.sum(rank * onehot, axis=1)
    row_token = jnp.zeros((total,), jnp.int32).at[pos].set(jnp.tile(jnp.arange(n, dtype=jnp.int32), TOP_K))
    tile_start = jnp.arange(total // tm, dtype=jnp.int32) * tm
    tile_used = (tile_start < ends[-1]).astype(jnp.int32)
    tile_expert = jnp.minimum(jnp.searchsorted(ends, tile_start, side="right"), experts - 1).astype(jnp.int32)
    last_used = tile_expert[jnp.maximum(ends[-1] // tm - 1, 0)]
    tile_expert = jnp.where(tile_used == 1, tile_expert, last_used)
    return pos.astype(jnp.int32), row_token, tile_expert, tile_used


def _row_copy(src_hbm, row, dst_ref, i, sem):
    return pltpu.make_async_copy(src_hbm.at[pl.ds(row, 1), :], dst_ref.at[pl.ds(i, 1), :], sem)


def _start_rows(src_hbm, dst_ref, sem, index_of, priorities):
    k = len(priorities)

    def start(i, carry):
        for j, prio in enumerate(priorities):
            _row_copy(src_hbm, index_of(i * k + j), dst_ref, i * k + j, sem).start(priority=prio)
        return carry

    lax.fori_loop(0, dst_ref.shape[0] // k, start, 0, unroll=8 // k)


def _wait_rows(src_hbm, dst_ref, sem):
    pltpu.make_async_copy(src_hbm.at[pl.ds(0, dst_ref.shape[0]), :], dst_ref, sem).wait()


def _expert_hidden_body(te_ref, used_ref, tok_ref, zn_hbm, w1_ref, w3_ref, h_ref, xg_ref, xb_ref, sem, *,
                        tm, ntiles):
    p, f = pl.program_id(0), pl.program_id(1)
    used = used_ref[p] == 1
    slot = p % 2
    rows_of = lambda tile: (lambda i: tok_ref[tile * tm + i])

    @pl.when(used & (f == 0) & (p == 0))
    def _():
        _start_rows(zn_hbm, xg_ref.at[0], sem.at[0], rows_of(0), GATHER_PRIORITY)

    @pl.when(used & (f == 0))
    def _():
        _wait_rows(zn_hbm, xg_ref.at[slot], sem.at[slot])
        words = xg_ref[slot]
        half = words.shape[1]
        xb_ref[:, :half] = lax.bitcast_convert_type(words << 16, F32).astype(BF16)
        xb_ref[:, half:] = lax.bitcast_convert_type(words & jnp.uint32(0xFFFF0000), F32).astype(BF16)

        @pl.when((p + 1 < ntiles) & (used_ref[jnp.minimum(p + 1, ntiles - 1)] == 1))
        def _():
            _start_rows(zn_hbm, xg_ref.at[1 - slot], sem.at[1 - slot], rows_of(p + 1), GATHER_PRIORITY)

    @pl.when(used)
    def _():
        x = xb_ref[...]
        h1 = _dot(x, w1_ref[...].astype(BF16))
        h3 = _dot(x, w3_ref[...].astype(BF16))
        h_ref[...] = ((h1 * jax.nn.sigmoid(h1)) * h3).astype(h_ref.dtype)

    @pl.when(jnp.logical_not(used))
    def _():
        h_ref[...] = jnp.zeros_like(h_ref)


def _expert_out_body(te_ref, used_ref, h_ref, w2_ref, o_ref):
    o_ref[...] = _dot(h_ref[...], w2_ref[...].astype(BF16))


def _experts(zn, row_token, tile_expert, tile_used, w1, w3, w2, layer, tm):
    d, ff = w1.shape[2], w1.shape[3]
    total = row_token.shape[0]
    ntiles = total // tm
    tf = min(256, ff)
    n_f = ff // tf
    tn = min(2048, d)
    n_j = d // tn
    hold = lambda used, p, j, last: jnp.where(used[p] == 1, j, last)
    w_spec = pl.BlockSpec((None, None, d, tf),
                          lambda p, f, te, us, tok: (layer, te[p], 0, hold(us, p, f, n_f - 1)))
    hidden = pl.pallas_call(
        functools.partial(_expert_hidden_body, tm=tm, ntiles=ntiles),
        grid_spec=pltpu.PrefetchScalarGridSpec(
            num_scalar_prefetch=3,
            grid=(ntiles, n_f),
            in_specs=[pl.BlockSpec(memory_space=pl.ANY), w_spec, w_spec],
            out_specs=pl.BlockSpec((tm, tf), lambda p, f, te, us, tok: (p, f)),
            scratch_shapes=[pltpu.VMEM((2, tm, d // 2), jnp.uint32), pltpu.VMEM((tm, d), BF16),
                            pltpu.SemaphoreType.DMA((2,))]),
        out_shape=jax.ShapeDtypeStruct((total, ff), BF16),
        compiler_params=_params("arbitrary", "arbitrary", disable_bounds_checks=True),
    )(tile_expert, tile_used, row_token, zn, w1, w3)
    return pl.pallas_call(
        _expert_out_body,
        grid_spec=pltpu.PrefetchScalarGridSpec(
            num_scalar_prefetch=2,
            grid=(n_j, ntiles),
            in_specs=[pl.BlockSpec((tm, ff), lambda j, p, te, us: (p, 0)),
                      pl.BlockSpec((None, None, ff, tn), lambda j, p, te, us: (layer, te[p], 0, j))],
            out_specs=pl.BlockSpec((tm, tn), lambda j, p, te, us: (p, j))),
        out_shape=jax.ShapeDtypeStruct((total, d), F32),
        compiler_params=_params("parallel", "parallel"),
    )(tile_expert, tile_used, hidden, w2)


def _combine_body(pos_ref, ys_hbm, wt_ref, x_ref, g2_ref, fg_ref, o_ref, ya_ref, yb_ref, sem_a, sem_b, *,
                  tm, n, rows_per_mod, fixed_row, final_norm):
    i = pl.program_id(0)
    slot = i % 2

    def start(tile, s):
        _start_rows(ys_hbm, ya_ref.at[s], sem_a.at[s], lambda r: pos_ref[tile * tm + r], COMBINE_PRIORITY)
        _start_rows(ys_hbm, yb_ref.at[s], sem_b.at[s], lambda r: pos_ref[n + tile * tm + r], COMBINE_PRIORITY)

    @pl.when(i == 0)
    def _():
        start(0, 0)

    @pl.when(i + 1 < n // tm)
    def _():
        start(i + 1, 1 - slot)

    _wait_rows(ys_hbm, ya_ref.at[slot], sem_a.at[slot])
    _wait_rows(ys_hbm, yb_ref.at[slot], sem_b.at[slot])
    row = _mod_row(i, tm, rows_per_mod, fixed_row)
    wt = wt_ref[...]
    mix = wt[:, 0:1] * ya_ref[slot] + wt[:, 1:2] * yb_ref[slot]
    out = x_ref[...] + g2_ref[pl.ds(row, 1), :] * mix
    if final_norm:
        out = out * lax.rsqrt(jnp.mean(out * out, axis=-1, keepdims=True) + EPS) * fg_ref[...]
    o_ref[...] = out


def _moe_combine(ys, pos, wts, x2d, mod, final_g, layer, rows_per_mod, fixed_row, final_norm):
    n, d = x2d.shape
    tm = min(256, n)
    body = functools.partial(_combine_body, tm=tm, n=n, rows_per_mod=rows_per_mod, fixed_row=fixed_row,
                             final_norm=final_norm)
    return pl.pallas_call(
        body,
        grid_spec=pltpu.PrefetchScalarGridSpec(
            num_scalar_prefetch=1,
            grid=(n // tm,),
            in_specs=[pl.BlockSpec(memory_space=pl.ANY),
                      pl.BlockSpec((tm, 128), lambda i, ps: (i, 0)),
                      pl.BlockSpec((tm, d), lambda i, ps: (i, 0)),
                      pl.BlockSpec((None, MOD_ROWS, d), lambda i, ps: (layer, 0, 5)),
                      pl.BlockSpec((1, d), lambda i, ps: (0, 0))],
            out_specs=pl.BlockSpec((tm, d), lambda i, ps: (i, 0)),
            scratch_shapes=[pltpu.VMEM((2, tm, d), F32), pltpu.VMEM((2, tm, d), F32),
                            pltpu.SemaphoreType.DMA((2,)), pltpu.SemaphoreType.DMA((2,))]),
        out_shape=jax.ShapeDtypeStruct((n, d), F32),
        compiler_params=_params("arbitrary", disable_bounds_checks=True),
    )(pos, ys, wts, x2d, mod, final_g)


def _moe(x2d, g, mod, router_w_pad, router_b_col, w1, w3, w2, final_g, layer, experts,
         rows_per_mod, fixed_row, final_norm=False):
    n = x2d.shape[0]
    tm = 512 if n >= 8 * 512 else 128
    zn, ids, wts = _router(x2d, g, mod, router_w_pad, router_b_col, layer, rows_per_mod, fixed_row, experts)
    pos, row_token, tile_expert, tile_used = _route_plan(ids[:TOP_K], experts, tm)
    ys = _experts(zn, row_token, tile_expert, tile_used, w1, w3, w2, layer, tm)
    return _moe_combine(ys, pos, wts, x2d, mod, final_g, layer, rows_per_mod, fixed_row, final_norm)


def _rope_tables(t):
    pos = jnp.arange(t)
    half = HEAD_DIM // 4
    freqs = ROPE_BASE ** (-jnp.arange(half, dtype=F32) / half)
    ang_r = (pos // GRID_W).astype(F32)[:, None] * freqs[None, :]
    ang_c = (pos % GRID_W).astype(F32)[:, None] * freqs[None, :]
    cos_t = jnp.concatenate([jnp.cos(ang_r)] * 2 + [jnp.cos(ang_c)] * 2, axis=1)
    sin_t = jnp.concatenate([-jnp.sin(ang_r), jnp.sin(ang_r), -jnp.sin(ang_c), jnp.sin(ang_c)], axis=1)
    return cos_t, sin_t


def kernel(x, c, ctx, c_ctx, ada_w, ada_b, norm1_g, w_in, rwkv_conv, attn_sink, cmlp_norm_g, cmlp_ws, cmlp_b,
           rwkv_w0, rwkv_w1, rwkv_w2, rwkv_a0, rwkv_a1, rwkv_a2, rwkv_kk, rwkv_ka, rwkv_rk, rwkv_ln_w, rwkv_ln_b,
           w_out, norm2_g, router_w, router_b, moe_w1, moe_w3, moe_w2, final_g):
    b, t, d = x.shape
    l = ctx.shape[1]
    depth = ada_w.shape[0]
    cw = cmlp_norm_g.shape[1]
    rw = rwkv_w0.shape[2]
    lora = rwkv_w1.shape[3]
    experts = router_w.shape[1]
    dp = w_in.shape[2]
    akv = KV_HEADS * HEAD_DIM
    aq = dp - 2 * akv - 2 * cw - 4 * rw
    group = aq // akv
    heads = rw // RWKV_HEAD
    ucol = (aq + 2 * akv) // cw
    rcol = (aq + 2 * akv + 2 * cw) // rw
    gcol = rcol + 3
    hcol = dp // (2 * lora)
    assert (aq + 2 * akv) % cw == 0 and (aq + 2 * akv + 2 * cw) % rw == 0 and dp % (2 * lora) == 0
    assert b + 1 <= MOD_ROWS and (b * t) % l == 0 and t % 256 == 0 and l % 128 == 0

    cpad = jnp.zeros((MOD_ROWS, d), F32).at[:b].set(c).at[b].set(c_ctx)
    mod = _ada(cpad, ada_w, ada_b)
    cos_t, sin_t = _rope_tables(t)
    router_w_pad = jnp.zeros((d, 128), F32).at[:, :experts].set(router_w)
    router_b_col = jnp.zeros((128, 1), F32).at[:experts, 0].set(router_b)
    w_out_bf = w_out.astype(BF16)
    w_in_bf = w_in.astype(BF16)
    w_lora_bf = jnp.concatenate([rwkv_w1[:, 0], rwkv_w1[:, 1], rwkv_a1[:, 0], rwkv_a1[:, 1]], axis=2).astype(BF16)
    assert dp % w_lora_bf.shape[2] == 0
    s_zero = jnp.zeros((2, b, heads, RWKV_HEAD, RWKV_HEAD), F32)

    xs = x.reshape(b * t, d)
    hs = ctx.reshape(b * l, d)
    for layer in range(depth):
        lat = dict(rows_per_mod=t, fixed_row=None)
        con = dict(rows_per_mod=None, fixed_row=b)
        last = layer == depth - 1
        px = _proj(xs, norm1_g.reshape(depth, 1, d), mod, w_in_bf, w_lora_bf, layer, **lat)
        pc = _proj(hs, norm1_g.reshape(depth, 1, d), mod, w_in_bf, w_lora_bf, layer, **con)

        sink = attn_sink[layer]
        sink_col = jnp.repeat(sink.reshape(KV_HEADS, group), ATTN_BLOCK, axis=1)[..., None]
        attn_x = _latent_attention(px, pc, cos_t, sin_t, sink_col, b, t, l, aq, akv)

        bs_b = jnp.broadcast_to(cmlp_b[layer][:, :, None], cmlp_b.shape[1:] + (CMLP_CH,))
        cmlp_x = _chunk_mlp(px, cmlp_norm_g[layer][None], cmlp_ws[layer], bs_b, ucol, cw)

        prep_args = (rwkv_conv[layer], rwkv_w2[layer], rwkv_a2[layer], rwkv_w0[layer], rwkv_a0[layer],
                     rwkv_kk[layer][None], rwkv_ka[layer][None])
        rc, kc, vc, kkc, lwc, bbc, krc = _rwkv_prep(pc, *prep_args, l, rcol, hcol, rw, lora)
        rx, kx, vx, kkx, lwx, bbx, krx = _rwkv_prep(px, *prep_args, t, rcol, hcol, rw, lora)
        y_c, s_ctx = _rwkv_scan(rc, kkc, vc, lwc, bbc, krc, s_zero, b, l, rw)
        y_x, _ = _rwkv_scan(rx, kkx, vx, lwx, bbx, krx, s_ctx, b, t, rw)
        out_args = (rwkv_rk[layer][None], rwkv_ln_w[layer][None], rwkv_ln_b[layer][None])
        rwkv_x = _rwkv_output(y_x, rx, kx, vx, px, gcol, *out_args)

        xs = _out_proj(attn_x, cmlp_x, rwkv_x, w_out_bf, xs, mod, layer, **lat)
        moe_args = (norm2_g.reshape(depth, 1, d), mod, router_w_pad, router_b_col, moe_w1, moe_w3, moe_w2,
                    final_g[None], layer, experts)
        xs = _moe(xs, *moe_args, final_norm=last, **lat)

        if not last:
            sink_rows = jnp.broadcast_to(sink[:, None, None], (KV_HEADS * group, l, 1))
            attn_c = _context_attention(pc, sink_rows, b, l, aq, akv)
            cmlp_c = _chunk_mlp(pc, cmlp_norm_g[layer][None], cmlp_ws[layer], bs_b, ucol, cw)
            rwkv_c = _rwkv_output(y_c, rc, kc, vc, pc, gcol, *out_args)
            hs = _out_proj(attn_c, cmlp_c, rwkv_c, w_out_bf, hs, mod, layer, **con)
            hs = _moe(hs, *moe_args, **con)
    return xs.reshape(b, t, d)
```

```python
import functools

import jax
import jax.numpy as jnp
from jax import lax
from jax.experimental import pallas as pl
from jax.experimental.pallas import tpu as pltpu

F32, BF16 = jnp.float32, jnp.bfloat16
HIGHEST = lax.Precision.HIGHEST

HEAD_DIM = 128
KV_HEADS = 4
WINDOW = 128
ATTN_BLOCK = 128
GRID_W = 64
ROPE_BASE = 10000.0
CMLP_CH = 128
CMLP_CHUNK = 128
RWKV_HEAD = 64
RWKV_GN_EPS = 64e-5
N_EXPERT_GROUPS = 4
TOP_K = 2
N_MOD = 6
EPS = 1e-6
MASKED = -1e30

MOD_ROWS = 8
SCAN_CHUNK = 64
SCAN_HEADS = 16
SCAN_PASSES = {"pair": 1, "inv": 1, "solve": 1, "state": 1}
EXPERT_OUT_TILE = 2048
GATHER_PRIORITY = (1,)
COMBINE_PRIORITY = (0, 1)
VMEM_LIMIT_BYTES = 56 * 1024 * 1024


def _params(*sem, **kw):
    return pltpu.CompilerParams(dimension_semantics=sem, vmem_limit_bytes=VMEM_LIMIT_BYTES, **kw)


def _dot(a, b, **kw):
    return jnp.dot(a, b, preferred_element_type=F32, **kw)


def _dot_nt(a, b, **kw):
    return lax.dot_general(a, b, (((1,), (1,)), ((), ())), preferred_element_type=F32, **kw)


def _dot_tn(a, b, **kw):
    return lax.dot_general(a, b, (((0,), (0,)), ((), ())), preferred_element_type=F32, **kw)


def _iota(shape, dim):
    return lax.broadcasted_iota(jnp.int32, shape, dim)


def _ada_body(c_ref, w_ref, b_ref, o_ref):
    c = c_ref[...]
    a = (c * jax.nn.sigmoid(c)).astype(BF16)
    o_ref[...] = _dot(a, w_ref[...].astype(BF16)) + b_ref[...]


def _ada(cpad, ada_w, ada_b):
    depth, d, n = ada_w.shape
    tn = 512
    return pl.pallas_call(
        _ada_body,
        grid=(depth, n // tn),
        in_specs=[pl.BlockSpec((MOD_ROWS, d), lambda l, j: (0, 0)),
                  pl.BlockSpec((None, d, tn), lambda l, j: (l, 0, j)),
                  pl.BlockSpec((None, 1, tn), lambda l, j: (l, 0, j))],
        out_specs=pl.BlockSpec((None, MOD_ROWS, tn), lambda l, j: (l, 0, j)),
        out_shape=jax.ShapeDtypeStruct((depth, MOD_ROWS, n), F32),
        compiler_params=_params("parallel", "parallel"),
    )(cpad, ada_w, ada_b.reshape(depth, 1, n))


def _mod_row(i, tm, rows_per_mod, fixed_row):
    return fixed_row if rows_per_mod is None else (i * tm) // rows_per_mod


def _modulated_norm(x, g, shift, scale):
    y = x * lax.rsqrt(jnp.mean(x * x, axis=-1, keepdims=True) + EPS) * g
    return y * (1.0 + scale) + shift


def _proj_body(x_ref, g_ref, sh_ref, sc_ref, w_ref, wl_ref, o_ref, xn_ref, *, tm, n_main, rows_per_mod, fixed_row):
    j = pl.program_id(1)

    @pl.when(j == 0)
    def _():
        r = _mod_row(pl.program_id(0), tm, rows_per_mod, fixed_row)
        xn = _modulated_norm(x_ref[...], g_ref[...], sh_ref[pl.ds(r, 1), :], sc_ref[pl.ds(r, 1), :])
        xn_ref[...] = xn.astype(BF16)

    @pl.when(j < n_main)
    def _():
        o_ref[...] = _dot(xn_ref[...], w_ref[...])

    @pl.when(j >= n_main)
    def _():
        o_ref[...] = _dot(xn_ref[...], wl_ref[...])


def _proj(x2d, g, mod, w_in, w_lora, layer, rows_per_mod, fixed_row):
    n, d = x2d.shape
    dp, tn = w_in.shape[2], w_lora.shape[2]
    tm = min(512, n)
    n_main = dp // tn
    body = functools.partial(_proj_body, tm=tm, n_main=n_main, rows_per_mod=rows_per_mod, fixed_row=fixed_row)
    return pl.pallas_call(
        body,
        grid=(n // tm, n_main + 1),
        in_specs=[pl.BlockSpec((tm, d), lambda i, j: (i, 0)),
                  pl.BlockSpec((None, 1, d), lambda i, j: (layer, 0, 0)),
                  pl.BlockSpec((None, MOD_ROWS, d), lambda i, j: (layer, 0, 0)),
                  pl.BlockSpec((None, MOD_ROWS, d), lambda i, j: (layer, 0, 1)),
                  pl.BlockSpec((None, d, tn), lambda i, j: (layer, 0, jnp.minimum(j, n_main - 1))),
                  pl.BlockSpec((None, d, tn), lambda i, j: (layer, 0, 0))],
        out_specs=pl.BlockSpec((tm, tn), lambda i, j: (i, j)),
        out_shape=jax.ShapeDtypeStruct((n, dp + tn), F32),
        scratch_shapes=[pltpu.VMEM((tm, d), BF16)],
        compiler_params=_params("parallel", "arbitrary"),
    )(x2d, g, mod, mod, w_in, w_lora)


def _rope(x, cos, sin_signed):
    lane = _iota(x.shape, 1)
    swapped = jnp.where((lane % 64) < 32, pltpu.roll(x, 96, axis=1), pltpu.roll(x, 32, axis=1))
    return x * cos + swapped * sin_signed


def _softmax_pv(parts, sink_col, vall):
    m = sink_col
    for s in parts:
        m = jnp.maximum(m, jnp.max(s, axis=-1, keepdims=True))
    ps = [jnp.exp(s - m) for s in parts]
    denom = jnp.exp(sink_col - m)
    for p in ps:
        denom = denom + jnp.sum(p, axis=-1, keepdims=True)
    p = ps[0] if len(ps) == 1 else jnp.concatenate(ps, axis=1)
    return _dot(p.astype(BF16), vall) / denom


def _attn_body(q_ref, kp_ref, kc_ref, kn_ref, vp_ref, vc_ref, vn_ref, kx_ref, vx_ref,
               cp_ref, cc_ref, cn_ref, sp_ref, sc_ref, sn_ref, sink_ref, o_ref, *, nb, group):
    n = pl.program_id(1)
    blk = ATTN_BLOCK
    cos = (cp_ref[...], cc_ref[...], cn_ref[...])
    sin = (sp_ref[...], sc_ref[...], sn_ref[...])
    qi = _iota((group * blk, 3 * blk), 0) % blk
    kj = _iota((group * blk, 3 * blk), 1)
    in_seq = ((kj >= blk) | (n > 0)) & ((kj < 2 * blk) | (n < nb - 1))
    band_ok = (jnp.abs(kj - blk - qi) <= WINDOW) & in_seq
    scale = HEAD_DIM ** -0.5
    for h in range(KV_HEADS):
        hs = slice(h * HEAD_DIM, (h + 1) * HEAD_DIM)
        kb = [_rope(r[:, hs], c, s) for r, c, s in zip((kp_ref, kc_ref, kn_ref), cos, sin)]
        kall = jnp.concatenate(kb + [kx_ref[:, hs]], axis=0).astype(BF16)
        vall = jnp.concatenate([vp_ref[:, hs], vc_ref[:, hs], vn_ref[:, hs], vx_ref[:, hs]], axis=0).astype(BF16)
        qs = []
        for g in range(group):
            c0 = (h * group + g) * HEAD_DIM
            qs.append(_rope(q_ref[:, c0:c0 + HEAD_DIM], cos[1], sin[1]))
        qh = jnp.concatenate(qs, axis=0).astype(BF16)
        s = _dot_nt(qh, kall) * scale
        s_loc = jnp.where(band_ok, s[:, :3 * blk], MASKED)
        o = _softmax_pv([s_loc, s[:, 3 * blk:]], sink_ref[h], vall)
        for g in range(group):
            c0 = (h * group + g) * HEAD_DIM
            o_ref[:, c0:c0 + HEAD_DIM] = o[g * blk:(g + 1) * blk].astype(o_ref.dtype)


def _latent_attention(px, pc, cos_t, sin_t, sink_col, b, t, l, aq, akv):
    blk = ATTN_BLOCK
    nb = t // blk
    group = aq // akv
    kcol, vcol = aq // akv, aq // akv + 1
    prev = lambda n: jnp.maximum(n - 1, 0)
    nxt = lambda n: jnp.minimum(n + 1, nb - 1)
    kv_spec = lambda col, f: pl.BlockSpec((blk, akv), lambda bi, n: (bi * nb + f(n), col))
    tab_spec = lambda f: pl.BlockSpec((blk, HEAD_DIM), lambda bi, n: (f(n), 0))
    ident = lambda n: n
    body = functools.partial(_attn_body, nb=nb, group=group)
    return pl.pallas_call(
        body,
        grid=(b, nb),
        in_specs=[pl.BlockSpec((blk, aq), lambda bi, n: (bi * nb + n, 0)),
                  kv_spec(kcol, prev), kv_spec(kcol, ident), kv_spec(kcol, nxt),
                  kv_spec(vcol, prev), kv_spec(vcol, ident), kv_spec(vcol, nxt),
                  pl.BlockSpec((l, akv), lambda bi, n: (bi, kcol)),
                  pl.BlockSpec((l, akv), lambda bi, n: (bi, vcol)),
                  tab_spec(prev), tab_spec(ident), tab_spec(nxt),
                  tab_spec(prev), tab_spec(ident), tab_spec(nxt),
                  pl.BlockSpec((KV_HEADS, group * blk, 1), lambda bi, n: (0, 0, 0))],
        out_specs=pl.BlockSpec((blk, aq), lambda bi, n: (bi * nb + n, 0)),
        out_shape=jax.ShapeDtypeStruct((b * t, aq), BF16),
        compiler_params=_params("parallel", "parallel"),
    )(px, px, px, px, px, px, px, pc, pc, cos_t, cos_t, cos_t, sin_t, sin_t, sin_t, sink_col)


def _ctx_attn_body(q_ref, k_ref, v_ref, sink_ref, o_ref, *, group):
    scale = HEAD_DIM ** -0.5
    for h in range(KV_HEADS):
        hs = slice(h * HEAD_DIM, (h + 1) * HEAD_DIM)
        kall = k_ref[:, hs].astype(BF16)
        vall = v_ref[:, hs].astype(BF16)
        for g in range(group):
            c0 = (h * group + g) * HEAD_DIM
            s = _dot_nt(q_ref[:, c0:c0 + HEAD_DIM].astype(BF16), kall) * scale
            o = _softmax_pv([s], sink_ref[h * group + g], vall)
            o_ref[:, c0:c0 + HEAD_DIM] = o.astype(o_ref.dtype)


def _context_attention(pc, sink_rows, b, l, aq, akv):
    group = aq // akv
    kcol, vcol = aq // akv, aq // akv + 1
    return pl.pallas_call(
        functools.partial(_ctx_attn_body, group=group),
        grid=(b,),
        in_specs=[pl.BlockSpec((l, aq), lambda bi: (bi, 0)),
                  pl.BlockSpec((l, akv), lambda bi: (bi, kcol)),
                  pl.BlockSpec((l, akv), lambda bi: (bi, vcol)),
                  pl.BlockSpec((KV_HEADS * group, l, 1), lambda bi: (0, 0, 0))],
        out_specs=pl.BlockSpec((l, aq), lambda bi: (bi, 0)),
        out_shape=jax.ShapeDtypeStruct((b * l, aq), BF16),
        compiler_params=_params("parallel"),
    )(pc, pc, pc, sink_rows)


def _cmlp_body(u_ref, gv_ref, g_ref, ws_ref, bs_ref, o_ref, *, groups):
    u = jax.nn.gelu(u_ref[...])
    gv = jax.nn.gelu(gv_ref[...])
    gvn = gv * lax.rsqrt(jnp.mean(gv * gv, axis=-1, keepdims=True) + EPS) * g_ref[...]
    for gi in range(groups):
        cs = slice(gi * CMLP_CH, (gi + 1) * CMLP_CH)
        mixed = _dot(ws_ref[gi].astype(BF16), gvn[:, cs].astype(BF16)) + bs_ref[gi]
        o_ref[:, cs] = (u[:, cs] * mixed).astype(o_ref.dtype)


def _chunk_mlp(p, norm_g, ws, bs_b, ucol, cw):
    n = p.shape[0]
    groups = cw // CMLP_CH
    ch = CMLP_CHUNK
    return pl.pallas_call(
        functools.partial(_cmlp_body, groups=groups),
        grid=(n // ch,),
        in_specs=[pl.BlockSpec((ch, cw), lambda i: (i, ucol)),
                  pl.BlockSpec((ch, cw), lambda i: (i, ucol + 1)),
                  pl.BlockSpec((1, cw), lambda i: (0, 0)),
                  pl.BlockSpec((groups, ch, ch), lambda i: (0, 0, 0)),
                  pl.BlockSpec((groups, ch, CMLP_CH), lambda i: (0, 0, 0))],
        out_specs=pl.BlockSpec((ch, cw), lambda i: (i, 0)),
        out_shape=jax.ShapeDtypeStruct((n, cw), BF16),
        compiler_params=_params("parallel"),
    )(p, p, norm_g, ws, bs_b)


def _head_sum(x):
    ones = (_iota((128, 128), 0) // RWKV_HEAD == _iota((128, 128), 1) // RWKV_HEAD).astype(F32)
    cols = [_dot(x[:, s * 128:(s + 1) * 128], ones, precision=HIGHEST) for s in range(x.shape[1] // 128)]
    return cols[0] if len(cols) == 1 else jnp.concatenate(cols, axis=1)


def _prep_body(r_ref, k_ref, v_ref, rp_ref, kp_ref, vp_ref, rn_ref, kn_ref, vn_ref,
               cr_ref, ck_ref, cv_ref, hw_ref, ha_ref, w2_ref, a2_ref, w0_ref, a0_ref, kkp_ref, kap_ref,
               ro_ref, ko_ref, vo_ref, kko_ref, lw_ref, bo_ref, kr_ref, *, tr, seq, lora):
    i = pl.program_id(0)
    first = (i * tr) % seq == 0
    last = ((i + 1) * tr) % seq == 0
    row = _iota(r_ref.shape, 0)

    def conv(x_ref, xp_ref, xn_ref, w_ref):
        x = x_ref[...]
        before = jnp.where(first, 0.0, xp_ref[7:8, :])
        after = jnp.where(last, 0.0, xn_ref[0:1, :])
        xm = jnp.where(row == 0, before, pltpu.roll(x, 1, axis=0))
        xp = jnp.where(row == tr - 1, after, pltpu.roll(x, tr - 1, axis=0))
        return xm * w_ref[0:1, :] + x * w_ref[1:2, :] + xp * w_ref[2:3, :]

    r = conv(r_ref, rp_ref, rn_ref, cr_ref)
    k = conv(k_ref, kp_ref, kn_ref, ck_ref)
    v = conv(v_ref, vp_ref, vn_ref, cv_ref)
    kk = k * kkp_ref[...]
    kk = kk * lax.rsqrt(_head_sum(kk * kk) + 1e-12)
    ro_ref[...] = r
    ko_ref[...] = k
    vo_ref[...] = v
    kko_ref[...] = kk
    for z in range(2):
        zs = slice(z * lora, (z + 1) * lora)
        w_raw = w0_ref[z:z + 1, :] + _dot(jnp.tanh(hw_ref[:, zs]).astype(BF16), w2_ref[z].astype(BF16))
        softplus_neg = jnp.maximum(-w_raw, 0.0) + jnp.log1p(jnp.exp(-jnp.abs(w_raw)))
        lw_ref[z] = -jnp.exp(-softplus_neg - 0.5)
        a = jax.nn.sigmoid(a0_ref[z:z + 1, :] + _dot(ha_ref[:, zs].astype(BF16), a2_ref[z].astype(BF16)))
        kr_ref[z] = k * (1.0 + (a - 1.0) * kap_ref[...])
        bo_ref[z] = kk * a


def _rwkv_prep(p, conv_w, w2, a2, w0, a0, kk_p, ka_p, seq, rcol, hcol, rw, lora):
    n = p.shape[0]
    tr = min(256, seq)
    nh = n // 8
    body = functools.partial(_prep_body, tr=tr, seq=seq, lora=lora)
    cur = lambda c: pl.BlockSpec((tr, rw), lambda i: (i, rcol + c))
    prv = lambda c: pl.BlockSpec((8, rw), lambda i: (jnp.maximum(i * (tr // 8) - 1, 0), rcol + c))
    nxt = lambda c: pl.BlockSpec((8, rw), lambda i: (jnp.minimum((i + 1) * (tr // 8), nh - 1), rcol + c))
    cw = lambda c: pl.BlockSpec((3, rw), lambda i: (0, c))
    full2 = lambda shape: pl.BlockSpec(shape, lambda i: (0,) * len(shape))
    shared = pl.BlockSpec((tr, rw), lambda i: (i, 0))
    directed = pl.BlockSpec((2, tr, rw), lambda i: (0, i, 0))
    return pl.pallas_call(
        body,
        grid=(n // tr,),
        in_specs=[cur(0), cur(1), cur(2), prv(0), prv(1), prv(2), nxt(0), nxt(1), nxt(2),
                  cw(0), cw(1), cw(2),
                  pl.BlockSpec((tr, 2 * lora), lambda i: (i, hcol)),
                  pl.BlockSpec((tr, 2 * lora), lambda i: (i, hcol + 1)),
                  full2((2, lora, rw)), full2((2, lora, rw)), full2((2, rw)), full2((2, rw)),
                  full2((1, rw)), full2((1, rw))],
        out_specs=[shared, shared, shared, shared, directed, directed, directed],
        out_shape=[jax.ShapeDtypeStruct((n, rw), F32)] * 4 + [jax.ShapeDtypeStruct((2, n, rw), F32)] * 3,
        compiler_params=_params("parallel"),
    )(p, p, p, p, p, p, p, p, p, conv_w, conv_w, conv_w, p, p, w2, a2, w0, a0, kk_p, ka_p)


def _mm(a, b, passes, kind="nn"):
    fn = {"nn": _dot, "nt": _dot_nt, "tn": _dot_tn}[kind]
    if passes == 6:
        return fn(a, b, precision=HIGHEST)
    a_hi, b_hi = a.astype(BF16), b.astype(BF16)
    if passes == 1:
        return fn(a_hi, b_hi)
    a_lo = (a - a_hi.astype(F32)).astype(BF16)
    b_lo = (b - b_hi.astype(F32)).astype(BF16)
    return fn(a_hi, b_hi) + (fn(a_lo, b_hi) + fn(a_hi, b_lo))


def _unit_tri_inverse(nmats, eye, same16, same32, passes):
    n16 = [jnp.where(same16, n, 0.0) for n in nmats]
    xs = [eye - n for n in n16]
    pw = n16
    for _ in range(3):
        pw = [_mm(p, p, passes) for p in pw]
        xs = [x + _mm(x, p, passes) for x, p in zip(xs, pw)]
    for mask in (same32 & ~same16, ~same32):
        offs = [jnp.where(mask, n, 0.0) for n in nmats]
        xo = [_mm(x, o, passes) for x, o in zip(xs, offs)]
        xs = [x - _mm(t, x, passes) for x, t in zip(xs, xo)]
    return xs


def _scan_body(r_ref, kk_ref, v_ref, lw_ref, b_ref, k_ref, s0_ref, y_ref, sf_ref, st_ref, *, heads, nchunks):
    z = pl.program_id(0)
    c = pl.program_id(3)
    C, K = SCAN_CHUNK, RWKV_HEAD
    pp = SCAN_PASSES

    @pl.when(c == 0)
    def _():
        st_ref[...] = s0_ref[...]

    ti, si = _iota((C, C), 0), _iota((C, C), 1)
    before = (si - ti) * (1 - 2 * z) < 0
    upto = before | (si == ti)
    eye = (si == ti).astype(F32)
    same16 = (ti // 16) == (si // 16)
    same32 = (ti // 32) == (si // 32)

    lw = lw_ref[...]
    lc = _dot(upto.astype(F32), lw, precision=HIGHEST)
    ltot = jnp.sum(lw, axis=0, keepdims=True)
    e_neg = jnp.exp(-lc)
    e_out = jnp.exp(ltot - lc)
    kkt = kk_ref[...] * jnp.exp(lc - lw)
    rt = r_ref[...] * jnp.exp(lc)
    bt = b_ref[...] * e_neg
    kt = k_ref[...] * e_neg
    bh = b_ref[...] * e_out
    kh = k_ref[...] * e_out
    etot = jnp.exp(ltot)
    v = v_ref[...]

    hr = range(heads)
    ls = [slice(h * K, (h + 1) * K) for h in hr]
    ps = [_mm(jnp.concatenate([kkt[:, s], rt[:, s]], axis=0),
              jnp.concatenate([bt[:, s], kt[:, s]], axis=0), pp["pair"], "nt") for s in ls]
    nmats = [jnp.where(before, p[:C, :C], 0.0) for p in ps]
    pkk = [jnp.where(before, p[:C, C:], 0.0) for p in ps]
    prb = [jnp.where(upto, p[C:, :C], 0.0) for p in ps]
    prk = [jnp.where(upto, p[C:, C:], 0.0) for p in ps]
    tinv = _unit_tri_inverse(nmats, eye, same16, same32, pp["inv"])
    tg = [_mm(tinv[h], jnp.concatenate([kkt[:, ls[h]], pkk[h]], axis=1), pp["solve"]) for h in hr]
    qa = [jnp.concatenate([rt[:, ls[h]], prk[h]], axis=1) - _mm(prb[h], tg[h], pp["solve"]) for h in hr]
    m3 = [_mm(bh[:, ls[h]], tg[h], pp["solve"], "tn") for h in hr]
    sv = [jnp.concatenate([st_ref[h], v[:, ls[h]]], axis=0) for h in hr]
    ys = [_mm(qa[h], sv[h], pp["state"]) for h in hr]
    for h in hr:
        decay_diag = eye * jnp.broadcast_to(etot[:, ls[h]], (K, K))
        trans = jnp.concatenate([decay_diag, jnp.zeros((K, C), F32)], axis=1) - m3[h]
        st_ref[h] = _mm(trans, sv[h], pp["state"]) + _mm(kh[:, ls[h]], v[:, ls[h]], pp["state"], "tn")
    y_ref[...] = ys[0] if heads == 1 else jnp.concatenate(ys, axis=1)

    @pl.when(c == nchunks - 1)
    def _():
        sf_ref[...] = st_ref[...]


def _rwkv_scan(r, kk, v, lw, bb, kr, s0, b, seq, rw):
    C, K = SCAN_CHUNK, RWKV_HEAD
    nchunks = seq // C
    heads = min(SCAN_HEADS, rw // K)
    ngroups = rw // (heads * K)
    n = b * seq
    row = lambda z, bi, hg, c: bi * nchunks + c + z * (nchunks - 1 - 2 * c)
    shared = pl.BlockSpec((C, heads * K), lambda z, bi, hg, c: (row(z, bi, hg, c), hg))
    directed = pl.BlockSpec((None, C, heads * K), lambda z, bi, hg, c: (z, row(z, bi, hg, c), hg))
    state = pl.BlockSpec((None, None, heads, K, K), lambda z, bi, hg, c: (z, bi, hg, 0, 0))
    body = functools.partial(_scan_body, heads=heads, nchunks=nchunks)
    return pl.pallas_call(
        body,
        grid=(2, b, ngroups, nchunks),
        in_specs=[shared, shared, shared, directed, directed, directed, state],
        out_specs=[directed, state],
        out_shape=[jax.ShapeDtypeStruct((2, n, rw), F32),
                   jax.ShapeDtypeStruct((2, b, rw // K, K, K), F32)],
        scratch_shapes=[pltpu.VMEM((heads, K, K), F32)],
        compiler_params=_params("parallel", "parallel", "parallel", "arbitrary"),
    )(r, kk, v, lw, bb, kr, s0)


def _rwkv_out_body(yf_ref, yb_ref, r_ref, k_ref, v_ref, g_ref, rk_ref, lnw_ref, lnb_ref, o_ref):
    y = yf_ref[...] + yb_ref[...]
    inv = 1.0 / RWKV_HEAD
    mu = _head_sum(y) * inv
    d = y - mu
    var = _head_sum(d * d) * inv
    yn = d * lax.rsqrt(var + RWKV_GN_EPS) * lnw_ref[...] + lnb_ref[...]
    bonus = _head_sum(r_ref[...] * k_ref[...] * rk_ref[...]) * v_ref[...]
    o_ref[...] = ((yn + bonus) * jax.nn.sigmoid(g_ref[...])).astype(o_ref.dtype)


def _rwkv_output(y, r, k, v, p, gcol, rk, ln_w, ln_b):
    n, rw = r.shape
    tr = min(256, n)
    shared = pl.BlockSpec((tr, rw), lambda i: (i, 0))
    vec = pl.BlockSpec((1, rw), lambda i: (0, 0))
    return pl.pallas_call(
        _rwkv_out_body,
        grid=(n // tr,),
        in_specs=[pl.BlockSpec((None, tr, rw), lambda i: (0, i, 0)),
                  pl.BlockSpec((None, tr, rw), lambda i: (1, i, 0)),
                  shared, shared, shared,
                  pl.BlockSpec((tr, rw), lambda i: (i, gcol)),
                  vec, vec, vec],
        out_specs=shared,
        out_shape=jax.ShapeDtypeStruct((n, rw), BF16),
        compiler_params=_params("parallel"),
    )(y, y, r, k, v, p, rk, ln_w, ln_b)


def _wout_body(a_ref, c_ref, r_ref, wa_ref, wc_ref, wr_ref, x_ref, g_ref, o_ref, *, tm, rows_per_mod, fixed_row):
    row = _mod_row(pl.program_id(0), tm, rows_per_mod, fixed_row)
    acc = _dot(a_ref[...], wa_ref[...]) + _dot(c_ref[...], wc_ref[...]) + _dot(r_ref[...], wr_ref[...])
    o_ref[...] = x_ref[...] + g_ref[pl.ds(row, 1), :] * acc


def _out_proj(attn, cmlp, rwkv, w_out, x2d, mod, layer, rows_per_mod, fixed_row):
    n, d = x2d.shape
    aq, cw, rw = attn.shape[1], cmlp.shape[1], rwkv.shape[1]
    tm = min(512, n)
    tn = min(1024, d)
    body = functools.partial(_wout_body, tm=tm, rows_per_mod=rows_per_mod, fixed_row=fixed_row)
    return pl.pallas_call(
        body,
        grid=(n // tm, d // tn),
        in_specs=[pl.BlockSpec((tm, aq), lambda i, j: (i, 0)),
                  pl.BlockSpec((tm, cw), lambda i, j: (i, 0)),
                  pl.BlockSpec((tm, rw), lambda i, j: (i, 0)),
                  pl.BlockSpec((None, aq, tn), lambda i, j: (layer, 0, j)),
                  pl.BlockSpec((None, cw, tn), lambda i, j: (layer, aq // cw, j)),
                  pl.BlockSpec((None, rw, tn), lambda i, j: (layer, (aq + cw) // rw, j)),
                  pl.BlockSpec((tm, tn), lambda i, j: (i, j)),
                  pl.BlockSpec((None, MOD_ROWS, tn), lambda i, j: (layer, 0, 2 * (d // tn) + j))],
        out_specs=pl.BlockSpec((tm, tn), lambda i, j: (i, j)),
        out_shape=jax.ShapeDtypeStruct((n, d), F32),
        compiler_params=_params("parallel", "parallel"),
    )(attn, cmlp, rwkv, w_out, w_out, w_out, x2d, mod)


def _router_body(x_ref, g_ref, sh_ref, sc_ref, rw_ref, rb_ref, zn_ref, ids_ref, wt_ref, *,
                 tm, rows_per_mod, fixed_row, experts):
    row = _mod_row(pl.program_id(0), tm, rows_per_mod, fixed_row)
    zn = _modulated_norm(x_ref[...], g_ref[...], sh_ref[pl.ds(row, 1), :], sc_ref[pl.ds(row, 1), :])
    bits = lax.bitcast_convert_type(zn.astype(BF16).astype(F32), jnp.uint32)
    half = zn.shape[1] // 2
    zn_ref[...] = (bits[:, :half] >> 16) | bits[:, half:]
    logits = _dot(zn, rw_ref[...], precision=HIGHEST).T
    per_group = experts // N_EXPERT_GROUPS
    scores = [jax.nn.sigmoid(logits[e:e + 1, :]) for e in range(experts)]
    sel = [scores[e] + rb_ref[e:e + 1, :] for e in range(experts)]
    best_val, best_grp = None, None
    for gi in range(N_EXPERT_GROUPS):
        mem = sel[gi * per_group:(gi + 1) * per_group]
        top2 = None
        for a in range(per_group):
            for b2 in range(a + 1, per_group):
                pair = mem[a] + mem[b2]
                top2 = pair if top2 is None else jnp.maximum(top2, pair)
        if gi == 0:
            best_val, best_grp = top2, jnp.zeros(top2.shape, jnp.int32)
        else:
            better = top2 > best_val
            best_grp = jnp.where(better, gi, best_grp)
            best_val = jnp.where(better, top2, best_val)
    chosen, picked = [], []
    for e in range(experts):
        gi = e // per_group
        rank = jnp.zeros(best_grp.shape, jnp.int32)
        for j in range(gi * per_group, (gi + 1) * per_group):
            if j != e:
                ahead = (sel[j] > sel[e]) | ((sel[j] == sel[e]) & (j < e))
                rank = rank + ahead.astype(jnp.int32)
        chosen.append((best_grp == gi) & (rank < TOP_K))
        picked.append(jnp.where(chosen[e], scores[e], 0.0))
    total = picked[0]
    for e in range(1, experts):
        total = total + picked[e]
    zero_i, zero_f = jnp.zeros(total.shape, jnp.int32), jnp.zeros(total.shape, F32)
    seen, ids, wts = zero_i, [zero_i, zero_i], [zero_f, zero_f]
    for e in range(experts):
        gate = picked[e] / total
        for slot in range(TOP_K):
            here = chosen[e] & (seen == slot)
            ids[slot] = jnp.where(here, e, ids[slot])
            wts[slot] = jnp.where(here, gate, wts[slot])
        seen = seen + chosen[e].astype(jnp.int32)
    ids_ref[...] = jnp.concatenate(ids + [jnp.zeros((8 - TOP_K, tm), jnp.int32)], axis=0)
    wt_ref[...] = jnp.concatenate(wts + [jnp.zeros((128 - TOP_K, tm), F32)], axis=0).T


def _router(x2d, g, mod, router_w_pad, router_b_col, layer, rows_per_mod, fixed_row, experts):
    n, d = x2d.shape
    tm = min(256, n)
    body = functools.partial(_router_body, tm=tm, rows_per_mod=rows_per_mod, fixed_row=fixed_row, experts=experts)
    return pl.pallas_call(
        body,
        grid=(n // tm,),
        in_specs=[pl.BlockSpec((tm, d), lambda i: (i, 0)),
                  pl.BlockSpec((None, 1, d), lambda i: (layer, 0, 0)),
                  pl.BlockSpec((None, MOD_ROWS, d), lambda i: (layer, 0, 3)),
                  pl.BlockSpec((None, MOD_ROWS, d), lambda i: (layer, 0, 4)),
                  pl.BlockSpec((d, 128), lambda i: (0, 0)),
                  pl.BlockSpec((128, 1), lambda i: (0, 0))],
        out_specs=[pl.BlockSpec((tm, d // 2), lambda i: (i, 0)),
                   pl.BlockSpec((8, tm), lambda i: (0, i)),
                   pl.BlockSpec((tm, 128), lambda i: (i, 0))],
        out_shape=[jax.ShapeDtypeStruct((n, d // 2), jnp.uint32), jax.ShapeDtypeStruct((8, n), jnp.int32),
                   jax.ShapeDtypeStruct((n, 128), F32)],
        compiler_params=_params("parallel"),
    )(x2d, g, mod, mod, router_w_pad, router_b_col)


def _route_plan(ids, experts, tm):
    n = ids.shape[1]
    total = TOP_K * n + experts * tm
    flat = ids.reshape(-1)
    onehot = (flat[:, None] == jnp.arange(experts, dtype=jnp.int32)[None, :]).astype(jnp.int32)
    rank = jnp.cumsum(onehot, axis=0) - onehot
    padded = ((jnp.sum(onehot, axis=0) + tm - 1) // tm) * tm
    ends = jnp.cumsum(padded)
    pos = (ends - padded)[flat] + jnp.sum(rank * onehot, axis=1)
    row_token = jnp.zeros((total,), jnp.int32).at[pos].set(jnp.tile(jnp.arange(n, dtype=jnp.int32), TOP_K))
    tile_start = jnp.arange(total // tm, dtype=jnp.int32) * tm
    tile_used = (tile_start < ends[-1]).astype(jnp.int32)
    tile_expert = jnp.minimum(jnp.searchsorted(ends, tile_start, side="right"), experts - 1).astype(jnp.int32)
    last_used = tile_expert[jnp.maximum(ends[-1] // tm - 1, 0)]
    tile_expert = jnp.where(tile_used == 1, tile_expert, last_used)
    return pos.astype(jnp.int32), row_token, tile_expert, tile_used


def _row_copy(src_hbm, row, dst_ref, i, sem):
    return pltpu.make_async_copy(src_hbm.at[pl.ds(row, 1), :], dst_ref.at[pl.ds(i, 1), :], sem)


def _start_rows(src_hbm, dst_ref, sem, index_of, priorities):
    k = len(priorities)

    def start(i, carry):
        for j, prio in enumerate(priorities):
            _row_copy(src_hbm, index_of(i * k + j), dst_ref, i * k + j, sem).start(priority=prio)
        return carry

    lax.fori_loop(0, dst_ref.shape[0] // k, start, 0, unroll=8 // k)


def _wait_rows(src_hbm, dst_ref, sem):
    pltpu.make_async_copy(src_hbm.at[pl.ds(0, dst_ref.shape[0]), :], dst_ref, sem).wait()


def _expert_hidden_body(te_ref, used_ref, tok_ref, zn_hbm, w1_ref, w3_ref, h_ref, xg_ref, xb_ref, sem, *,
                        tm, ntiles):
    p, f = pl.program_id(0), pl.program_id(1)
    used = used_ref[p] == 1
    slot = p % 2
    rows_of = lambda tile: (lambda i: tok_ref[tile * tm + i])

    @pl.when(used & (f == 0) & (p == 0))
    def _():
        _start_rows(zn_hbm, xg_ref.at[0], sem.at[0], rows_of(0), GATHER_PRIORITY)

    @pl.when(used & (f == 0))
    def _():
        _wait_rows(zn_hbm, xg_ref.at[slot], sem.at[slot])
        words = xg_ref[slot]
        half = words.shape[1]
        xb_ref[:, :half] = lax.bitcast_convert_type(words << 16, F32).astype(BF16)
        xb_ref[:, half:] = lax.bitcast_convert_type(words & jnp.uint32(0xFFFF0000), F32).astype(BF16)

        @pl.when((p + 1 < ntiles) & (used_ref[jnp.minimum(p + 1, ntiles - 1)] == 1))
        def _():
            _start_rows(zn_hbm, xg_ref.at[1 - slot], sem.at[1 - slot], rows_of(p + 1), GATHER_PRIORITY)

    @pl.when(used)
    def _():
        x = xb_ref[...]
        h1 = _dot(x, w1_ref[...].astype(BF16))
        h3 = _dot(x, w3_ref[...].astype(BF16))
        h_ref[...] = ((h1 * jax.nn.sigmoid(h1)) * h3).astype(h_ref.dtype)

    @pl.when(jnp.logical_not(used))
    def _():
        h_ref[...] = jnp.zeros_like(h_ref)


def _pack_pairs(y):
    bits = lax.bitcast_convert_type(y.astype(BF16).astype(F32), jnp.uint32)
    half = y.shape[1] // 2
    return (bits[:, :half] >> 16) | bits[:, half:]


def _unpack_pairs(words, tile):
    half = tile // 2
    cols = []
    for j in range(words.shape[1] // half):
        w = words[:, j * half:(j + 1) * half]
        cols.append(lax.bitcast_convert_type(w << 16, F32))
        cols.append(lax.bitcast_convert_type(w & jnp.uint32(0xFFFF0000), F32))
    return jnp.concatenate(cols, axis=1)


def _expert_out_body(te_ref, used_ref, h_ref, w2_ref, o_ref):
    o_ref[...] = _pack_pairs(_dot(h_ref[...], w2_ref[...].astype(BF16)))


def _experts(zn, row_token, tile_expert, tile_used, w1, w3, w2, layer, tm):
    d, ff = w1.shape[2], w1.shape[3]
    total = row_token.shape[0]
    ntiles = total // tm
    tf = min(256, ff)
    n_f = ff // tf
    tn = min(EXPERT_OUT_TILE, d)
    n_j = d // tn
    hold = lambda used, p, j, last: jnp.where(used[p] == 1, j, last)
    w_spec = pl.BlockSpec((None, None, d, tf),
                          lambda p, f, te, us, tok: (layer, te[p], 0, hold(us, p, f, n_f - 1)))
    hidden = pl.pallas_call(
        functools.partial(_expert_hidden_body, tm=tm, ntiles=ntiles),
        grid_spec=pltpu.PrefetchScalarGridSpec(
            num_scalar_prefetch=3,
            grid=(ntiles, n_f),
            in_specs=[pl.BlockSpec(memory_space=pl.ANY), w_spec, w_spec],
            out_specs=pl.BlockSpec((tm, tf), lambda p, f, te, us, tok: (p, f)),
            scratch_shapes=[pltpu.VMEM((2, tm, d // 2), jnp.uint32), pltpu.VMEM((tm, d), BF16),
                            pltpu.SemaphoreType.DMA((2,))]),
        out_shape=jax.ShapeDtypeStruct((total, ff), BF16),
        compiler_params=_params("arbitrary", "arbitrary", disable_bounds_checks=True),
    )(tile_expert, tile_used, row_token, zn, w1, w3)
    return pl.pallas_call(
        _expert_out_body,
        grid_spec=pltpu.PrefetchScalarGridSpec(
            num_scalar_prefetch=2,
            grid=(n_j, ntiles),
            in_specs=[pl.BlockSpec((tm, ff), lambda j, p, te, us: (p, 0)),
                      pl.BlockSpec((None, None, ff, tn), lambda j, p, te, us: (layer, te[p], 0, j))],
            out_specs=pl.BlockSpec((tm, tn // 2), lambda j, p, te, us: (p, j))),
        out_shape=jax.ShapeDtypeStruct((total, d // 2), jnp.uint32),
        compiler_params=_params("parallel", "parallel"),
    )(tile_expert, tile_used, hidden, w2)


def _combine_body(pos_ref, ys_hbm, wt_ref, x_ref, g2_ref, fg_ref, o_ref, ya_ref, yb_ref, sem_a, sem_b, *,
                  tm, n, rows_per_mod, fixed_row, final_norm):
    i = pl.program_id(0)
    slot = i % 2

    def start(tile, s):
        _start_rows(ys_hbm, ya_ref.at[s], sem_a.at[s], lambda r: pos_ref[tile * tm + r], COMBINE_PRIORITY)
        _start_rows(ys_hbm, yb_ref.at[s], sem_b.at[s], lambda r: pos_ref[n + tile * tm + r], COMBINE_PRIORITY)

    @pl.when(i == 0)
    def _():
        start(0, 0)

    @pl.when(i + 1 < n // tm)
    def _():
        start(i + 1, 1 - slot)

    _wait_rows(ys_hbm, ya_ref.at[slot], sem_a.at[slot])
    _wait_rows(ys_hbm, yb_ref.at[slot], sem_b.at[slot])
    row = _mod_row(i, tm, rows_per_mod, fixed_row)
    wt = wt_ref[...]
    tile = min(EXPERT_OUT_TILE, x_ref.shape[1])
    mix = wt[:, 0:1] * _unpack_pairs(ya_ref[slot], tile) + wt[:, 1:2] * _unpack_pairs(yb_ref[slot], tile)
    out = x_ref[...] + g2_ref[pl.ds(row, 1), :] * mix
    if final_norm:
        out = out * lax.rsqrt(jnp.mean(out * out, axis=-1, keepdims=True) + EPS) * fg_ref[...]
    o_ref[...] = out


def _moe_combine(ys, pos, wts, x2d, mod, final_g, layer, rows_per_mod, fixed_row, final_norm):
    n, d = x2d.shape
    tm = min(256, n)
    body = functools.partial(_combine_body, tm=tm, n=n, rows_per_mod=rows_per_mod, fixed_row=fixed_row,
                             final_norm=final_norm)
    return pl.pallas_call(
        body,
        grid_spec=pltpu.PrefetchScalarGridSpec(
            num_scalar_prefetch=1,
            grid=(n // tm,),
            in_specs=[pl.BlockSpec(memory_space=pl.ANY),
                      pl.BlockSpec((tm, 128), lambda i, ps: (i, 0)),
                      pl.BlockSpec((tm, d), lambda i, ps: (i, 0)),
                      pl.BlockSpec((None, MOD_ROWS, d), lambda i, ps: (layer, 0, 5)),
                      pl.BlockSpec((1, d), lambda i, ps: (0, 0))],
            out_specs=pl.BlockSpec((tm, d), lambda i, ps: (i, 0)),
            scratch_shapes=[pltpu.VMEM((2, tm, d // 2), jnp.uint32), pltpu.VMEM((2, tm, d // 2), jnp.uint32),
                            pltpu.SemaphoreType.DMA((2,)), pltpu.SemaphoreType.DMA((2,))]),
        out_shape=jax.ShapeDtypeStruct((n, d), F32),
        compiler_params=_params("arbitrary", disable_bounds_checks=True),
    )(pos, ys, wts, x2d, mod, final_g)


def _moe(x2d, g, mod, router_w_pad, router_b_col, w1, w3, w2, final_g, layer, experts,
         rows_per_mod, fixed_row, final_norm=False):
    n = x2d.shape[0]
    tm = 512 if n >= 8 * 512 else 128
    zn, ids, wts = _router(x2d, g, mod, router_w_pad, router_b_col, layer, rows_per_mod, fixed_row, experts)
    pos, row_token, tile_expert, tile_used = _route_plan(ids[:TOP_K], experts, tm)
    ys = _experts(zn, row_token, tile_expert, tile_used, w1, w3, w2, layer, tm)
    return _moe_combine(ys, pos, wts, x2d, mod, final_g, layer, rows_per_mod, fixed_row, final_norm)


def _rope_tables(t):
    pos = jnp.arange(t)
    half = HEAD_DIM // 4
    freqs = ROPE_BASE ** (-jnp.arange(half, dtype=F32) / half)
    ang_r = (pos // GRID_W).astype(F32)[:, None] * freqs[None, :]
    ang_c = (pos % GRID_W).astype(F32)[:, None] * freqs[None, :]
    cos_t = jnp.concatenate([jnp.cos(ang_r)] * 2 + [jnp.cos(ang_c)] * 2, axis=1)
    sin_t = jnp.concatenate([-jnp.sin(ang_r), jnp.sin(ang_r), -jnp.sin(ang_c), jnp.sin(ang_c)], axis=1)
    return cos_t, sin_t


def kernel(x, c, ctx, c_ctx, ada_w, ada_b, norm1_g, w_in, rwkv_conv, attn_sink, cmlp_norm_g, cmlp_ws, cmlp_b,
           rwkv_w0, rwkv_w1, rwkv_w2, rwkv_a0, rwkv_a1, rwkv_a2, rwkv_kk, rwkv_ka, rwkv_rk, rwkv_ln_w, rwkv_ln_b,
           w_out, norm2_g, router_w, router_b, moe_w1, moe_w3, moe_w2, final_g):
    b, t, d = x.shape
    l = ctx.shape[1]
    depth = ada_w.shape[0]
    cw = cmlp_norm_g.shape[1]
    rw = rwkv_w0.shape[2]
    lora = rwkv_w1.shape[3]
    experts = router_w.shape[1]
    dp = w_in.shape[2]
    akv = KV_HEADS * HEAD_DIM
    aq = dp - 2 * akv - 2 * cw - 4 * rw
    group = aq // akv
    heads = rw // RWKV_HEAD
    ucol = (aq + 2 * akv) // cw
    rcol = (aq + 2 * akv + 2 * cw) // rw
    gcol = rcol + 3
    hcol = dp // (2 * lora)
    assert (aq + 2 * akv) % cw == 0 and (aq + 2 * akv + 2 * cw) % rw == 0 and dp % (2 * lora) == 0
    assert b + 1 <= MOD_ROWS and (b * t) % l == 0 and t % 256 == 0 and l % 128 == 0

    cpad = jnp.zeros((MOD_ROWS, d), F32).at[:b].set(c).at[b].set(c_ctx)
    mod = _ada(cpad, ada_w, ada_b)
    cos_t, sin_t = _rope_tables(t)
    router_w_pad = jnp.zeros((d, 128), F32).at[:, :experts].set(router_w)
    router_b_col = jnp.zeros((128, 1), F32).at[:experts, 0].set(router_b)
    w_out_bf = w_out.astype(BF16)
    w_in_bf = w_in.astype(BF16)
    w_lora_bf = jnp.concatenate([rwkv_w1[:, 0], rwkv_w1[:, 1], rwkv_a1[:, 0], rwkv_a1[:, 1]], axis=2).astype(BF16)
    assert dp % w_lora_bf.shape[2] == 0
    s_zero = jnp.zeros((2, b, heads, RWKV_HEAD, RWKV_HEAD), F32)

    xs = x.reshape(b * t, d)
    hs = ctx.reshape(b * l, d)
    for layer in range(depth):
        lat = dict(rows_per_mod=t, fixed_row=None)
        con = dict(rows_per_mod=None, fixed_row=b)
        last = layer == depth - 1
        px = _proj(xs, norm1_g.reshape(depth, 1, d), mod, w_in_bf, w_lora_bf, layer, **lat)
        pc = _proj(hs, norm1_g.reshape(depth, 1, d), mod, w_in_bf, w_lora_bf, layer, **con)

        sink = attn_sink[layer]
        sink_col = jnp.repeat(sink.reshape(KV_HEADS, group), ATTN_BLOCK, axis=1)[..., None]
        attn_x = _latent_attention(px, pc, cos_t, sin_t, sink_col, b, t, l, aq, akv)

        bs_b = jnp.broadcast_to(cmlp_b[layer][:, :, None], cmlp_b.shape[1:] + (CMLP_CH,))
        cmlp_x = _chunk_mlp(px, cmlp_norm_g[layer][None], cmlp_ws[layer], bs_b, ucol, cw)

        prep_args = (rwkv_conv[layer], rwkv_w2[layer], rwkv_a2[layer], rwkv_w0[layer], rwkv_a0[layer],
                     rwkv_kk[layer][None], rwkv_ka[layer][None])
        rc, kc, vc, kkc, lwc, bbc, krc = _rwkv_prep(pc, *prep_args, l, rcol, hcol, rw, lora)
        rx, kx, vx, kkx, lwx, bbx, krx = _rwkv_prep(px, *prep_args, t, rcol, hcol, rw, lora)
        y_c, s_ctx = _rwkv_scan(rc, kkc, vc, lwc, bbc, krc, s_zero, b, l, rw)
        y_x, _ = _rwkv_scan(rx, kkx, vx, lwx, bbx, krx, s_ctx, b, t, rw)
        out_args = (rwkv_rk[layer][None], rwkv_ln_w[layer][None], rwkv_ln_b[layer][None])
        rwkv_x = _rwkv_output(y_x, rx, kx, vx, px, gcol, *out_args)

        xs = _out_proj(attn_x, cmlp_x, rwkv_x, w_out_bf, xs, mod, layer, **lat)
        moe_args = (norm2_g.reshape(depth, 1, d), mod, router_w_pad, router_b_col, moe_w1, moe_w3, moe_w2,
                    final_g[None], layer, experts)
        xs = _moe(xs, *moe_args, final_norm=last, **lat)

        if not last:
            sink_rows = jnp.broadcast_to(sink[:, None, None], (KV_HEADS * group, l, 1))
            attn_c = _context_attention(pc, sink_rows, b, l, aq, akv)
            cmlp_c = _chunk_mlp(pc, cmlp_norm_g[layer][None], cmlp_ws[layer], bs_b, ucol, cw)
            rwkv_c = _rwkv_output(y_c, rc, kc, vc, pc, gcol, *out_args)
            hs = _out_proj(attn_c, cmlp_c, rwkv_c, w_out_bf, hs, mod, layer, **con)
            hs = _moe(hs, *moe_args, **con)
    return xs.reshape(b, t, d)
```
